```python
import math
import jax, jax.numpy as jnp
from jax import lax
import numpy as np

D_MODEL = 1024
BATCH = 2
SEQ = 16384
DEPTH = 4
DEC_BATCH = 1
DEC_SEQ = 16384
PAST_LEN = 128

HY_WIDTH = D_MODEL // 2
HY_GROUPS = 8
HY_ORDER = 2
HY_PROJ = (HY_ORDER + 1) * HY_WIDTH
HY_BANDS = 16
HY_EMB = 2 * HY_BANDS + 1
HY_FFN = 64
HY_TARGET = 1e-2
HY_FAST = 0.3
HY_SLOW = 1.5
MLA_HEADS = 8
QK_NOPE = 64
QK_ROPE = 32
V_HEAD = 64
Q_RANK = 256
KV_RANK = 128
ATTN_WIDTH = MLA_HEADS * V_HEAD
ROPE_BASE = 10000.0
Q_BLOCK = 128
IN_WIDTH = HY_PROJ + Q_RANK + KV_RANK + QK_ROPE
MIX_WIDTH = HY_WIDTH + ATTN_WIDTH
D_FF = 4 * D_MODEL
EPS = 1e-6

kernel_name = 'hyena_mla_parallel_encoder'


def rms_norm(x, g):
    xf = x.astype(jnp.float32)
    y = xf * lax.rsqrt(jnp.mean(xf * xf, axis=-1, keepdims=True) + EPS)
    return (y * g.astype(jnp.float32)).astype(x.dtype)


def group_rms_norm(x, g, n_groups):
    shp = x.shape
    xf = x.astype(jnp.float32).reshape(shp[:-1] + (n_groups, shp[-1] // n_groups))
    y = xf * lax.rsqrt(jnp.mean(xf * xf, axis=-1, keepdims=True) + EPS)
    return (y.reshape(shp) * g.astype(jnp.float32)).astype(x.dtype)


def hyena_filters(L, w1, b1, w2, b2, w3, sin_freq, decay):
    f32 = jnp.float32
    t = jnp.linspace(0.0, 1.0, L, dtype=f32)[:, None]
    omega = (2.0 * math.pi / L) * jnp.arange(L, dtype=f32)
    bands = jnp.linspace(1e-4, HY_BANDS - 1, HY_BANDS, dtype=f32)
    phase = omega[:, None] * bands[None, :]
    z = jnp.concatenate([t, jnp.cos(phase), -jnp.sin(phase)], axis=-1)
    sf = sin_freq.astype(f32)
    h = jnp.sin(sf[0] * (z @ w1.astype(f32) + b1.astype(f32)))
    h = jnp.sin(sf[1] * (h @ w2.astype(f32) + b2.astype(f32)))
    h = (h @ w3.astype(f32)).reshape(L, HY_ORDER, 2, HY_WIDTH)
    h = h * jnp.exp(-t[:, :, None, None] * jnp.abs(decay.astype(f32)))
    fwd = h[:, :, 0]
    bwd = h[:, :, 1]
    k_circ = jnp.concatenate([fwd, jnp.zeros_like(fwd[:1]), bwd[:0:-1]], axis=0)
    return jnp.fft.rfft(k_circ, axis=0)


def long_conv(z, kf):
    L = z.shape[1]
    zf = jnp.fft.rfft(z.astype(jnp.float32), n=2 * L, axis=1)
    return jnp.fft.irfft(zf * kf[None], n=2 * L, axis=1)[:, :L]


def hyena_mixer(u, conv_w, conv_b, w1, b1, w2, b2, w3, sin_freq, decay, hy_bias):
    L = u.shape[1]
    up = jnp.pad(u, ((0, 0), (1, 1), (0, 0)))
    u = up[:, :-2] * conv_w[0] + up[:, 1:-1] * conv_w[1] + up[:, 2:] * conv_w[2] + conv_b
    v, x1, x2 = jnp.split(u, 3, axis=-1)
    kf = hyena_filters(L, w1, b1, w2, b2, w3, sin_freq, decay)
    z = v
    for n, gate in enumerate((x1, x2)):
        zf = z.astype(jnp.float32)
        conv = long_conv(zf, kf[:, n]) + zf * hy_bias[n].astype(jnp.float32)
        z = (gate.astype(jnp.float32) * conv).astype(u.dtype)
    return z


def rope_tables(L):
    inv = 1.0 / (ROPE_BASE ** (jnp.arange(0, QK_ROPE, 2, dtype=jnp.float32) / QK_ROPE))
    ang = jnp.arange(L, dtype=jnp.float32)[:, None] * inv[None, :]
    return jnp.cos(ang)[:, None, :], jnp.sin(ang)[:, None, :]


def apply_rope(x, cos, sin):
    half = x.shape[-1] // 2
    x1, x2 = x[..., :half], x[..., half:]
    return jnp.concatenate([x1 * cos - x2 * sin, x2 * cos + x1 * sin], axis=-1).astype(x.dtype)


def mla_mixer(p, q_norm, w_uq, kv_norm, w_ukv):
    B, L, _ = p.shape
    H = MLA_HEADS
    dq = QK_NOPE + QK_ROPE
    c_q = p[..., :Q_RANK]
    c_kv = p[..., Q_RANK:Q_RANK + KV_RANK]
    k_pe = p[..., Q_RANK + KV_RANK:]
    q = (rms_norm(c_q, q_norm) @ w_uq).reshape(B, L, H, dq)
    kv = (rms_norm(c_kv, kv_norm) @ w_ukv).reshape(B, L, H, QK_NOPE + V_HEAD)
    cos, sin = rope_tables(L)
    q = jnp.concatenate([q[..., :QK_NOPE], apply_rope(q[..., QK_NOPE:], cos, sin)], axis=-1)
    k_pe = apply_rope(k_pe[:, :, None, :], cos, sin)
    k = jnp.concatenate([kv[..., :QK_NOPE], jnp.broadcast_to(k_pe, (B, L, H, QK_ROPE))], axis=-1)
    v = kv[..., QK_NOPE:]
    q = q.transpose(0, 2, 1, 3)
    k = k.transpose(0, 2, 1, 3)
    v = v.transpose(0, 2, 1, 3)
    scale = dq ** -0.5
    nblk = L // Q_BLOCK
    qb = q.reshape(B, H, nblk, Q_BLOCK, dq).transpose(2, 0, 1, 3, 4)

    def attend(qblk):
        s = jnp.einsum('bhqd,bhkd->bhqk', qblk, k).astype(jnp.float32) * scale
        pr = jax.nn.softmax(s, axis=-1).astype(v.dtype)
        return jnp.einsum('bhqk,bhkd->bhqd', pr, v)

    o = lax.map(attend, qb)
    return o.transpose(1, 0, 3, 2, 4).reshape(B, L, H * V_HEAD)


def encoder_layer(x, w_in, hy_conv_w, hy_conv_b, hy_ffn_w1, hy_ffn_b1, hy_ffn_w2, hy_ffn_b2,
                  hy_ffn_w3, hy_sin_freq, hy_decay, hy_bias, mla_q_norm, mla_w_uq, mla_kv_norm,
                  mla_w_ukv, grp_norm_hy, grp_norm_attn, w_out, norm_mix_pre, norm_mix_post,
                  norm_mlp_pre, norm_mlp_post, w_mlp_up, w_mlp_down):
    h = rms_norm(x, norm_mix_pre)
    proj = h @ w_in
    y_hy = hyena_mixer(proj[..., :HY_PROJ], hy_conv_w, hy_conv_b, hy_ffn_w1, hy_ffn_b1,
                       hy_ffn_w2, hy_ffn_b2, hy_ffn_w3, hy_sin_freq, hy_decay, hy_bias)
    y_at = mla_mixer(proj[..., HY_PROJ:], mla_q_norm, mla_w_uq, mla_kv_norm, mla_w_ukv)
    mix = jnp.concatenate([group_rms_norm(y_hy, grp_norm_hy, HY_GROUPS),
                           group_rms_norm(y_at, grp_norm_attn, MLA_HEADS)], axis=-1) @ w_out
    x = x + rms_norm(mix, norm_mix_post)
    h = rms_norm(x, norm_mlp_pre)
    m = jnp.square(jax.nn.relu(h @ w_mlp_up)) @ w_mlp_down
    return x + rms_norm(m, norm_mlp_post)


def setup_inputs(seed: int = 0) -> dict:
    key = jax.random.key(seed)
    ks = jax.random.split(key, 26)
    f32 = jnp.float32

    def nrm(k, shape, scale):
        return jax.random.normal(k, shape, f32) * scale

    def gain(k, shape):
        return 1.0 + 0.02 * jax.random.normal(k, shape, f32)

    decay_base = jnp.abs(jnp.linspace(math.log(HY_TARGET) / HY_SLOW, math.log(HY_TARGET) / HY_FAST,
                                      HY_WIDTH, dtype=f32))
    return {
        'x_prompt': nrm(ks[0], (BATCH, SEQ, D_MODEL), 1.0),
        'x_sample': nrm(ks[1], (DEC_BATCH, DEC_SEQ, D_MODEL), 1.0),
        'w_in': nrm(ks[2], (DEPTH, D_MODEL, IN_WIDTH), D_MODEL ** -0.5),
        'hy_conv_w': nrm(ks[3], (DEPTH, 3, HY_PROJ), 3 ** -0.5),
        'hy_conv_b': nrm(ks[4], (DEPTH, HY_PROJ), 0.02),
        'hy_ffn_w1': nrm(ks[5], (DEPTH, HY_EMB, HY_FFN), HY_EMB ** -0.5),
        'hy_ffn_b1': nrm(ks[6], (DEPTH, HY_FFN), 0.02),
        'hy_ffn_w2': nrm(ks[7], (DEPTH, HY_FFN, HY_FFN), HY_FFN ** -0.5),
        'hy_ffn_b2': nrm(ks[8], (DEPTH, HY_FFN), 0.02),
        'hy_ffn_w3': nrm(ks[9], (DEPTH, HY_FFN, HY_ORDER * 2 * HY_WIDTH), HY_FFN ** -0.5),
        'hy_sin_freq': gain(ks[10], (DEPTH, 2, HY_FFN)),
        'hy_decay': decay_base * (1.0 + 0.05 * jax.random.normal(ks[11], (DEPTH, HY_ORDER, 2, HY_WIDTH), f32)),
        'hy_bias': nrm(ks[12], (DEPTH, HY_ORDER, HY_WIDTH), 1.0),
        'mla_q_norm': gain(ks[13], (DEPTH, Q_RANK)),
        'mla_w_uq': nrm(ks[14], (DEPTH, Q_RANK, MLA_HEADS * (QK_NOPE + QK_ROPE)), Q_RANK ** -0.5),
        'mla_kv_norm': gain(ks[15], (DEPTH, KV_RANK)),
        'mla_w_ukv': nrm(ks[16], (DEPTH, KV_RANK, MLA_HEADS * (QK_NOPE + V_HEAD)), KV_RANK ** -0.5),
        'grp_norm_hy': gain(ks[17], (DEPTH, HY_WIDTH)),
        'grp_norm_attn': gain(ks[18], (DEPTH, ATTN_WIDTH)),
        'w_out': nrm(ks[19], (DEPTH, MIX_WIDTH, D_MODEL), MIX_WIDTH ** -0.5),
        'norm_mix_pre': gain(ks[20], (DEPTH, D_MODEL)),
        'norm_mix_post': gain(ks[21], (DEPTH, D_MODEL)),
        'norm_mlp_pre': gain(ks[22], (DEPTH, D_MODEL)),
        'norm_mlp_post': gain(ks[23], (DEPTH, D_MODEL)),
        'w_mlp_up': nrm(ks[24], (DEPTH, D_MODEL, D_FF), D_MODEL ** -0.5),
        'w_mlp_down': nrm(ks[25], (DEPTH, D_FF, D_MODEL), D_FF ** -0.5),
    }


def reference(x_prompt, x_sample, w_in, hy_conv_w, hy_conv_b, hy_ffn_w1, hy_ffn_b1, hy_ffn_w2,
              hy_ffn_b2, hy_ffn_w3, hy_sin_freq, hy_decay, hy_bias, mla_q_norm, mla_w_uq,
              mla_kv_norm, mla_w_ukv, grp_norm_hy, grp_norm_attn, w_out, norm_mix_pre,
              norm_mix_post, norm_mlp_pre, norm_mlp_post, w_mlp_up, w_mlp_down):
    params = (w_in, hy_conv_w, hy_conv_b, hy_ffn_w1, hy_ffn_b1, hy_ffn_w2, hy_ffn_b2, hy_ffn_w3,
              hy_sin_freq, hy_decay, hy_bias, mla_q_norm, mla_w_uq, mla_kv_norm, mla_w_ukv,
              grp_norm_hy, grp_norm_attn, w_out, norm_mix_pre, norm_mix_post, norm_mlp_pre,
              norm_mlp_post, w_mlp_up, w_mlp_down)
    y_prompt = x_prompt
    y_sample = x_sample
    for i in range(DEPTH):
        layer_p = [p[i] for p in params]
        y_prompt = encoder_layer(y_prompt, *layer_p)
        y_sample = encoder_layer(y_sample, *layer_p)
    return (y_prompt, y_sample)
```

```python
import functools
import math

import jax
import jax.numpy as jnp
from jax import lax
from jax.experimental import pallas as pl
from jax.experimental.pallas import tpu as pltpu

F32 = jnp.float32
BF16 = jnp.bfloat16

EPS = 1e-6
HY_WIDTH = 512
HY_GROUPS = 8
HY_ORDER = 2
HY_BANDS = 16
HY_EMB = 2 * HY_BANDS + 1
HY_EMB_PAD = 40
HY_FFN = 64
MLA_HEADS = 8
QK_NOPE = 64
QK_ROPE = 32
V_HEAD = 64
Q_RANK = 256
KV_RANK = 128
ROPE_BASE = 10000.0
HEAD_PAD = 128
ATTN_WIDTH = MLA_HEADS * V_HEAD

DFT_N2 = 128
HALO = 16
VMEM_LIMIT = 56 * 1024 * 1024


def _cparams(sem):
    return pltpu.CompilerParams(dimension_semantics=sem, vmem_limit_bytes=VMEM_LIMIT)


def _dot(a, b):
    return jnp.dot(a, b, preferred_element_type=F32)


def _dot_nt(a, b):
    return lax.dot_general(a, b, (((1,), (1,)), ((), ())), preferred_element_type=F32)


def _rms(x, g):
    return x * lax.rsqrt(jnp.mean(x * x, axis=-1, keepdims=True) + EPS) * g


def _inproj_kernel(xp_ref, x_ref, xn_ref, gpre_ref, win_ref, cw_ref, cb_ref, qg_ref, wqa_ref,
                   wqb_ref, kvg_ref, wkt_ref, pkt_ref, wv_ref, vone_ref, qcos_ref, qsin_ref,
                   kcs_ref, v_out, x1_out, x2_out, q_out, kt_out, vv_out, h_scr, pe_scr,
                   *, tm, tiles_per_seq, scale):
    i = pl.program_id(0)
    t_idx = i % tiles_per_seq
    g = gpre_ref[...]
    h_scr[0:HALO, :] = _rms(xp_ref[...], g).astype(BF16)
    h_scr[HALO:HALO + tm, :] = _rms(x_ref[...], g).astype(BF16)
    h_scr[HALO + tm:2 * HALO + tm, :] = _rms(xn_ref[...], g).astype(BF16)
    proj = _dot(h_scr[...], win_ref[...])

    hw3 = 3 * HY_WIDTH
    pe_scr[...] = proj[:, :hw3]
    row = lax.broadcasted_iota(jnp.int32, (tm, 1), 0)
    prev = pe_scr[HALO - 1:HALO - 1 + tm, :]
    cur = pe_scr[HALO:HALO + tm, :]
    nxt = pe_scr[HALO + 1:HALO + 1 + tm, :]
    prev = jnp.where(jnp.logical_and(row == 0, t_idx == 0), 0.0, prev)
    nxt = jnp.where(jnp.logical_and(row == tm - 1, t_idx == tiles_per_seq - 1), 0.0, nxt)
    u = prev * cw_ref[0:1, :] + cur * cw_ref[1:2, :] + nxt * cw_ref[2:3, :] + cb_ref[...]
    v_out[...] = u[:, :HY_WIDTH]
    x1_out[...] = u[:, HY_WIDTH:2 * HY_WIDTH]
    x2_out[...] = u[:, 2 * HY_WIDTH:]

    core = proj[HALO:HALO + tm, :]
    cq = core[:, hw3:hw3 + Q_RANK]
    cqn = _rms(cq, qg_ref[...]).astype(BF16)
    qa = _dot(cqn, wqa_ref[...])
    qb = _dot(cqn, wqb_ref[...])
    qcos = qcos_ref[...]
    qsin = qsin_ref[...]
    for h in range(MLA_HEADS):
        sl = slice(h * HEAD_PAD, (h + 1) * HEAD_PAD)
        q_out[h] = ((qa[:, sl] * qcos + qb[:, sl] * qsin) * scale).astype(BF16)

    ckv = core[:, hw3 + Q_RANK:hw3 + Q_RANK + KV_RANK]
    ckvn = _rms(ckv, kvg_ref[...]).astype(BF16)
    kp = core[:, hw3 + Q_RANK + KV_RANK:] * kcs_ref[...]
    kp = kp + pltpu.roll(kp, HEAD_PAD - QK_ROPE, axis=1)
    kt = _dot_nt(wkt_ref[...], ckvn) + _dot_nt(pkt_ref[...], kp.astype(BF16))
    for h in range(MLA_HEADS):
        kt_out[h] = kt[h * HEAD_PAD:(h + 1) * HEAD_PAD, :].astype(BF16)
    vv = _dot(ckvn, wv_ref[...]) + vone_ref[...]
    for h in range(MLA_HEADS):
        vv_out[h] = vv[:, h * HEAD_PAD:(h + 1) * HEAD_PAD].astype(BF16)


def _inproj(x, lw, tabs, nb, L):
    M, D = x.shape
    tm = min(512, L)
    tps = L // tm
    nt = M // tm
    hb = tm // HALO
    nhb = M // HALO
    H = MLA_HEADS
    wcols = lw["win"].shape[1]
    const = lambda *shape: pl.BlockSpec(shape, lambda i: (0,) * len(shape))
    seq_tab = pl.BlockSpec((tm, HEAD_PAD), lambda i: (i % tps, 0))
    in_specs = [
        pl.BlockSpec((HALO, D), lambda i: (jnp.maximum(i * hb - 1, 0), 0)),
        pl.BlockSpec((tm, D), lambda i: (i, 0)),
        pl.BlockSpec((HALO, D), lambda i: (jnp.minimum((i + 1) * hb, nhb - 1), 0)),
        const(1, D), const(D, wcols), const(3, 3 * HY_WIDTH), const(1, 3 * HY_WIDTH),
        const(1, Q_RANK), const(Q_RANK, H * HEAD_PAD), const(Q_RANK, H * HEAD_PAD),
        const(1, KV_RANK), const(H * HEAD_PAD, KV_RANK), const(H * HEAD_PAD, HEAD_PAD),
        const(KV_RANK, H * HEAD_PAD), const(1, H * HEAD_PAD),
        seq_tab, seq_tab, seq_tab,
    ]
    hy_spec = pl.BlockSpec((tm, HY_WIDTH), lambda i: (i, 0))
    out_specs = [
        hy_spec, hy_spec, hy_spec,
        pl.BlockSpec((None, H, tm, HEAD_PAD), lambda i: (i // tps, 0, i % tps, 0)),
        pl.BlockSpec((None, H, None, HEAD_PAD, tm), lambda i: (i // tps, 0, i % tps, 0, 0)),
        pl.BlockSpec((None, H, tm, HEAD_PAD), lambda i: (i // tps, 0, i % tps, 0)),
    ]
    out_shape = [
        jax.ShapeDtypeStruct((M, HY_WIDTH), F32),
        jax.ShapeDtypeStruct((M, HY_WIDTH), F32),
        jax.ShapeDtypeStruct((M, HY_WIDTH), F32),
        jax.ShapeDtypeStruct((nb, H, L, HEAD_PAD), BF16),
        jax.ShapeDtypeStruct((nb, H, tps, HEAD_PAD, tm), BF16),
        jax.ShapeDtypeStruct((nb, H, L, HEAD_PAD), BF16),
    ]
    scale = float((QK_NOPE + QK_ROPE) ** -0.5)
    return pl.pallas_call(
        functools.partial(_inproj_kernel, tm=tm, tiles_per_seq=tps, scale=scale),
        grid=(nt,), in_specs=in_specs, out_specs=out_specs, out_shape=out_shape,
        scratch_shapes=[pltpu.VMEM((tm + 2 * HALO, D), BF16),
                        pltpu.VMEM((tm + 2 * HALO, 3 * HY_WIDTH), F32)],
        compiler_params=_cparams(("parallel",)), name="inproj",
    )(x, x, x, lw["g_pre"], lw["win"], lw["conv_w"], lw["conv_b"], lw["q_g"], lw["wqa"], lw["wqb"],
      lw["kv_g"], lw["wkt"], tabs["pkt"], lw["wv"], tabs["vone"], tabs["qcos"], tabs["qsin"], tabs["kcs"])


def _filter_kernel(z_ref, w1_ref, b1_ref, sf_ref, w2_ref, b2_ref, w3_ref, dec_ref, out_ref, *, rb, L):
    hi = lax.Precision.HIGHEST
    r = pl.program_id(1)
    z = z_ref[...]
    h = jnp.sin(sf_ref[0:1, :] * (jnp.dot(z, w1_ref[...], precision=hi, preferred_element_type=F32) + b1_ref[...]))
    h = jnp.sin(sf_ref[1:2, :] * (jnp.dot(h, w2_ref[...], precision=hi, preferred_element_type=F32) + b2_ref[...]))
    k = jnp.dot(h, w3_ref[...], precision=hi, preferred_element_type=F32)
    k = k * jnp.exp(-z[:, 0:1] * jnp.abs(dec_ref[...]))
    row = r * rb + lax.broadcasted_iota(jnp.int32, (rb, 1), 0)
    out_ref[...] = jnp.where(row == L, 0.0, k)


def _filters(ztab, fw, L):
    depth = fw["w1"].shape[0]
    rows = 2 * L
    rb = min(512, L)
    nblk = rows // rb
    half = nblk // 2
    oc = HY_ORDER * HY_WIDTH
    lay = lambda *shape: pl.BlockSpec((None,) + shape, lambda l, r: (l,) + (0,) * len(shape))
    in_specs = [
        pl.BlockSpec((rb, HY_EMB_PAD), lambda l, r: (r, 0)),
        lay(HY_EMB_PAD, HY_FFN), lay(1, HY_FFN), lay(2, HY_FFN), lay(HY_FFN, HY_FFN), lay(1, HY_FFN),
        pl.BlockSpec((None, None, HY_FFN, oc), lambda l, r: (l, r // half, 0, 0)),
        pl.BlockSpec((None, None, 1, oc), lambda l, r: (l, r // half, 0, 0)),
    ]
    return pl.pallas_call(
        functools.partial(_filter_kernel, rb=rb, L=L),
        grid=(depth, nblk), in_specs=in_specs,
        out_specs=pl.BlockSpec((None, rb, oc), lambda l, r: (l, r, 0)),
        out_shape=jax.ShapeDtypeStruct((depth, rows, oc), F32),
        compiler_params=_cparams(("parallel", "parallel")), name="hyena_filter",
    )(ztab, fw["w1"], fw["b1"], fw["sf"], fw["w2"], fw["b2"], fw["w3"], fw["dec"])


def _dft1_kernel(f_ref, x_ref, out_ref):
    out_ref[...] = _dot(f_ref[...], x_ref[...].astype(BF16))


def _dft1(x, f1, tn):
    nb, R, NC = x.shape
    rows = f1.shape[0]
    return pl.pallas_call(
        _dft1_kernel, grid=(nb, NC // tn),
        in_specs=[pl.BlockSpec((rows, R), lambda b, j: (0, 0)),
                  pl.BlockSpec((None, R, tn), lambda b, j: (b, 0, j))],
        out_specs=pl.BlockSpec((None, rows, tn), lambda b, j: (b, 0, j)),
        out_shape=jax.ShapeDtypeStruct((nb, rows, NC), F32),
        compiler_params=_cparams(("parallel", "parallel")), name="dft_outer",
    )(f1, x)


def _spec_kernel(m1_ref, a_ref, out_ref):
    n2, c = a_ref.shape[1], a_ref.shape[2]
    a = a_ref[...].reshape(2 * n2, c).astype(BF16)
    out_ref[...] = _dot(m1_ref[...], a).reshape(2, n2, c)


def _filter_spectrum(a, m1):
    depth, _, kp, n2, oc = a.shape
    c = HY_WIDTH
    blk = pl.BlockSpec((None, 2, None, n2, c), lambda l, k, o: (l, 0, k, 0, o))
    return pl.pallas_call(
        _spec_kernel, grid=(depth, kp, oc // c),
        in_specs=[pl.BlockSpec((None, 2 * n2, 2 * n2), lambda l, k, o: (k, 0, 0)), blk],
        out_specs=blk, out_shape=jax.ShapeDtypeStruct(a.shape, F32),
        compiler_params=_cparams(("parallel", "parallel", "parallel")), name="filter_spectrum",
    )(m1, a)


def _mid_kernel(m1_ref, m2_ref, kf_ref, a_ref, out_ref):
    n2, c = a_ref.shape[1], a_ref.shape[2]
    a = a_ref[...].reshape(2 * n2, c).astype(BF16)
    x = _dot(m1_ref[...], a)
    xr, xi = x[:n2], x[n2:]
    kr, ki = kf_ref[0], kf_ref[1]
    y = jnp.concatenate([xr * kr - xi * ki, xr * ki + xi * kr], axis=0).astype(BF16)
    out_ref[...] = _dot(m2_ref[...], y).reshape(2, n2, c)


def _conv_mid(a, kf, m1, m2, layer, order):
    nb, _, kp, n2, c = a.shape
    mspec = pl.BlockSpec((None, 2 * n2, 2 * n2), lambda k, b: (k, 0, 0))
    blk = pl.BlockSpec((None, 2, None, n2, c), lambda k, b: (b, 0, k, 0, 0))
    return pl.pallas_call(
        _mid_kernel, grid=(kp, nb),
        in_specs=[mspec, mspec,
                  pl.BlockSpec((None, 2, None, n2, c), lambda k, b: (layer, 0, k, 0, order)), blk],
        out_specs=blk, out_shape=jax.ShapeDtypeStruct(a.shape, F32),
        compiler_params=_cparams(("parallel", "arbitrary")), name="conv_mid",
    )(m1, m2, kf, a)


def _gate_kernel(g_ref, b_ref, z_ref, gate_ref, bias_ref, out_ref):
    y = _dot(g_ref[...], b_ref[...].astype(BF16))
    out_ref[...] = gate_ref[...] * (y + z_ref[...] * bias_ref[...])


def _conv_out(bsp, ginv, z, gate, bias_t, tn):
    nb, rows, NC = bsp.shape
    R = ginv.shape[0]
    tile = pl.BlockSpec((None, R, tn), lambda b, j: (b, 0, j))
    return pl.pallas_call(
        _gate_kernel, grid=(nb, NC // tn),
        in_specs=[pl.BlockSpec((R, rows), lambda b, j: (0, 0)),
                  pl.BlockSpec((None, rows, tn), lambda b, j: (b, 0, j)),
                  tile, tile, pl.BlockSpec((1, tn), lambda b, j: (0, 0))],
        out_specs=tile, out_shape=jax.ShapeDtypeStruct((nb, R, NC), F32),
        compiler_params=_cparams(("parallel", "parallel")), name="conv_gate",
    )(ginv, bsp, z, gate, bias_t)


def _long_conv_gate(z, gate, kf, bias_t, tabs, layer, order, nb, L):
    c = HY_WIDTH
    n1h = L // DFT_N2
    nc = DFT_N2 * c
    tn = 8 * c
    kp = tabs["m1"].shape[0]
    zv = z.reshape(nb, n1h, nc)
    a = _dft1(zv, tabs["f1d"], tn)
    bsp = _conv_mid(a.reshape(nb, 2, kp, DFT_N2, c), kf, tabs["m1"], tabs["m2"], layer, order)
    out = _conv_out(bsp.reshape(nb, 2 * kp, nc), tabs["ginv"], zv, gate.reshape(nb, n1h, nc), bias_t, tn)
    return out.reshape(nb * L, c)


def _attn_kernel(q_ref, kt_ref, v_ref, g_ref, o_ref, m_scr, acc_scr, *, nk, tk):
    outs = []
    for hh in range(2):
        q = q_ref[hh]
        m_scr[...] = jnp.full(m_scr.shape, -jnp.inf, F32)
        acc_scr[...] = jnp.zeros(acc_scr.shape, F32)

        def body(j, carry, hh=hh, q=q):
            s = _dot(q, kt_ref[hh, j])
            m_prev = m_scr[...]
            m_new = jnp.maximum(m_prev, jnp.max(s, axis=1, keepdims=True))
            alpha = jnp.exp(m_prev - m_new)
            p = jnp.exp(s - m_new).astype(BF16)
            v = v_ref[hh, pl.ds(pl.multiple_of(j * tk, tk), tk), :]
            acc_scr[...] = alpha * acc_scr[...] + _dot(p, v)
            m_scr[...] = m_new
            return carry

        lax.fori_loop(0, nk, body, 0)
        acc = acc_scr[...]
        o = acc[:, :V_HEAD] / acc[:, V_HEAD:V_HEAD + 1]
        ms = jnp.mean(o * o, axis=-1, keepdims=True)
        outs.append(o * lax.rsqrt(ms + EPS))
    o_ref[...] = (jnp.concatenate(outs, axis=1) * g_ref[...]).astype(o_ref.dtype)


def _attention(q, kt, v, g_attn, nb, L):
    H = MLA_HEADS
    tq = min(512, L)
    nk, tk = kt.shape[2], kt.shape[4]
    return pl.pallas_call(
        functools.partial(_attn_kernel, nk=nk, tk=tk),
        grid=(nb, H // 2, L // tq),
        in_specs=[pl.BlockSpec((None, 2, tq, HEAD_PAD), lambda b, h, i: (b, h, i, 0)),
                  pl.BlockSpec((None, 2, nk, HEAD_PAD, tk), lambda b, h, i: (b, h, 0, 0, 0)),
                  pl.BlockSpec((None, 2, L, HEAD_PAD), lambda b, h, i: (b, h, 0, 0)),
                  pl.BlockSpec((1, 2 * V_HEAD), lambda b, h, i: (0, h))],
        out_specs=pl.BlockSpec((None, tq, 2 * V_HEAD), lambda b, h, i: (b, i, h)),
        out_shape=jax.ShapeDtypeStruct((nb, L, ATTN_WIDTH), BF16),
        scratch_shapes=[pltpu.VMEM((tq, 1), F32), pltpu.VMEM((tq, HEAD_PAD), F32)],
        compiler_params=_cparams(("parallel", "parallel", "arbitrary")), name="attention",
    )(q, kt, v, g_attn)


def _mix_mlp_kernel(x_ref, zh_ref, an_ref, ghy_ref, gsum_ref, wo_ref, gpost_ref, gmpre_ref, wup_ref,
                    wdn_ref, gmpost_ref, out_ref):
    zh = zh_ref[...]
    ms = _dot((zh * zh).astype(BF16), gsum_ref[...]) * (HY_GROUPS / HY_WIDTH)
    hn = (zh * lax.rsqrt(ms + EPS) * ghy_ref[...]).astype(BF16)
    mix = _dot(hn, wo_ref[0:HY_WIDTH, :]) + _dot(an_ref[...], wo_ref[HY_WIDTH:, :])
    x = x_ref[...] + _rms(mix, gpost_ref[...])
    h = _rms(x, gmpre_ref[...]).astype(BF16)
    up = jnp.maximum(_dot(h, wup_ref[...]), 0.0)
    m = _dot((up * up).astype(BF16), wdn_ref[...])
    out_ref[...] = x + _rms(m, gmpost_ref[...])


def _mix_mlp(x, zh, an, lw, tabs):
    M, D = x.shape
    tm = min(256, M)
    dff = lw["wup"].shape[1]
    mw = HY_WIDTH + ATTN_WIDTH
    const = lambda *shape: pl.BlockSpec(shape, lambda i: (0,) * len(shape), pipeline_mode=pl.Buffered(1))
    rows = lambda w: pl.BlockSpec((tm, w), lambda i: (i, 0))
    return pl.pallas_call(
        _mix_mlp_kernel, grid=(M // tm,),
        in_specs=[rows(D), rows(HY_WIDTH), rows(ATTN_WIDTH), const(1, HY_WIDTH), const(HY_WIDTH, HY_WIDTH),
                  const(mw, D), const(1, D), const(1, D), const(D, dff), const(dff, D), const(1, D)],
        out_specs=rows(D), out_shape=jax.ShapeDtypeStruct((M, D), F32),
        compiler_params=_cparams(("parallel",)), name="mix_mlp",
    )(x, zh, an, lw["g_hy"], tabs["gsum"], lw["wo"], lw["g_post"], lw["g_mpre"], lw["wup"], lw["wdn"],
      lw["g_mpost"])


def _tables(L):
    n = 2 * L
    n2 = DFT_N2
    n1 = n // n2
    nh = n1 // 2
    kp = -(-(nh + 1) // 8) * 8
    two_pi = 2.0 * math.pi

    k1 = jnp.arange(kp, dtype=jnp.int32)
    valid = (k1 <= nh)
    def outer(ncols):
        nn = jnp.arange(ncols, dtype=jnp.int32)
        ang = ((k1[:, None] * nn[None, :]) % n1).astype(F32) * (two_pi / n1)
        c = jnp.where(valid[:, None], jnp.cos(ang), 0.0)
        s = jnp.where(valid[:, None], -jnp.sin(ang), 0.0)
        return jnp.concatenate([c, s], axis=0).astype(BF16)
    f1d = outer(nh)
    f1f = outer(n1)
    nn = jnp.arange(nh, dtype=jnp.int32)
    ang = ((nn[:, None] * k1[None, :]) % n1).astype(F32) * (two_pi / n1)
    wgt = jnp.where(valid, jnp.where((k1 == 0) | (k1 == nh), 1.0, 2.0), 0.0) / n
    ginv = jnp.concatenate([jnp.cos(ang) * wgt[None, :], -jnp.sin(ang) * wgt[None, :]], axis=1).astype(BF16)
    a2 = jnp.arange(n2, dtype=jnp.int32)
    idx = (a2[None, :, None] * a2[None, None, :] * n1 + a2[None, None, :] * k1[:, None, None]) % n
    ph = idx.astype(F32) * (two_pi / n)
    gr, gi = jnp.cos(ph), -jnp.sin(ph)
    m1 = jnp.concatenate([jnp.concatenate([gr, -gi], axis=2), jnp.concatenate([gi, gr], axis=2)], axis=1)
    m2 = jnp.swapaxes(m1, 1, 2)
    inv = 1.0 / (ROPE_BASE ** (jnp.arange(0, QK_ROPE, 2, dtype=F32) / QK_ROPE))
    ang = jnp.arange(L, dtype=F32)[:, None] * inv[None, :]
    cos, sin = jnp.cos(ang), jnp.sin(ang)
    pad = HEAD_PAD - QK_NOPE - QK_ROPE
    qcos = jnp.concatenate([jnp.ones((L, QK_NOPE), F32), cos, cos, jnp.ones((L, pad), F32)], axis=1)
    qsin = jnp.concatenate([jnp.zeros((L, QK_NOPE), F32), sin, sin, jnp.zeros((L, pad), F32)], axis=1)
    kcs = jnp.concatenate([cos, cos, sin, sin, jnp.zeros((L, HEAD_PAD - 2 * QK_ROPE), F32)], axis=1)
    rr = jnp.arange(MLA_HEADS * HEAD_PAD)
    pkt = ((rr[:, None] % HEAD_PAD) - QK_NOPE == jnp.arange(HEAD_PAD)[None, :]) & (jnp.arange(HEAD_PAD)[None, :] < QK_ROPE)
    vone = ((rr % HEAD_PAD) == V_HEAD).astype(F32)[None, :]
    cc = jnp.arange(HY_WIDTH) // (HY_WIDTH // HY_GROUPS)
    gsum = (cc[:, None] == cc[None, :]).astype(BF16)
    t = jnp.linspace(0.0, 1.0, L, dtype=F32)[:, None]
    omega = (two_pi / L) * jnp.arange(L, dtype=F32)
    bands = jnp.linspace(1e-4, HY_BANDS - 1, HY_BANDS, dtype=F32)
    phase = omega[:, None] * bands[None, :]
    z = jnp.concatenate([t, jnp.cos(phase), -jnp.sin(phase), jnp.zeros((L, HY_EMB_PAD - HY_EMB), F32)], axis=-1)
    ztab = jnp.concatenate([z, z[:1], z[:0:-1]], axis=0)
    return dict(f1d=f1d, f1f=f1f, ginv=ginv, m1=m1.astype(BF16), m2=m2.astype(BF16), qcos=qcos, qsin=qsin,
                kcs=kcs, pkt=pkt.astype(BF16), vone=vone, gsum=gsum, ztab=ztab, kp=kp)


def _rot_half_cols(w):
    half = QK_ROPE // 2
    return jnp.concatenate([-w[..., half:], w[..., :half]], axis=-1)


def _layer_weights(i, p):
    D = p["w_in"].shape[1]
    H = MLA_HEADS
    hw3 = 3 * HY_WIDTH
    w_in = p["w_in"][i]
    kpe = w_in[:, hw3 + Q_RANK + KV_RANK:]
    win = jnp.concatenate([w_in, _rot_half_cols(kpe),
                           jnp.zeros((D, HEAD_PAD - 2 * QK_ROPE), F32)], axis=1).astype(BF16)
    dq = QK_NOPE + QK_ROPE
    wq = p["mla_w_uq"][i].reshape(Q_RANK, H, dq)
    zq = jnp.zeros((Q_RANK, H, HEAD_PAD - dq), F32)
    wqa = jnp.concatenate([wq, zq], axis=2).reshape(Q_RANK, H * HEAD_PAD).astype(BF16)
    wqb = jnp.concatenate([jnp.zeros((Q_RANK, H, QK_NOPE), F32), _rot_half_cols(wq[..., QK_NOPE:]), zq],
                          axis=2).reshape(Q_RANK, H * HEAD_PAD).astype(BF16)
    wkv = p["mla_w_ukv"][i].reshape(KV_RANK, H, QK_NOPE + V_HEAD)
    wk = jnp.concatenate([wkv[..., :QK_NOPE], jnp.zeros((KV_RANK, H, HEAD_PAD - QK_NOPE), F32)], axis=2)
    wkt = wk.reshape(KV_RANK, H * HEAD_PAD).T.astype(BF16)
    wv = jnp.concatenate([wkv[..., QK_NOPE:], jnp.zeros((KV_RANK, H, HEAD_PAD - V_HEAD), F32)],
                         axis=2).reshape(KV_RANK, H * HEAD_PAD).astype(BF16)
    row = lambda a: a.reshape(1, -1)
    return dict(
        win=win, g_pre=row(p["norm_mix_pre"][i]), conv_w=p["hy_conv_w"][i], conv_b=row(p["hy_conv_b"][i]),
        q_g=row(p["mla_q_norm"][i]), wqa=wqa, wqb=wqb, kv_g=row(p["mla_kv_norm"][i]), wkt=wkt, wv=wv,
        g_hy=row(p["grp_norm_hy"][i]), g_attn=row(p["grp_norm_attn"][i]), wo=p["w_out"][i].astype(BF16),
        g_post=row(p["norm_mix_post"][i]), g_mpre=row(p["norm_mlp_pre"][i]), g_mpost=row(p["norm_mlp_post"][i]),
        wup=p["w_mlp_up"][i].astype(BF16), wdn=p["w_mlp_down"][i].astype(BF16),
    )


def _filter_weights(p):
    depth = p["hy_ffn_w1"].shape[0]
    oc = HY_ORDER * HY_WIDTH
    w1 = jnp.pad(p["hy_ffn_w1"], ((0, 0), (0, HY_EMB_PAD - HY_EMB), (0, 0)))
    w3 = p["hy_ffn_w3"].reshape(depth, HY_FFN, HY_ORDER, 2, HY_WIDTH).transpose(0, 3, 1, 2, 4)
    dec = p["hy_decay"].transpose(0, 2, 1, 3).reshape(depth, 2, 1, oc)
    return dict(w1=w1, b1=p["hy_ffn_b1"][:, None, :], sf=p["hy_sin_freq"], w2=p["hy_ffn_w2"],
                b2=p["hy_ffn_b2"][:, None, :], w3=w3.reshape(depth, 2, HY_FFN, oc), dec=dec)


def kernel(x_prompt, x_sample, w_in, hy_conv_w, hy_conv_b, hy_ffn_w1, hy_ffn_b1, hy_ffn_w2, hy_ffn_b2,
           hy_ffn_w3, hy_sin_freq, hy_decay, hy_bias, mla_q_norm, mla_w_uq, mla_kv_norm, mla_w_ukv,
           grp_norm_hy, grp_norm_attn, w_out, norm_mix_pre, norm_mix_post, norm_mlp_pre, norm_mlp_post,
           w_mlp_up, w_mlp_down):
    p = dict(w_in=w_in, hy_conv_w=hy_conv_w, hy_conv_b=hy_conv_b, hy_ffn_w1=hy_ffn_w1, hy_ffn_b1=hy_ffn_b1,
             hy_ffn_w2=hy_ffn_w2, hy_ffn_b2=hy_ffn_b2, hy_ffn_w3=hy_ffn_w3, hy_sin_freq=hy_sin_freq,
             hy_decay=hy_decay, hy_bias=hy_bias, mla_q_norm=mla_q_norm, mla_w_uq=mla_w_uq,
             mla_kv_norm=mla_kv_norm, mla_w_ukv=mla_w_ukv, grp_norm_hy=grp_norm_hy,
             grp_norm_attn=grp_norm_attn, w_out=w_out, norm_mix_pre=norm_mix_pre, norm_mix_post=norm_mix_post,
             norm_mlp_pre=norm_mlp_pre, norm_mlp_post=norm_mlp_post, w_mlp_up=w_mlp_up, w_mlp_down=w_mlp_down)
    bp, L, D = x_prompt.shape
    bs, Ls, _ = x_sample.shape
    assert L == Ls and L % DFT_N2 == 0
    nb = bp + bs
    depth = w_in.shape[0]
    c = HY_WIDTH

    tabs = _tables(L)
    kp = tabs["kp"]
    n1 = 2 * L // DFT_N2
    oc = HY_ORDER * c

    kc = _filters(tabs["ztab"], _filter_weights(p), L)
    ka = _dft1(kc.reshape(depth, n1, DFT_N2 * oc), tabs["f1f"], 8 * oc)
    kf = _filter_spectrum(ka.reshape(depth, 2, kp, DFT_N2, oc), tabs["m1"])

    x = jnp.concatenate([x_prompt.reshape(bp * L, D), x_sample.reshape(bs * L, D)], axis=0)
    for i in range(depth):
        lw = _layer_weights(i, p)
        v, x1, x2, q, kt, vv = _inproj(x, lw, tabs, nb, L)
        bias = hy_bias[i]
        z = _long_conv_gate(v, x1, kf, jnp.tile(bias[0], 8)[None, :], tabs, i, 0, nb, L)
        z = _long_conv_gate(z, x2, kf, jnp.tile(bias[1], 8)[None, :], tabs, i, 1, nb, L)
        an = _attention(q, kt, vv, lw["g_attn"], nb, L).reshape(nb * L, ATTN_WIDTH)
        x = _mix_mlp(x, z, an, lw, tabs)
    return (x[:bp * L].reshape(bp, L, D), x[bp * L:].reshape(bs, L, D))
```

```python
import functools
import math

import jax
import jax.numpy as jnp
from jax import lax
from jax.experimental import pallas as pl
from jax.experimental.pallas import tpu as pltpu

F32 = jnp.float32
BF16 = jnp.bfloat16

EPS = 1e-6
HY_WIDTH = 512
HY_GROUPS = 8
HY_ORDER = 2
HY_BANDS = 16
HY_EMB = 2 * HY_BANDS + 1
HY_EMB_PAD = 40
HY_FFN = 64
MLA_HEADS = 8
QK_NOPE = 64
QK_ROPE = 32
V_HEAD = 64
Q_RANK = 256
KV_RANK = 128
ROPE_BASE = 10000.0
HEAD_PAD = 128
ATTN_WIDTH = MLA_HEADS * V_HEAD

LANES = 128
SUBLANES = 8
HY_CB = HY_WIDTH // LANES
DFT_N2 = 128
DFT_G = DFT_N2 // SUBLANES
HALO = 16
VMEM_LIMIT = 56 * 1024 * 1024
ATTN_UNROLL = 4
STAT_ROWS = 24

FAST_S_MAX = 64.0
FAST_V_MAX = 2.0 ** 30


def _cparams(sem):
    return pltpu.CompilerParams(dimension_semantics=sem, vmem_limit_bytes=VMEM_LIMIT)


def _dot(a, b):
    return jnp.dot(a, b, preferred_element_type=F32)


def _dot_nt(a, b):
    return lax.dot_general(a, b, (((1,), (1,)), ((), ())), preferred_element_type=F32)


def _rms(x, g):
    return x * lax.rsqrt(jnp.mean(x * x, axis=-1, keepdims=True) + EPS) * g


def _fold_lanes(row):
    parts = [row[:, j * LANES:(j + 1) * LANES] for j in range(row.shape[1] // LANES)]
    return functools.reduce(jnp.maximum, parts)


def _inproj_kernel(xp_ref, x_ref, xn_ref, gpre_ref, win_ref, cw_ref, cb_ref, qg_ref, wqat_ref,
                   wqbt_ref, kvg_ref, wk_ref, pk_ref, wvt_ref, qcos_ref, qsin_ref, kcs_ref,
                   v_out, x1_out, x2_out, qt_out, k_out, vt_out, st_out, h_scr, pe_scr,
                   *, tm, tiles_per_seq, scale):
    i = pl.program_id(0)
    t_idx = i % tiles_per_seq
    g = gpre_ref[...]
    h_scr[0:HALO, :] = _rms(xp_ref[...], g).astype(BF16)
    h_scr[HALO:HALO + tm, :] = _rms(x_ref[...], g).astype(BF16)
    h_scr[HALO + tm:2 * HALO + tm, :] = _rms(xn_ref[...], g).astype(BF16)
    proj = _dot(h_scr[...], win_ref[...])

    hw3 = 3 * HY_WIDTH
    pe_scr[...] = proj[:, :hw3]
    row = lax.broadcasted_iota(jnp.int32, (tm, 1), 0)
    prev = pe_scr[HALO - 1:HALO - 1 + tm, :]
    cur = pe_scr[HALO:HALO + tm, :]
    nxt = pe_scr[HALO + 1:HALO + 1 + tm, :]
    prev = jnp.where(jnp.logical_and(row == 0, t_idx == 0), 0.0, prev)
    nxt = jnp.where(jnp.logical_and(row == tm - 1, t_idx == tiles_per_seq - 1), 0.0, nxt)
    u = prev * cw_ref[0:1, :] + cur * cw_ref[1:2, :] + nxt * cw_ref[2:3, :] + cb_ref[...]
    for j in range(HY_CB):
        v_out[j] = u[:, j * LANES:(j + 1) * LANES]
        x1_out[j] = u[:, HY_WIDTH + j * LANES:HY_WIDTH + (j + 1) * LANES]
        x2_out[j] = u[:, 2 * HY_WIDTH + j * LANES:2 * HY_WIDTH + (j + 1) * LANES]

    core = proj[HALO:HALO + tm, :]
    cq = core[:, hw3:hw3 + Q_RANK]
    cqn = _rms(cq, qg_ref[...]).astype(BF16)
    qat = _dot_nt(wqat_ref[...], cqn)
    qbt = _dot_nt(wqbt_ref[...], cqn)
    qcos = qcos_ref[...]
    qsin = qsin_ref[...]
    for h in range(MLA_HEADS):
        sl = slice(h * HEAD_PAD, (h + 1) * HEAD_PAD)
        qh = ((qat[sl, :] * qcos + qbt[sl, :] * qsin) * scale).astype(BF16)
        qt_out[h] = qh
        qf = qh.astype(F32)
        st_out[h:h + 1, :] = _fold_lanes(jnp.sum(qf * qf, axis=0, keepdims=True))

    ckv = core[:, hw3 + Q_RANK:hw3 + Q_RANK + KV_RANK]
    ckvn = _rms(ckv, kvg_ref[...]).astype(BF16)
    kp = core[:, hw3 + Q_RANK + KV_RANK:] * kcs_ref[...]
    kp = kp + pltpu.roll(kp, HEAD_PAD - QK_ROPE, axis=1)
    kk = _dot(ckvn, wk_ref[...]) + _dot(kp.astype(BF16), pk_ref[...])
    for h in range(MLA_HEADS):
        kh = kk[:, h * HEAD_PAD:(h + 1) * HEAD_PAD].astype(BF16)
        k_out[h] = kh
        kf = kh.astype(F32)
        kn = jnp.max(jnp.sum(kf * kf, axis=1, keepdims=True), axis=0, keepdims=True)
        st_out[MLA_HEADS + h:MLA_HEADS + h + 1, :] = jnp.broadcast_to(kn, (1, LANES))
    vt = _dot_nt(wvt_ref[...], ckvn)
    st_out[2 * MLA_HEADS:2 * MLA_HEADS + 1, :] = _fold_lanes(jnp.max(jnp.abs(vt), axis=0, keepdims=True))
    st_out[2 * MLA_HEADS + 1:, :] = jnp.zeros((STAT_ROWS - 2 * MLA_HEADS - 1, LANES), F32)
    frow = lax.broadcasted_iota(jnp.int32, (MLA_HEADS * HEAD_PAD, 1), 0)
    vt = vt + jnp.where(frow % HEAD_PAD == V_HEAD, 1.0, 0.0)
    for h in range(MLA_HEADS):
        vt_out[h] = vt[h * HEAD_PAD:(h + 1) * HEAD_PAD, :].astype(BF16)


def _inproj(x, lw, tabs, nb, L):
    M, D = x.shape
    tm = min(512, L)
    tps = L // tm
    nt = M // tm
    hb = tm // HALO
    nhb = M // HALO
    H = MLA_HEADS
    wcols = lw["win"].shape[1]
    const = lambda *shape: pl.BlockSpec(shape, lambda i: (0,) * len(shape))
    tab_t = pl.BlockSpec((HEAD_PAD, tm), lambda i: (0, i % tps))
    in_specs = [
        pl.BlockSpec((HALO, D), lambda i: (jnp.maximum(i * hb - 1, 0), 0)),
        pl.BlockSpec((tm, D), lambda i: (i, 0)),
        pl.BlockSpec((HALO, D), lambda i: (jnp.minimum((i + 1) * hb, nhb - 1), 0)),
        const(1, D), const(D, wcols), const(3, 3 * HY_WIDTH), const(1, 3 * HY_WIDTH),
        const(1, Q_RANK), const(H * HEAD_PAD, Q_RANK), const(H * HEAD_PAD, Q_RANK),
        const(1, KV_RANK), const(KV_RANK, H * HEAD_PAD), const(HEAD_PAD, H * HEAD_PAD),
        const(H * HEAD_PAD, KV_RANK),
        tab_t, tab_t, pl.BlockSpec((tm, HEAD_PAD), lambda i: (i % tps, 0)),
    ]
    hy_spec = pl.BlockSpec((None, HY_CB, tm, LANES), lambda i: (i // tps, 0, i % tps, 0))
    out_specs = [
        hy_spec, hy_spec, hy_spec,
        pl.BlockSpec((None, H, HEAD_PAD, tm), lambda i: (i // tps, 0, 0, i % tps)),
        pl.BlockSpec((None, H, tm, HEAD_PAD), lambda i: (i // tps, 0, i % tps, 0)),
        pl.BlockSpec((None, H, None, HEAD_PAD, tm), lambda i: (i // tps, 0, i % tps, 0, 0)),
        pl.BlockSpec((None, None, STAT_ROWS, LANES), lambda i: (i // tps, i % tps, 0, 0)),
    ]
    hy_shape = jax.ShapeDtypeStruct((nb, HY_CB, L, LANES), F32)
    out_shape = [
        hy_shape, hy_shape, hy_shape,
        jax.ShapeDtypeStruct((nb, H, HEAD_PAD, L), BF16),
        jax.ShapeDtypeStruct((nb, H, L, HEAD_PAD), BF16),
        jax.ShapeDtypeStruct((nb, H, tps, HEAD_PAD, tm), BF16),
        jax.ShapeDtypeStruct((nb, tps, STAT_ROWS, LANES), F32),
    ]
    scale = float((QK_NOPE + QK_ROPE) ** -0.5 * math.log2(math.e))
    return pl.pallas_call(
        functools.partial(_inproj_kernel, tm=tm, tiles_per_seq=tps, scale=scale),
        grid=(nt,), in_specs=in_specs, out_specs=out_specs, out_shape=out_shape,
        scratch_shapes=[pltpu.VMEM((tm + 2 * HALO, D), BF16),
                        pltpu.VMEM((tm + 2 * HALO, 3 * HY_WIDTH), F32)],
        compiler_params=_cparams(("parallel",)), name="inproj",
    )(x, x, x, lw["g_pre"], lw["win"], lw["conv_w"], lw["conv_b"], lw["q_g"], lw["wqat"], lw["wqbt"],
      lw["kv_g"], lw["wk"], tabs["pk"], lw["wvt"], tabs["qcos_t"], tabs["qsin_t"], tabs["kcs"])


def _filter_kernel(z_ref, w1_ref, b1_ref, sf_ref, w2_ref, b2_ref, w3_ref, dec_ref, out_ref, *, rb, L):
    hi = lax.Precision.HIGHEST
    r = pl.program_id(1)
    z = z_ref[...]
    h = jnp.sin(sf_ref[0:1, :] * (jnp.dot(z, w1_ref[...], precision=hi, preferred_element_type=F32) + b1_ref[...]))
    h = jnp.sin(sf_ref[1:2, :] * (jnp.dot(h, w2_ref[...], precision=hi, preferred_element_type=F32) + b2_ref[...]))
    k = jnp.dot(h, w3_ref[...], precision=hi, preferred_element_type=F32)
    k = k * jnp.exp(-z[:, 0:1] * jnp.abs(dec_ref[...]))
    row = r * rb + lax.broadcasted_iota(jnp.int32, (rb, 1), 0)
    k = jnp.where(row == L, 0.0, k)
    for j in range(out_ref.shape[0]):
        out_ref[j] = k[:, j * LANES:(j + 1) * LANES]


def _filters(ztab, fw, L):
    depth = fw["w1"].shape[0]
    rows = 2 * L
    rb = min(512, L)
    nblk = rows // rb
    half = nblk // 2
    oc = HY_ORDER * HY_WIDTH
    lay = lambda *shape: pl.BlockSpec((None,) + shape, lambda l, r: (l,) + (0,) * len(shape))
    in_specs = [
        pl.BlockSpec((rb, HY_EMB_PAD), lambda l, r: (r, 0)),
        lay(HY_EMB_PAD, HY_FFN), lay(1, HY_FFN), lay(2, HY_FFN), lay(HY_FFN, HY_FFN), lay(1, HY_FFN),
        pl.BlockSpec((None, None, HY_FFN, oc), lambda l, r: (l, r // half, 0, 0)),
        pl.BlockSpec((None, None, 1, oc), lambda l, r: (l, r // half, 0, 0)),
    ]
    return pl.pallas_call(
        functools.partial(_filter_kernel, rb=rb, L=L),
        grid=(depth, nblk), in_specs=in_specs,
        out_specs=pl.BlockSpec((None, oc // LANES, rb, LANES), lambda l, r: (l, 0, r, 0)),
        out_shape=jax.ShapeDtypeStruct((depth, oc // LANES, rows, LANES), F32),
        compiler_params=_cparams(("parallel", "parallel")), name="hyena_filter",
    )(ztab, fw["w1"], fw["b1"], fw["sf"], fw["w2"], fw["b2"], fw["w3"], fw["dec"])


def _dft1_kernel(f_ref, x_ref, out_ref):
    xs = [x_ref[:, p, :].astype(BF16) for p in range(SUBLANES)]
    res = _dot(f_ref[...], jnp.concatenate(xs, axis=1))
    for p in range(SUBLANES):
        out_ref[p] = res[:, p * LANES:(p + 1) * LANES]


def _dft1(x, f1):
    nb, ncb, rows, _ = x.shape
    R = rows // DFT_N2
    kp2 = f1.shape[0]
    xv = x.reshape(nb, ncb, R, DFT_G, SUBLANES, LANES)
    out = pl.pallas_call(
        _dft1_kernel, grid=(nb, ncb, DFT_G),
        in_specs=[pl.BlockSpec((kp2, R), lambda b, c, g: (0, 0)),
                  pl.BlockSpec((None, None, R, None, SUBLANES, LANES), lambda b, c, g: (b, c, 0, g, 0, 0))],
        out_specs=pl.BlockSpec((None, None, None, SUBLANES, kp2, LANES), lambda b, c, g: (b, c, g, 0, 0, 0)),
        out_shape=jax.ShapeDtypeStruct((nb, ncb, DFT_G, SUBLANES, kp2, LANES), F32),
        compiler_params=_cparams(("parallel", "parallel", "parallel")), name="dft_outer",
    )(f1, xv)
    return out.reshape(nb, ncb, DFT_N2, kp2, LANES)


def _inner_fwd(m1_ref, are_ref, aim_ref, kk):
    ar = are_ref[:, kk, :].astype(BF16)
    ai = aim_ref[:, kk, :].astype(BF16)
    return _dot(m1_ref[kk, :, 0:DFT_N2], ar) + _dot(m1_ref[kk, :, DFT_N2:], ai)


def _spec_kernel(m1_ref, are_ref, aim_ref, out_ref):
    for kk in range(SUBLANES):
        x = _inner_fwd(m1_ref, are_ref, aim_ref, kk)
        out_ref[kk, 0] = x[:DFT_N2]
        out_ref[kk, 1] = x[DFT_N2:]


def _filter_spectrum(a, m1):
    depth, ncb, n2, kp2, _ = a.shape
    kp = kp2 // 2
    kg = kp // SUBLANES
    ablk = lambda off: pl.BlockSpec((None, None, n2, SUBLANES, LANES), lambda k, l, c: (l, c, 0, off + k, 0))
    return pl.pallas_call(
        _spec_kernel, grid=(kg, depth, ncb),
        in_specs=[pl.BlockSpec((SUBLANES, 2 * n2, 2 * n2), lambda k, l, c: (k, 0, 0)), ablk(0), ablk(kg)],
        out_specs=pl.BlockSpec((None, None, SUBLANES, 2, n2, LANES), lambda k, l, c: (l, c, k, 0, 0, 0)),
        out_shape=jax.ShapeDtypeStruct((depth, ncb, kp, 2, n2, LANES), F32),
        compiler_params=_cparams(("parallel", "parallel", "parallel")), name="filter_spectrum",
    )(m1, a, a)


def _mid_kernel(m1_ref, m2_ref, kf_ref, are_ref, aim_ref, out_ref):
    for kk in range(SUBLANES):
        x = _inner_fwd(m1_ref, are_ref, aim_ref, kk)
        xr, xi = x[:DFT_N2], x[DFT_N2:]
        kr, ki = kf_ref[kk, 0], kf_ref[kk, 1]
        yr = (xr * kr - xi * ki).astype(BF16)
        yi = (xr * ki + xi * kr).astype(BF16)
        y = _dot(m2_ref[kk, :, 0:DFT_N2], yr) + _dot(m2_ref[kk, :, DFT_N2:], yi)
        out_ref[0, kk] = y[:DFT_N2]
        out_ref[1, kk] = y[DFT_N2:]


def _conv_mid(a, kf, m1, m2, layer, order):
    nb, ncb, n2, kp2, _ = a.shape
    kp = kp2 // 2
    kg = kp // SUBLANES
    mspec = pl.BlockSpec((SUBLANES, 2 * n2, 2 * n2), lambda k, c, b: (k, 0, 0))
    ablk = lambda off: pl.BlockSpec((None, None, n2, SUBLANES, LANES), lambda k, c, b: (b, c, 0, off + k, 0))
    return pl.pallas_call(
        _mid_kernel, grid=(kg, ncb, nb),
        in_specs=[mspec, mspec,
                  pl.BlockSpec((None, None, SUBLANES, 2, n2, LANES),
                               lambda k, c, b: (layer, order * ncb + c, k, 0, 0, 0)),
                  ablk(0), ablk(kg)],
        out_specs=pl.BlockSpec((None, None, 2, SUBLANES, n2, LANES), lambda k, c, b: (b, c, 0, k, 0, 0)),
        out_shape=jax.ShapeDtypeStruct((nb, ncb, 2, kp, n2, LANES), F32),
        compiler_params=_cparams(("parallel", "parallel", "arbitrary")), name="conv_mid",
    )(m1, m2, kf, a, a)


def _gate_kernel(g_ref, b_ref, z_ref, gate_ref, bias_ref, out_ref):
    bs = [b_ref[:, p, :].astype(BF16) for p in range(SUBLANES)]
    y = _dot(g_ref[...], jnp.concatenate(bs, axis=1))
    bias = bias_ref[...]
    for p in range(SUBLANES):
        yp = y[:, p * LANES:(p + 1) * LANES]
        out_ref[:, p, :] = gate_ref[:, p, :] * (yp + z_ref[:, p, :] * bias)


def _conv_out(bsp, ginv, z, gate, bias):
    nb, ncb, _, kp, n2, _ = bsp.shape
    R = ginv.shape[0]
    bv = bsp.reshape(nb, ncb, 2 * kp, DFT_G, SUBLANES, LANES)
    view = lambda t: t.reshape(nb, ncb, R, DFT_G, SUBLANES, LANES)
    tile = pl.BlockSpec((None, None, R, None, SUBLANES, LANES), lambda b, c, g: (b, c, 0, g, 0, 0))
    out = pl.pallas_call(
        _gate_kernel, grid=(nb, ncb, DFT_G),
        in_specs=[pl.BlockSpec((R, 2 * kp), lambda b, c, g: (0, 0)),
                  pl.BlockSpec((None, None, 2 * kp, None, SUBLANES, LANES), lambda b, c, g: (b, c, 0, g, 0, 0)),
                  tile, tile, pl.BlockSpec((None, 1, LANES), lambda b, c, g: (c, 0, 0))],
        out_specs=tile, out_shape=jax.ShapeDtypeStruct((nb, ncb, R, DFT_G, SUBLANES, LANES), F32),
        compiler_params=_cparams(("parallel", "parallel", "parallel")), name="conv_gate",
    )(ginv, bv, view(z), view(gate), bias)
    return out.reshape(z.shape)


def _long_conv_gate(z, gate, kf, bias, tabs, layer, order):
    a = _dft1(z, tabs["f1d"])
    bsp = _conv_mid(a, kf, tabs["m1"], tabs["m2"], layer, order)
    return _conv_out(bsp, tabs["ginv"], z, gate, bias.reshape(HY_CB, 1, LANES))


def _attn_kernel(flag_ref, qt_ref, k_ref, vt_ref, g_ref, o_ref, acc_scr, m_scr, *, nk, tk, unroll):
    pair = pl.program_id(0) * pl.num_programs(1) + pl.program_id(1)
    outs = []
    for hh in range(2):
        qt = qt_ref[hh]
        fast = flag_ref[2 * pair + hh] == 1
        acc_scr[...] = jnp.zeros(acc_scr.shape, F32)

        @pl.when(fast)
        def _(hh=hh, qt=qt):
            def body(j, carry):
                pv = None
                for u in range(unroll):
                    c = j * unroll + u
                    kk = k_ref[hh, pl.ds(pl.multiple_of(c * tk, tk), tk), :]
                    pt = jnp.exp2(_dot(kk, qt)).astype(BF16)
                    d = _dot(vt_ref[hh, c], pt)
                    pv = d if pv is None else pv + d
                acc_scr[...] += pv
                return carry
            lax.fori_loop(0, nk // unroll, body, 0)

        @pl.when(jnp.logical_not(fast))
        def _(hh=hh, qt=qt):
            m_scr[...] = jnp.full(m_scr.shape, -jnp.inf, F32)

            def body(j, carry):
                kk = k_ref[hh, pl.ds(pl.multiple_of(j * tk, tk), tk), :]
                s = _dot(kk, qt)
                m_prev = m_scr[...]
                m_new = jnp.maximum(m_prev, jnp.max(s, axis=0, keepdims=True))
                pt = jnp.exp2(s - m_new).astype(BF16)
                acc_scr[...] = jnp.exp2(m_prev - m_new) * acc_scr[...] + _dot(vt_ref[hh, j], pt)
                m_scr[...] = m_new
                return carry
            lax.fori_loop(0, nk, body, 0)

        acc = acc_scr[...]
        o = acc[:V_HEAD] / acc[V_HEAD:V_HEAD + 1]
        ms = jnp.mean(o * o, axis=0, keepdims=True)
        outs.append(o * lax.rsqrt(ms + EPS))
    ot = jnp.concatenate(outs, axis=0)
    o_ref[...] = (ot.T * g_ref[...]).astype(o_ref.dtype)


def _attention(flags, qt, k, vt, g_attn, nb, L):
    H = MLA_HEADS
    tq = min(512, L)
    nk, tk = vt.shape[2], vt.shape[4]
    grid_spec = pltpu.PrefetchScalarGridSpec(
        num_scalar_prefetch=1, grid=(nb, H // 2, L // tq),
        in_specs=[pl.BlockSpec((None, 2, HEAD_PAD, tq), lambda b, h, i, f: (b, h, 0, i)),
                  pl.BlockSpec((None, 2, L, HEAD_PAD), lambda b, h, i, f: (b, h, 0, 0)),
                  pl.BlockSpec((None, 2, nk, HEAD_PAD, tk), lambda b, h, i, f: (b, h, 0, 0, 0)),
                  pl.BlockSpec((1, 2 * V_HEAD), lambda b, h, i, f: (0, h))],
        out_specs=pl.BlockSpec((None, tq, 2 * V_HEAD), lambda b, h, i, f: (b, i, h)),
        scratch_shapes=[pltpu.VMEM((HEAD_PAD, tq), F32), pltpu.VMEM((1, tq), F32)])
    return pl.pallas_call(
        functools.partial(_attn_kernel, nk=nk, tk=tk, unroll=math.gcd(nk, ATTN_UNROLL)), grid_spec=grid_spec,
        out_shape=jax.ShapeDtypeStruct((nb, L, ATTN_WIDTH), BF16),
        compiler_params=_cparams(("parallel", "parallel", "arbitrary")), name="attention",
    )(flags, qt, k, vt, g_attn)


def _fast_flags(stats):
    H = MLA_HEADS
    qn = jnp.max(stats[:, :, 0:H, :], axis=(1, 3))
    kn = jnp.max(stats[:, :, H:2 * H, :], axis=(1, 3))
    vm = jnp.max(stats[:, :, 2 * H, :], axis=(1, 2))
    ok = jnp.logical_and(qn * kn <= FAST_S_MAX * FAST_S_MAX, (vm <= FAST_V_MAX)[:, None])
    return ok.astype(jnp.int32).reshape(-1)


def _mix_mlp_kernel(x_ref, zh_ref, an_ref, ghy_ref, gsum_ref, wo_ref, gpost_ref, gmpre_ref, wup_ref,
                    wdn_ref, gmpost_ref, out_ref):
    zh = jnp.concatenate([zh_ref[j] for j in range(HY_CB)], axis=1)
    ms = _dot((zh * zh).astype(BF16), gsum_ref[...]) * (HY_GROUPS / HY_WIDTH)
    hn = (zh * lax.rsqrt(ms + EPS) * ghy_ref[...]).astype(BF16)
    mix = _dot(hn, wo_ref[0:HY_WIDTH, :]) + _dot(an_ref[...], wo_ref[HY_WIDTH:, :])
    x = x_ref[...] + _rms(mix, gpost_ref[...])
    h = _rms(x, gmpre_ref[...]).astype(BF16)
    up = jnp.maximum(_dot(h, wup_ref[...]), 0.0)
    m = _dot((up * up).astype(BF16), wdn_ref[...])
    out_ref[...] = x + _rms(m, gmpost_ref[...])


def _mix_mlp(x, zh, an, lw, tabs, L):
    M, D = x.shape
    tm = min(256, L)
    tps = L // tm
    dff = lw["wup"].shape[1]
    mw = HY_WIDTH + ATTN_WIDTH
    const = lambda *shape: pl.BlockSpec(shape, lambda i: (0,) * len(shape), pipeline_mode=pl.Buffered(1))
    rows = lambda w: pl.BlockSpec((tm, w), lambda i: (i, 0))
    return pl.pallas_call(
        _mix_mlp_kernel, grid=(M // tm,),
        in_specs=[rows(D), pl.BlockSpec((None, HY_CB, tm, LANES), lambda i: (i // tps, 0, i % tps, 0)),
                  rows(ATTN_WIDTH), const(1, HY_WIDTH), const(HY_WIDTH, HY_WIDTH),
                  const(mw, D), const(1, D), const(1, D), const(D, dff), const(dff, D), const(1, D)],
        out_specs=rows(D), out_shape=jax.ShapeDtypeStruct((M, D), F32),
        compiler_params=_cparams(("parallel",)), name="mix_mlp",
    )(x, zh, an, lw["g_hy"], tabs["gsum"], lw["wo"], lw["g_post"], lw["g_mpre"], lw["wup"], lw["wdn"],
      lw["g_mpost"])


def _tables(L):
    n = 2 * L
    n2 = DFT_N2
    n1 = n // n2
    nh = n1 // 2
    kp = -(-(nh + 1) // SUBLANES) * SUBLANES
    two_pi = 2.0 * math.pi

    k1 = jnp.arange(kp, dtype=jnp.int32)
    valid = (k1 <= nh)
    def outer(ncols):
        nn = jnp.arange(ncols, dtype=jnp.int32)
        ang = ((k1[:, None] * nn[None, :]) % n1).astype(F32) * (two_pi / n1)
        c = jnp.where(valid[:, None], jnp.cos(ang), 0.0)
        s = jnp.where(valid[:, None], -jnp.sin(ang), 0.0)
        return jnp.concatenate([c, s], axis=0).astype(BF16)
    f1d = outer(nh)
    f1f = outer(n1)
    nn = jnp.arange(nh, dtype=jnp.int32)
    ang = ((nn[:, None] * k1[None, :]) % n1).astype(F32) * (two_pi / n1)
    wgt = jnp.where(valid, jnp.where((k1 == 0) | (k1 == nh), 1.0, 2.0), 0.0) / n
    ginv = jnp.concatenate([jnp.cos(ang) * wgt[None, :], -jnp.sin(ang) * wgt[None, :]], axis=1).astype(BF16)
    a2 = jnp.arange(n2, dtype=jnp.int32)
    idx = (a2[None, :, None] * a2[None, None, :] * n1 + a2[None, None, :] * k1[:, None, None]) % n
    ph = idx.astype(F32) * (two_pi / n)
    gr, gi = jnp.cos(ph), -jnp.sin(ph)
    m1 = jnp.concatenate([jnp.concatenate([gr, -gi], axis=2), jnp.concatenate([gi, gr], axis=2)], axis=1)
    m2 = jnp.swapaxes(m1, 1, 2)
    inv = 1.0 / (ROPE_BASE ** (jnp.arange(0, QK_ROPE, 2, dtype=F32) / QK_ROPE))
    ang = jnp.arange(L, dtype=F32)[:, None] * inv[None, :]
    cos, sin = jnp.cos(ang), jnp.sin(ang)
    pad = HEAD_PAD - QK_NOPE - QK_ROPE
    qcos = jnp.concatenate([jnp.ones((L, QK_NOPE), F32), cos, cos, jnp.ones((L, pad), F32)], axis=1)
    qsin = jnp.concatenate([jnp.zeros((L, QK_NOPE), F32), sin, sin, jnp.zeros((L, pad), F32)], axis=1)
    kcs = jnp.concatenate([cos, cos, sin, sin, jnp.zeros((L, HEAD_PAD - 2 * QK_ROPE), F32)], axis=1)
    cc = jnp.arange(MLA_HEADS * HEAD_PAD)
    src = jnp.arange(HEAD_PAD)
    pk = ((cc[None, :] % HEAD_PAD) - QK_NOPE == src[:, None]) & (src[:, None] < QK_ROPE)
    grp = jnp.arange(HY_WIDTH) // (HY_WIDTH // HY_GROUPS)
    gsum = (grp[:, None] == grp[None, :]).astype(BF16)
    t = jnp.linspace(0.0, 1.0, L, dtype=F32)[:, None]
    omega = (two_pi / L) * jnp.arange(L, dtype=F32)
    bands = jnp.linspace(1e-4, HY_BANDS - 1, HY_BANDS, dtype=F32)
    phase = omega[:, None] * bands[None, :]
    z = jnp.concatenate([t, jnp.cos(phase), -jnp.sin(phase), jnp.zeros((L, HY_EMB_PAD - HY_EMB), F32)], axis=-1)
    ztab = jnp.concatenate([z, z[:1], z[:0:-1]], axis=0)
    return dict(f1d=f1d, f1f=f1f, ginv=ginv, m1=m1.astype(BF16), m2=m2.astype(BF16), qcos_t=qcos.T,
                qsin_t=qsin.T, kcs=kcs, pk=pk.astype(BF16), gsum=gsum, ztab=ztab)


def _rot_half_cols(w):
    half = QK_ROPE // 2
    return jnp.concatenate([-w[..., half:], w[..., :half]], axis=-1)


def _layer_weights(i, p):
    D = p["w_in"].shape[1]
    H = MLA_HEADS
    hw3 = 3 * HY_WIDTH
    w_in = p["w_in"][i]
    kpe = w_in[:, hw3 + Q_RANK + KV_RANK:]
    win = jnp.concatenate([w_in, _rot_half_cols(kpe),
                           jnp.zeros((D, HEAD_PAD - 2 * QK_ROPE), F32)], axis=1).astype(BF16)
    dq = QK_NOPE + QK_ROPE
    wq = p["mla_w_uq"][i].reshape(Q_RANK, H, dq)
    zq = jnp.zeros((Q_RANK, H, HEAD_PAD - dq), F32)
    wqa = jnp.concatenate([wq, zq], axis=2).reshape(Q_RANK, H * HEAD_PAD)
    wqb = jnp.concatenate([jnp.zeros((Q_RANK, H, QK_NOPE), F32), _rot_half_cols(wq[..., QK_NOPE:]), zq],
                          axis=2).reshape(Q_RANK, H * HEAD_PAD)
    wkv = p["mla_w_ukv"][i].reshape(KV_RANK, H, QK_NOPE + V_HEAD)
    wk = jnp.concatenate([wkv[..., :QK_NOPE], jnp.zeros((KV_RANK, H, HEAD_PAD - QK_NOPE), F32)], axis=2)
    wv = jnp.concatenate([wkv[..., QK_NOPE:], jnp.zeros((KV_RANK, H, HEAD_PAD - V_HEAD), F32)], axis=2)
    row = lambda a: a.reshape(1, -1)
    return dict(
        win=win, g_pre=row(p["norm_mix_pre"][i]), conv_w=p["hy_conv_w"][i], conv_b=row(p["hy_conv_b"][i]),
        q_g=row(p["mla_q_norm"][i]), wqat=wqa.T.astype(BF16), wqbt=wqb.T.astype(BF16),
        kv_g=row(p["mla_kv_norm"][i]), wk=wk.reshape(KV_RANK, H * HEAD_PAD).astype(BF16),
        wvt=wv.reshape(KV_RANK, H * HEAD_PAD).T.astype(BF16),
        g_hy=row(p["grp_norm_hy"][i]), g_attn=row(p["grp_norm_attn"][i]), wo=p["w_out"][i].astype(BF16),
        g_post=row(p["norm_mix_post"][i]), g_mpre=row(p["norm_mlp_pre"][i]), g_mpost=row(p["norm_mlp_post"][i]),
        wup=p["w_mlp_up"][i].astype(BF16), wdn=p["w_mlp_down"][i].astype(BF16),
    )


def _filter_weights(p):
    depth = p["hy_ffn_w1"].shape[0]
    oc = HY_ORDER * HY_WIDTH
    w1 = jnp.pad(p["hy_ffn_w1"], ((0, 0), (0, HY_EMB_PAD - HY_EMB), (0, 0)))
    w3 = p["hy_ffn_w3"].reshape(depth, HY_FFN, HY_ORDER, 2, HY_WIDTH).transpose(0, 3, 1, 2, 4)
    dec = p["hy_decay"].transpose(0, 2, 1, 3).reshape(depth, 2, 1, oc)
    return dict(w1=w1, b1=p["hy_ffn_b1"][:, None, :], sf=p["hy_sin_freq"], w2=p["hy_ffn_w2"],
                b2=p["hy_ffn_b2"][:, None, :], w3=w3.reshape(depth, 2, HY_FFN, oc), dec=dec)


def kernel(x_prompt, x_sample, w_in, hy_conv_w, hy_conv_b, hy_ffn_w1, hy_ffn_b1, hy_ffn_w2, hy_ffn_b2,
           hy_ffn_w3, hy_sin_freq, hy_decay, hy_bias, mla_q_norm, mla_w_uq, mla_kv_norm, mla_w_ukv,
           grp_norm_hy, grp_norm_attn, w_out, norm_mix_pre, norm_mix_post, norm_mlp_pre, norm_mlp_post,
           w_mlp_up, w_mlp_down):
    p = dict(w_in=w_in, hy_conv_w=hy_conv_w, hy_conv_b=hy_conv_b, hy_ffn_w1=hy_ffn_w1, hy_ffn_b1=hy_ffn_b1,
             hy_ffn_w2=hy_ffn_w2, hy_ffn_b2=hy_ffn_b2, hy_ffn_w3=hy_ffn_w3, hy_sin_freq=hy_sin_freq,
             hy_decay=hy_decay, hy_bias=hy_bias, mla_q_norm=mla_q_norm, mla_w_uq=mla_w_uq,
             mla_kv_norm=mla_kv_norm, mla_w_ukv=mla_w_ukv, grp_norm_hy=grp_norm_hy,
             grp_norm_attn=grp_norm_attn, w_out=w_out, norm_mix_pre=norm_mix_pre, norm_mix_post=norm_mix_post,
             norm_mlp_pre=norm_mlp_pre, norm_mlp_post=norm_mlp_post, w_mlp_up=w_mlp_up, w_mlp_down=w_mlp_down)
    bp, L, D = x_prompt.shape
    bs, Ls, _ = x_sample.shape
    assert L == Ls and L % (DFT_N2 * SUBLANES) == 0
    nb = bp + bs
    depth = w_in.shape[0]

    tabs = _tables(L)
    kc = _filters(tabs["ztab"], _filter_weights(p), L)
    kf = _filter_spectrum(_dft1(kc, tabs["f1f"]), tabs["m1"])

    x = jnp.concatenate([x_prompt.reshape(bp * L, D), x_sample.reshape(bs * L, D)], axis=0)
    for i in range(depth):
        lw = _layer_weights(i, p)
        v, x1, x2, qt, k, vt, stats = _inproj(x, lw, tabs, nb, L)
        z = _long_conv_gate(v, x1, kf, hy_bias[i, 0], tabs, i, 0)
        z = _long_conv_gate(z, x2, kf, hy_bias[i, 1], tabs, i, 1)
        an = _attention(_fast_flags(stats), qt, k, vt, lw["g_attn"], nb, L).reshape(nb * L, ATTN_WIDTH)
        x = _mix_mlp(x, z, an, lw, tabs, L)
    return (x[:bp * L].reshape(bp, L, D), x[bp * L:].reshape(bs, L, D))
```

```python
import functools
import math

import jax
import jax.numpy as jnp
from jax import lax
from jax.experimental import pallas as pl
from jax.experimental.pallas import tpu as pltpu

F32 = jnp.float32
BF16 = jnp.bfloat16

EPS = 1e-6
HY_WIDTH = 512
HY_GROUPS = 8
HY_ORDER = 2
HY_BANDS = 16
HY_EMB = 2 * HY_BANDS + 1
HY_EMB_PAD = 40
HY_FFN = 64
MLA_HEADS = 8
QK_NOPE = 64
QK_ROPE = 32
V_HEAD = 64
Q_RANK = 256
KV_RANK = 128
ROPE_BASE = 10000.0
HEAD_PAD = 128
V_PAD = 80
ATTN_WIDTH = MLA_HEADS * V_HEAD

LANES = 128
SUBLANES = 8
HY_CB = HY_WIDTH // LANES
DFT_N2 = 128
DFT_G = DFT_N2 // SUBLANES
HALO = 16
VMEM_LIMIT = 56 * 1024 * 1024
ATTN_UNROLL = 16
STAT_ROWS = 24

FAST_S_MAX = 64.0
FAST_V_MAX = 2.0 ** 30


def _cparams(sem):
    return pltpu.CompilerParams(dimension_semantics=sem, vmem_limit_bytes=VMEM_LIMIT)


def _dot(a, b):
    return jnp.dot(a, b, preferred_element_type=F32)


def _dot_nt(a, b):
    return lax.dot_general(a, b, (((1,), (1,)), ((), ())), preferred_element_type=F32)


def _rms(x, g):
    return x * lax.rsqrt(jnp.mean(x * x, axis=-1, keepdims=True) + EPS) * g


def _store_grouped(out_ref, val, lane0, ncb):
    for j in range(ncb):
        for nl in range(val.shape[0] // DFT_N2):
            for g in range(DFT_G):
                r0 = nl * DFT_N2 + g * SUBLANES
                out_ref[j, g, nl * SUBLANES:(nl + 1) * SUBLANES, :] = (
                    val[r0:r0 + SUBLANES, lane0 + j * LANES:lane0 + (j + 1) * LANES])


def _fold_lanes(row):
    parts = [row[:, j * LANES:(j + 1) * LANES] for j in range(row.shape[1] // LANES)]
    return functools.reduce(jnp.maximum, parts)


def _inproj_kernel(xp_ref, x_ref, xn_ref, gpre_ref, win_ref, cw_ref, cb_ref, qg_ref, wqat_ref,
                   wqbt_ref, kvg_ref, wk_ref, pk_ref, wvt_ref, qcos_ref, qsin_ref, kcs_ref,
                   v_out, x1_out, x2_out, qt_out, k_out, vt_out, st_out, h_scr, pe_scr,
                   *, tm, tiles_per_seq, scale):
    i = pl.program_id(0)
    t_idx = i % tiles_per_seq
    g = gpre_ref[...]
    h_scr[0:HALO, :] = _rms(xp_ref[...], g).astype(BF16)
    h_scr[HALO:HALO + tm, :] = _rms(x_ref[...], g).astype(BF16)
    h_scr[HALO + tm:2 * HALO + tm, :] = _rms(xn_ref[...], g).astype(BF16)
    proj = _dot(h_scr[...], win_ref[...])

    hw3 = 3 * HY_WIDTH
    pe_scr[...] = proj[:, :hw3]
    row = lax.broadcasted_iota(jnp.int32, (tm, 1), 0)
    prev = pe_scr[HALO - 1:HALO - 1 + tm, :]
    cur = pe_scr[HALO:HALO + tm, :]
    nxt = pe_scr[HALO + 1:HALO + 1 + tm, :]
    prev = jnp.where(jnp.logical_and(row == 0, t_idx == 0), 0.0, prev)
    nxt = jnp.where(jnp.logical_and(row == tm - 1, t_idx == tiles_per_seq - 1), 0.0, nxt)
    u = prev * cw_ref[0:1, :] + cur * cw_ref[1:2, :] + nxt * cw_ref[2:3, :] + cb_ref[...]
    for o, out in enumerate((v_out, x1_out, x2_out)):
        _store_grouped(out, u, o * HY_WIDTH, HY_CB)

    core = proj[HALO:HALO + tm, :]
    cq = core[:, hw3:hw3 + Q_RANK]
    cqn = _rms(cq, qg_ref[...]).astype(BF16)
    qat = _dot_nt(wqat_ref[...], cqn)
    qbt = _dot_nt(wqbt_ref[...], cqn)
    qcos = qcos_ref[...]
    qsin = qsin_ref[...]
    for h in range(MLA_HEADS):
        sl = slice(h * HEAD_PAD, (h + 1) * HEAD_PAD)
        qh = ((qat[sl, :] * qcos + qbt[sl, :] * qsin) * scale).astype(BF16)
        qt_out[h] = qh
        qf = qh.astype(F32)
        st_out[h:h + 1, :] = _fold_lanes(jnp.sum(qf * qf, axis=0, keepdims=True))

    ckv = core[:, hw3 + Q_RANK:hw3 + Q_RANK + KV_RANK]
    ckvn = _rms(ckv, kvg_ref[...]).astype(BF16)
    kp = core[:, hw3 + Q_RANK + KV_RANK:] * kcs_ref[...]
    kp = kp + pltpu.roll(kp, HEAD_PAD - QK_ROPE, axis=1)
    kk = _dot(ckvn, wk_ref[...]) + _dot(kp.astype(BF16), pk_ref[...])
    for h in range(MLA_HEADS):
        kh = kk[:, h * HEAD_PAD:(h + 1) * HEAD_PAD].astype(BF16)
        k_out[h] = kh
        kf = kh.astype(F32)
        kn = jnp.max(jnp.sum(kf * kf, axis=1, keepdims=True), axis=0, keepdims=True)
        st_out[MLA_HEADS + h:MLA_HEADS + h + 1, :] = jnp.broadcast_to(kn, (1, LANES))
    vt = _dot_nt(wvt_ref[...], ckvn)
    st_out[2 * MLA_HEADS:2 * MLA_HEADS + 1, :] = _fold_lanes(jnp.max(jnp.abs(vt), axis=0, keepdims=True))
    st_out[2 * MLA_HEADS + 1:, :] = jnp.zeros((STAT_ROWS - 2 * MLA_HEADS - 1, LANES), F32)
    frow = lax.broadcasted_iota(jnp.int32, (MLA_HEADS * V_PAD, 1), 0)
    vt = vt + jnp.where(frow % V_PAD == V_HEAD, 1.0, 0.0)
    for h in range(MLA_HEADS):
        vt_out[h] = vt[h * V_PAD:(h + 1) * V_PAD, :].astype(BF16)


def _inproj(x, lw, tabs, nb, L):
    M, D = x.shape
    tm = min(512, L)
    tps = L // tm
    nt = M // tm
    hb = tm // HALO
    nhb = M // HALO
    H = MLA_HEADS
    wcols = lw["win"].shape[1]
    const = lambda *shape: pl.BlockSpec(shape, lambda i: (0,) * len(shape))
    tab_t = pl.BlockSpec((HEAD_PAD, tm), lambda i: (0, i % tps))
    in_specs = [
        pl.BlockSpec((HALO, D), lambda i: (jnp.maximum(i * hb - 1, 0), 0)),
        pl.BlockSpec((tm, D), lambda i: (i, 0)),
        pl.BlockSpec((HALO, D), lambda i: (jnp.minimum((i + 1) * hb, nhb - 1), 0)),
        const(1, D), const(D, wcols), const(3, 3 * HY_WIDTH), const(1, 3 * HY_WIDTH),
        const(1, Q_RANK), const(H * HEAD_PAD, Q_RANK), const(H * HEAD_PAD, Q_RANK),
        const(1, KV_RANK), const(KV_RANK, H * HEAD_PAD), const(HEAD_PAD, H * HEAD_PAD),
        const(H * V_PAD, KV_RANK),
        tab_t, tab_t, pl.BlockSpec((tm, HEAD_PAD), lambda i: (i % tps, 0)),
    ]
    hy_spec = pl.BlockSpec((None, HY_CB, DFT_G, tm // DFT_G, LANES), lambda i: (i // tps, 0, 0, i % tps, 0))
    out_specs = [
        hy_spec, hy_spec, hy_spec,
        pl.BlockSpec((None, H, HEAD_PAD, tm), lambda i: (i // tps, 0, 0, i % tps)),
        pl.BlockSpec((None, H, tm, HEAD_PAD), lambda i: (i // tps, 0, i % tps, 0)),
        pl.BlockSpec((None, H, None, V_PAD, tm), lambda i: (i // tps, 0, i % tps, 0, 0)),
        pl.BlockSpec((None, None, STAT_ROWS, LANES), lambda i: (i // tps, i % tps, 0, 0)),
    ]
    hy_shape = jax.ShapeDtypeStruct((nb, HY_CB, DFT_G, L // DFT_G, LANES), F32)
    out_shape = [
        hy_shape, hy_shape, hy_shape,
        jax.ShapeDtypeStruct((nb, H, HEAD_PAD, L), BF16),
        jax.ShapeDtypeStruct((nb, H, L, HEAD_PAD), BF16),
        jax.ShapeDtypeStruct((nb, H, tps, V_PAD, tm), BF16),
        jax.ShapeDtypeStruct((nb, tps, STAT_ROWS, LANES), F32),
    ]
    scale = float((QK_NOPE + QK_ROPE) ** -0.5 * math.log2(math.e))
    return pl.pallas_call(
        functools.partial(_inproj_kernel, tm=tm, tiles_per_seq=tps, scale=scale),
        grid=(nt,), in_specs=in_specs, out_specs=out_specs, out_shape=out_shape,
        scratch_shapes=[pltpu.VMEM((tm + 2 * HALO, D), BF16),
                        pltpu.VMEM((tm + 2 * HALO, 3 * HY_WIDTH), F32)],
        compiler_params=_cparams(("parallel",)), name="inproj",
    )(x, x, x, lw["g_pre"], lw["win"], lw["conv_w"], lw["conv_b"], lw["q_g"], lw["wqat"], lw["wqbt"],
      lw["kv_g"], lw["wk"], tabs["pk"], lw["wvt"], tabs["qcos_t"], tabs["qsin_t"], tabs["kcs"])


def _filter_kernel(z_ref, w1_ref, b1_ref, sf_ref, w2_ref, b2_ref, w3_ref, dec_ref, out_ref, *, rb, L):
    hi = lax.Precision.HIGHEST
    r = pl.program_id(1)
    z = z_ref[...]
    h = jnp.sin(sf_ref[0:1, :] * (jnp.dot(z, w1_ref[...], precision=hi, preferred_element_type=F32) + b1_ref[...]))
    h = jnp.sin(sf_ref[1:2, :] * (jnp.dot(h, w2_ref[...], precision=hi, preferred_element_type=F32) + b2_ref[...]))
    k = jnp.dot(h, w3_ref[...], precision=hi, preferred_element_type=F32)
    k = k * jnp.exp(-z[:, 0:1] * jnp.abs(dec_ref[...]))
    row = r * rb + lax.broadcasted_iota(jnp.int32, (rb, 1), 0)
    k = jnp.where(row == L, 0.0, k)
    _store_grouped(out_ref, k, 0, out_ref.shape[0])


def _filters(ztab, fw, L):
    depth = fw["w1"].shape[0]
    rows = 2 * L
    rb = min(512, L)
    nblk = rows // rb
    half = nblk // 2
    oc = HY_ORDER * HY_WIDTH
    lay = lambda *shape: pl.BlockSpec((None,) + shape, lambda l, r: (l,) + (0,) * len(shape))
    in_specs = [
        pl.BlockSpec((rb, HY_EMB_PAD), lambda l, r: (r, 0)),
        lay(HY_EMB_PAD, HY_FFN), lay(1, HY_FFN), lay(2, HY_FFN), lay(HY_FFN, HY_FFN), lay(1, HY_FFN),
        pl.BlockSpec((None, None, HY_FFN, oc), lambda l, r: (l, r // half, 0, 0)),
        pl.BlockSpec((None, None, 1, oc), lambda l, r: (l, r // half, 0, 0)),
    ]
    return pl.pallas_call(
        functools.partial(_filter_kernel, rb=rb, L=L),
        grid=(depth, nblk), in_specs=in_specs,
        out_specs=pl.BlockSpec((None, oc // LANES, DFT_G, rb // DFT_G, LANES), lambda l, r: (l, 0, 0, r, 0)),
        out_shape=jax.ShapeDtypeStruct((depth, oc // LANES, DFT_G, rows // DFT_G, LANES), F32),
        compiler_params=_cparams(("parallel", "parallel")), name="hyena_filter",
    )(ztab, fw["w1"], fw["b1"], fw["sf"], fw["w2"], fw["b2"], fw["w3"], fw["dec"])


def _rows_of(ref, p, n, lead=()):
    return ref[lead + (pl.ds(p, n, stride=SUBLANES), slice(None))]


def _dft1_kernel(f_ref, x_ref, out_ref, *, kp):
    R = f_ref.shape[1]
    xs = [_rows_of(x_ref, p, R).astype(BF16) for p in range(SUBLANES)]
    res = _dot(f_ref[...], jnp.concatenate(xs, axis=1))
    for ri in range(2):
        for kg in range(kp // SUBLANES):
            r0 = ri * kp + kg * SUBLANES
            for p in range(SUBLANES):
                out_ref[ri, kg, p * SUBLANES:(p + 1) * SUBLANES, :] = (
                    res[r0:r0 + SUBLANES, p * LANES:(p + 1) * LANES])


def _dft1(x, f1):
    nb, ncb, _, rows, _ = x.shape
    kp = f1.shape[0] // 2
    kg = kp // SUBLANES
    return pl.pallas_call(
        functools.partial(_dft1_kernel, kp=kp), grid=(nb, ncb, DFT_G),
        in_specs=[pl.BlockSpec(f1.shape, lambda b, c, g: (0, 0)),
                  pl.BlockSpec((None, None, None, rows, LANES), lambda b, c, g: (b, c, g, 0, 0))],
        out_specs=pl.BlockSpec((None, None, 2, kg, SUBLANES * SUBLANES, LANES), lambda b, c, g: (b, c, 0, 0, g, 0)),
        out_shape=jax.ShapeDtypeStruct((nb, ncb, 2, kg, DFT_N2 * SUBLANES, LANES), F32),
        compiler_params=_cparams(("parallel", "parallel", "parallel")), name="dft_outer",
    )(f1, x)


def _inner_fwd(m1_ref, a_ref, kk):
    ar = _rows_of(a_ref, kk, DFT_N2, (0,)).astype(BF16)
    ai = _rows_of(a_ref, kk, DFT_N2, (1,)).astype(BF16)
    return _dot(m1_ref[kk], jnp.concatenate([ar, ai], axis=0))


def _spec_kernel(m1_ref, a_ref, out_ref):
    for kk in range(SUBLANES):
        x = _inner_fwd(m1_ref, a_ref, kk)
        out_ref[kk, 0] = x[:DFT_N2]
        out_ref[kk, 1] = x[DFT_N2:]


def _filter_spectrum(a, m1):
    depth, ncb, _, kg, rows, _ = a.shape
    n2 = DFT_N2
    return pl.pallas_call(
        _spec_kernel, grid=(kg, depth, ncb),
        in_specs=[pl.BlockSpec((SUBLANES, 2 * n2, 2 * n2), lambda k, l, c: (k, 0, 0)),
                  pl.BlockSpec((None, None, 2, None, rows, LANES), lambda k, l, c: (l, c, 0, k, 0, 0))],
        out_specs=pl.BlockSpec((None, None, SUBLANES, 2, n2, LANES), lambda k, l, c: (l, c, k, 0, 0, 0)),
        out_shape=jax.ShapeDtypeStruct((depth, ncb, kg * SUBLANES, 2, n2, LANES), F32),
        compiler_params=_cparams(("parallel", "parallel", "parallel")), name="filter_spectrum",
    )(m1, a)


def _mid_kernel(m1_ref, m2_ref, kf_ref, a_ref, out_ref):
    for kk in range(SUBLANES):
        x = _inner_fwd(m1_ref, a_ref, kk)
        xr, xi = x[:DFT_N2], x[DFT_N2:]
        kr, ki = kf_ref[kk, 0], kf_ref[kk, 1]
        yr = (xr * kr - xi * ki).astype(BF16)
        yi = (xr * ki + xi * kr).astype(BF16)
        y = _dot(m2_ref[kk], jnp.concatenate([yr, yi], axis=0))
        for ri in range(2):
            for g in range(DFT_G):
                r0 = ri * DFT_N2 + g * SUBLANES
                out_ref[ri, g, kk * SUBLANES:(kk + 1) * SUBLANES, :] = y[r0:r0 + SUBLANES]


def _conv_mid(a, kf, m1, m2, layer, order):
    nb, ncb, _, kg, rows, _ = a.shape
    n2 = DFT_N2
    mspec = pl.BlockSpec((SUBLANES, 2 * n2, 2 * n2), lambda k, c, b: (k, 0, 0))
    return pl.pallas_call(
        _mid_kernel, grid=(kg, ncb, nb),
        in_specs=[mspec, mspec,
                  pl.BlockSpec((None, None, SUBLANES, 2, n2, LANES),
                               lambda k, c, b: (layer, order * ncb + c, k, 0, 0, 0)),
                  pl.BlockSpec((None, None, 2, None, rows, LANES), lambda k, c, b: (b, c, 0, k, 0, 0))],
        out_specs=pl.BlockSpec((None, None, 2, DFT_G, SUBLANES * SUBLANES, LANES),
                               lambda k, c, b: (b, c, 0, 0, k, 0)),
        out_shape=jax.ShapeDtypeStruct((nb, ncb, 2, DFT_G, kg * SUBLANES * SUBLANES, LANES), F32),
        compiler_params=_cparams(("parallel", "parallel", "arbitrary")), name="conv_mid",
    )(m1, m2, kf, a)


def _gate_kernel(gre_ref, gim_ref, b_ref, z_ref, gate_ref, bias_ref, out_ref):
    R, kp = gre_ref.shape
    bre = jnp.concatenate([_rows_of(b_ref, p, kp, (0,)).astype(BF16) for p in range(SUBLANES)], axis=1)
    bim = jnp.concatenate([_rows_of(b_ref, p, kp, (1,)).astype(BF16) for p in range(SUBLANES)], axis=1)
    y = _dot(gre_ref[...], bre) + _dot(gim_ref[...], bim)
    bias = bias_ref[...]
    for p in range(SUBLANES):
        yp = y[:, p * LANES:(p + 1) * LANES]
        rows = (pl.ds(p, R, stride=SUBLANES), slice(None))
        out_ref[rows] = _rows_of(gate_ref, p, R) * (yp + _rows_of(z_ref, p, R) * bias)


def _conv_out(bsp, gre, gim, z, gate, bias):
    nb, ncb, _, _, krows, _ = bsp.shape
    rows = z.shape[3]
    tile = pl.BlockSpec((None, None, None, rows, LANES), lambda b, c, g: (b, c, g, 0, 0))
    return pl.pallas_call(
        _gate_kernel, grid=(nb, ncb, DFT_G),
        in_specs=[pl.BlockSpec(gre.shape, lambda b, c, g: (0, 0)), pl.BlockSpec(gim.shape, lambda b, c, g: (0, 0)),
                  pl.BlockSpec((None, None, 2, None, krows, LANES), lambda b, c, g: (b, c, 0, g, 0, 0)),
                  tile, tile, pl.BlockSpec((None, 1, LANES), lambda b, c, g: (c, 0, 0))],
        out_specs=tile, out_shape=jax.ShapeDtypeStruct(z.shape, F32),
        compiler_params=_cparams(("parallel", "parallel", "parallel")), name="conv_gate",
    )(gre, gim, bsp, z, gate, bias)


def _long_conv_gate(z, gate, kf, bias, tabs, layer, order):
    a = _dft1(z, tabs["f1d"])
    bsp = _conv_mid(a, kf, tabs["m1"], tabs["m2"], layer, order)
    return _conv_out(bsp, tabs["gre"], tabs["gim"], z, gate, bias.reshape(HY_CB, 1, LANES))


def _attn_kernel(flag_ref, qt_ref, k_ref, vt_ref, g_ref, o_ref, acc_scr, m_scr, s_scr, *, nk, tk, unroll):
    pair = pl.program_id(0) * pl.num_programs(1) + pl.program_id(1)
    outs = []
    for hh in range(2):
        qt = qt_ref[hh]
        fast = flag_ref[2 * pair + hh] == 1
        acc_scr[...] = jnp.zeros(acc_scr.shape, F32)

        @pl.when(fast)
        def _(hh=hh, qt=qt):
            def scores(c):
                return _dot(k_ref[hh, pl.ds(pl.multiple_of(c * tk, tk), tk), :], qt)

            s_scr[0] = scores(0)

            def body(j, carry):
                pv = None
                for u in range(unroll):
                    c = j * unroll + u
                    s_scr[(u + 1) % 2] = scores(jnp.minimum(c + 1, nk - 1))
                    pt = jnp.exp2(s_scr[u % 2]).astype(BF16)
                    d = _dot(vt_ref[hh, c], pt)
                    pv = d if pv is None else pv + d
                acc_scr[...] += pv
                return carry
            lax.fori_loop(0, nk // unroll, body, 0)

        @pl.when(jnp.logical_not(fast))
        def _(hh=hh, qt=qt):
            m_scr[...] = jnp.full(m_scr.shape, -jnp.inf, F32)

            def body(j, carry):
                kk = k_ref[hh, pl.ds(pl.multiple_of(j * tk, tk), tk), :]
                s = _dot(kk, qt)
                m_prev = m_scr[...]
                m_new = jnp.maximum(m_prev, jnp.max(s, axis=0, keepdims=True))
                pt = jnp.exp2(s - m_new).astype(BF16)
                acc_scr[...] = jnp.exp2(m_prev - m_new) * acc_scr[...] + _dot(vt_ref[hh, j], pt)
                m_scr[...] = m_new
                return carry
            lax.fori_loop(0, nk, body, 0)

        acc = acc_scr[...]
        o = acc[:V_HEAD] / acc[V_HEAD:V_HEAD + 1]
        ms = jnp.mean(o * o, axis=0, keepdims=True)
        outs.append(o * lax.rsqrt(ms + EPS))
    ot = jnp.concatenate(outs, axis=0)
    o_ref[...] = (ot.T * g_ref[...]).astype(o_ref.dtype)


def _attention(flags, qt, k, vt, g_attn, nb, L):
    H = MLA_HEADS
    tq = min(512, L)
    nk, tk = vt.shape[2], vt.shape[4]
    grid_spec = pltpu.PrefetchScalarGridSpec(
        num_scalar_prefetch=1, grid=(nb, H // 2, L // tq),
        in_specs=[pl.BlockSpec((None, 2, HEAD_PAD, tq), lambda b, h, i, f: (b, h, 0, i)),
                  pl.BlockSpec((None, 2, L, HEAD_PAD), lambda b, h, i, f: (b, h, 0, 0)),
                  pl.BlockSpec((None, 2, nk, V_PAD, tk), lambda b, h, i, f: (b, h, 0, 0, 0)),
                  pl.BlockSpec((1, 2 * V_HEAD), lambda b, h, i, f: (0, h))],
        out_specs=pl.BlockSpec((None, tq, 2 * V_HEAD), lambda b, h, i, f: (b, i, h)),
        scratch_shapes=[pltpu.VMEM((V_PAD, tq), F32), pltpu.VMEM((1, tq), F32), pltpu.VMEM((2, tk, tq), F32)])
    unroll = math.gcd(nk, ATTN_UNROLL)
    assert unroll % 2 == 0 or unroll == nk
    return pl.pallas_call(
        functools.partial(_attn_kernel, nk=nk, tk=tk, unroll=unroll), grid_spec=grid_spec,
        out_shape=jax.ShapeDtypeStruct((nb, L, ATTN_WIDTH), BF16),
        compiler_params=_cparams(("parallel", "parallel", "arbitrary")), name="attention",
    )(flags, qt, k, vt, g_attn)


def _fast_flags(stats):
    H = MLA_HEADS
    qn = jnp.max(stats[:, :, 0:H, :], axis=(1, 3))
    kn = jnp.max(stats[:, :, H:2 * H, :], axis=(1, 3))
    vm = jnp.max(stats[:, :, 2 * H, :], axis=(1, 2))
    ok = jnp.logical_and(qn * kn <= FAST_S_MAX * FAST_S_MAX, (vm <= FAST_V_MAX)[:, None])
    return ok.astype(jnp.int32).reshape(-1)


def _mix_mlp_kernel(x_ref, zh_ref, an_ref, ghy_ref, gsum_ref, wo_ref, gpost_ref, gmpre_ref, wup_ref,
                    wdn_ref, gmpost_ref, out_ref):
    zh = jnp.concatenate(
        [jnp.concatenate([zh_ref[j, g, nl * SUBLANES:(nl + 1) * SUBLANES, :] for j in range(HY_CB)], axis=1)
         for nl in range(zh_ref.shape[2] // SUBLANES) for g in range(DFT_G)], axis=0)
    ms = _dot((zh * zh).astype(BF16), gsum_ref[...]) * (HY_GROUPS / HY_WIDTH)
    hn = (zh * lax.rsqrt(ms + EPS) * ghy_ref[...]).astype(BF16)
    mix = _dot(hn, wo_ref[0:HY_WIDTH, :]) + _dot(an_ref[...], wo_ref[HY_WIDTH:, :])
    x = x_ref[...] + _rms(mix, gpost_ref[...])
    h = _rms(x, gmpre_ref[...]).astype(BF16)
    up = jnp.maximum(_dot(h, wup_ref[...]), 0.0)
    m = _dot((up * up).astype(BF16), wdn_ref[...])
    out_ref[...] = x + _rms(m, gmpost_ref[...])


def _mix_mlp(x, zh, an, lw, tabs, L):
    M, D = x.shape
    tm = min(256, L)
    tps = L // tm
    dff = lw["wup"].shape[1]
    mw = HY_WIDTH + ATTN_WIDTH
    const = lambda *shape: pl.BlockSpec(shape, lambda i: (0,) * len(shape), pipeline_mode=pl.Buffered(1))
    rows = lambda w: pl.BlockSpec((tm, w), lambda i: (i, 0))
    return pl.pallas_call(
        _mix_mlp_kernel, grid=(M // tm,),
        in_specs=[rows(D), pl.BlockSpec((None, HY_CB, DFT_G, tm // DFT_G, LANES),
                                        lambda i: (i // tps, 0, 0, i % tps, 0)),
                  rows(ATTN_WIDTH), const(1, HY_WIDTH), const(HY_WIDTH, HY_WIDTH),
                  const(mw, D), const(1, D), const(1, D), const(D, dff), const(dff, D), const(1, D)],
        out_specs=rows(D), out_shape=jax.ShapeDtypeStruct((M, D), F32),
        compiler_params=_cparams(("parallel",)), name="mix_mlp",
    )(x, zh, an, lw["g_hy"], tabs["gsum"], lw["wo"], lw["g_post"], lw["g_mpre"], lw["wup"], lw["wdn"],
      lw["g_mpost"])


def _tables(L):
    n = 2 * L
    n2 = DFT_N2
    n1 = n // n2
    nh = n1 // 2
    kp = -(-(nh + 1) // SUBLANES) * SUBLANES
    two_pi = 2.0 * math.pi

    k1 = jnp.arange(kp, dtype=jnp.int32)
    valid = (k1 <= nh)
    def outer(ncols):
        nn = jnp.arange(ncols, dtype=jnp.int32)
        ang = ((k1[:, None] * nn[None, :]) % n1).astype(F32) * (two_pi / n1)
        c = jnp.where(valid[:, None], jnp.cos(ang), 0.0)
        s = jnp.where(valid[:, None], -jnp.sin(ang), 0.0)
        return jnp.concatenate([c, s], axis=0).astype(BF16)
    f1d = outer(nh)
    f1f = outer(n1)
    nn = jnp.arange(nh, dtype=jnp.int32)
    ang = ((nn[:, None] * k1[None, :]) % n1).astype(F32) * (two_pi / n1)
    wgt = jnp.where(valid, jnp.where((k1 == 0) | (k1 == nh), 1.0, 2.0), 0.0) / n
    gre = (jnp.cos(ang) * wgt[None, :]).astype(BF16)
    gim = (-jnp.sin(ang) * wgt[None, :]).astype(BF16)
    a2 = jnp.arange(n2, dtype=jnp.int32)
    idx = (a2[None, :, None] * a2[None, None, :] * n1 + a2[None, None, :] * k1[:, None, None]) % n
    ph = idx.astype(F32) * (two_pi / n)
    gr, gi = jnp.cos(ph), -jnp.sin(ph)
    m1 = jnp.concatenate([jnp.concatenate([gr, -gi], axis=2), jnp.concatenate([gi, gr], axis=2)], axis=1)
    m2 = jnp.swapaxes(m1, 1, 2)
    inv = 1.0 / (ROPE_BASE ** (jnp.arange(0, QK_ROPE, 2, dtype=F32) / QK_ROPE))
    ang = jnp.arange(L, dtype=F32)[:, None] * inv[None, :]
    cos, sin = jnp.cos(ang), jnp.sin(ang)
    pad = HEAD_PAD - QK_NOPE - QK_ROPE
    qcos = jnp.concatenate([jnp.ones((L, QK_NOPE), F32), cos, cos, jnp.ones((L, pad), F32)], axis=1)
    qsin = jnp.concatenate([jnp.zeros((L, QK_NOPE), F32), sin, sin, jnp.zeros((L, pad), F32)], axis=1)
    kcs = jnp.concatenate([cos, cos, sin, sin, jnp.zeros((L, HEAD_PAD - 2 * QK_ROPE), F32)], axis=1)
    cc = jnp.arange(MLA_HEADS * HEAD_PAD)
    src = jnp.arange(HEAD_PAD)
    pk = ((cc[None, :] % HEAD_PAD) - QK_NOPE == src[:, None]) & (src[:, None] < QK_ROPE)
    grp = jnp.arange(HY_WIDTH) // (HY_WIDTH // HY_GROUPS)
    gsum = (grp[:, None] == grp[None, :]).astype(BF16)
    t = jnp.linspace(0.0, 1.0, L, dtype=F32)[:, None]
    omega = (two_pi / L) * jnp.arange(L, dtype=F32)
    bands = jnp.linspace(1e-4, HY_BANDS - 1, HY_BANDS, dtype=F32)
    phase = omega[:, None] * bands[None, :]
    z = jnp.concatenate([t, jnp.cos(phase), -jnp.sin(phase), jnp.zeros((L, HY_EMB_PAD - HY_EMB), F32)], axis=-1)
    ztab = jnp.concatenate([z, z[:1], z[:0:-1]], axis=0)
    return dict(f1d=f1d, f1f=f1f, gre=gre, gim=gim, m1=m1.astype(BF16), m2=m2.astype(BF16), qcos_t=qcos.T,
                qsin_t=qsin.T, kcs=kcs, pk=pk.astype(BF16), gsum=gsum, ztab=ztab)


def _rot_half_cols(w):
    half = QK_ROPE // 2
    return jnp.concatenate([-w[..., half:], w[..., :half]], axis=-1)


def _layer_weights(i, p):
    D = p["w_in"].shape[1]
    H = MLA_HEADS
    hw3 = 3 * HY_WIDTH
    w_in = p["w_in"][i]
    kpe = w_in[:, hw3 + Q_RANK + KV_RANK:]
    win = jnp.concatenate([w_in, _rot_half_cols(kpe),
                           jnp.zeros((D, HEAD_PAD - 2 * QK_ROPE), F32)], axis=1).astype(BF16)
    dq = QK_NOPE + QK_ROPE
    wq = p["mla_w_uq"][i].reshape(Q_RANK, H, dq)
    zq = jnp.zeros((Q_RANK, H, HEAD_PAD - dq), F32)
    wqa = jnp.concatenate([wq, zq], axis=2).reshape(Q_RANK, H * HEAD_PAD)
    wqb = jnp.concatenate([jnp.zeros((Q_RANK, H, QK_NOPE), F32), _rot_half_cols(wq[..., QK_NOPE:]), zq],
                          axis=2).reshape(Q_RANK, H * HEAD_PAD)
    wkv = p["mla_w_ukv"][i].reshape(KV_RANK, H, QK_NOPE + V_HEAD)
    wk = jnp.concatenate([wkv[..., :QK_NOPE], jnp.zeros((KV_RANK, H, HEAD_PAD - QK_NOPE), F32)], axis=2)
    wv = jnp.concatenate([wkv[..., QK_NOPE:], jnp.zeros((KV_RANK, H, V_PAD - V_HEAD), F32)], axis=2)
    row = lambda a: a.reshape(1, -1)
    return dict(
        win=win, g_pre=row(p["norm_mix_pre"][i]), conv_w=p["hy_conv_w"][i], conv_b=row(p["hy_conv_b"][i]),
        q_g=row(p["mla_q_norm"][i]), wqat=wqa.T.astype(BF16), wqbt=wqb.T.astype(BF16),
        kv_g=row(p["mla_kv_norm"][i]), wk=wk.reshape(KV_RANK, H * HEAD_PAD).astype(BF16),
        wvt=wv.reshape(KV_RANK, H * V_PAD).T.astype(BF16),
        g_hy=row(p["grp_norm_hy"][i]), g_attn=row(p["grp_norm_attn"][i]), wo=p["w_out"][i].astype(BF16),
        g_post=row(p["norm_mix_post"][i]), g_mpre=row(p["norm_mlp_pre"][i]), g_mpost=row(p["norm_mlp_post"][i]),
        wup=p["w_mlp_up"][i].astype(BF16), wdn=p["w_mlp_down"][i].astype(BF16),
    )


def _filter_weights(p):
    depth = p["hy_ffn_w1"].shape[0]
    oc = HY_ORDER * HY_WIDTH
    w1 = jnp.pad(p["hy_ffn_w1"], ((0, 0), (0, HY_EMB_PAD - HY_EMB), (0, 0)))
    w3 = p["hy_ffn_w3"].reshape(depth, HY_FFN, HY_ORDER, 2, HY_WIDTH).transpose(0, 3, 1, 2, 4)
    dec = p["hy_decay"].transpose(0, 2, 1, 3).reshape(depth, 2, 1, oc)
    return dict(w1=w1, b1=p["hy_ffn_b1"][:, None, :], sf=p["hy_sin_freq"], w2=p["hy_ffn_w2"],
                b2=p["hy_ffn_b2"][:, None, :], w3=w3.reshape(depth, 2, HY_FFN, oc), dec=dec)


def kernel(x_prompt, x_sample, w_in, hy_conv_w, hy_conv_b, hy_ffn_w1, hy_ffn_b1, hy_ffn_w2, hy_ffn_b2,
           hy_ffn_w3, hy_sin_freq, hy_decay, hy_bias, mla_q_norm, mla_w_uq, mla_kv_norm, mla_w_ukv,
           grp_norm_hy, grp_norm_attn, w_out, norm_mix_pre, norm_mix_post, norm_mlp_pre, norm_mlp_post,
           w_mlp_up, w_mlp_down):
    p = dict(w_in=w_in, hy_conv_w=hy_conv_w, hy_conv_b=hy_conv_b, hy_ffn_w1=hy_ffn_w1, hy_ffn_b1=hy_ffn_b1,
             hy_ffn_w2=hy_ffn_w2, hy_ffn_b2=hy_ffn_b2, hy_ffn_w3=hy_ffn_w3, hy_sin_freq=hy_sin_freq,
             hy_decay=hy_decay, hy_bias=hy_bias, mla_q_norm=mla_q_norm, mla_w_uq=mla_w_uq,
             mla_kv_norm=mla_kv_norm, mla_w_ukv=mla_w_ukv, grp_norm_hy=grp_norm_hy,
             grp_norm_attn=grp_norm_attn, w_out=w_out, norm_mix_pre=norm_mix_pre, norm_mix_post=norm_mix_post,
             norm_mlp_pre=norm_mlp_pre, norm_mlp_post=norm_mlp_post, w_mlp_up=w_mlp_up, w_mlp_down=w_mlp_down)
    bp, L, D = x_prompt.shape
    bs, Ls, _ = x_sample.shape
    assert L == Ls and L % (DFT_N2 * SUBLANES) == 0
    nb = bp + bs
    depth = w_in.shape[0]

    tabs = _tables(L)
    kc = _filters(tabs["ztab"], _filter_weights(p), L)
    kf = _filter_spectrum(_dft1(kc, tabs["f1f"]), tabs["m1"])

    x = jnp.concatenate([x_prompt.reshape(bp * L, D), x_sample.reshape(bs * L, D)], axis=0)
    for i in range(depth):
        lw = _layer_weights(i, p)
        v, x1, x2, qt, k, vt, stats = _inproj(x, lw, tabs, nb, L)
        z = _long_conv_gate(v, x1, kf, hy_bias[i, 0], tabs, i, 0)
        z = _long_conv_gate(z, x2, kf, hy_bias[i, 1], tabs, i, 1)
        an = _attention(_fast_flags(stats), qt, k, vt, lw["g_attn"], nb, L).reshape(nb * L, ATTN_WIDTH)
        x = _mix_mlp(x, z, an, lw, tabs, L)
    return (x[:bp * L].reshape(bp, L, D), x[bp * L:].reshape(bs, L, D))
```

```python
import functools
import math

import jax
import jax.numpy as jnp
from jax import lax
from jax.experimental import pallas as pl
from jax.experimental.pallas import tpu as pltpu

F32 = jnp.float32
BF16 = jnp.bfloat16

EPS = 1e-6
HY_WIDTH = 512
HY_GROUPS = 8
HY_ORDER = 2
HY_BANDS = 16
HY_EMB = 2 * HY_BANDS + 1
HY_EMB_PAD = 40
HY_FFN = 64
MLA_HEADS = 8
QK_NOPE = 64
QK_ROPE = 32
V_HEAD = 64
Q_RANK = 256
KV_RANK = 128
ROPE_BASE = 10000.0
HEAD_PAD = 128
V_PAD = 80
ATTN_WIDTH = MLA_HEADS * V_HEAD

LANES = 128
SUBLANES = 8
HY_CB = HY_WIDTH // LANES
DFT_N2 = 128
DFT_G = DFT_N2 // SUBLANES
MID_CB = 2
HALO = 16
VMEM_LIMIT = 56 * 1024 * 1024
ATTN_UNROLL = 32
STAT_ROWS = 24

FAST_S_MAX = 64.0
FAST_V_MAX = 2.0 ** 30


def _cparams(sem):
    return pltpu.CompilerParams(dimension_semantics=sem, vmem_limit_bytes=VMEM_LIMIT)


def _dot(a, b):
    return jnp.dot(a, b, preferred_element_type=F32)


def _dot_nt(a, b):
    return lax.dot_general(a, b, (((1,), (1,)), ((), ())), preferred_element_type=F32)


def _rms(x, g):
    return x * lax.rsqrt(jnp.mean(x * x, axis=-1, keepdims=True) + EPS) * g


def _store_grouped(out_ref, val, lane0, ncb):
    for j in range(ncb):
        for nl in range(val.shape[0] // DFT_N2):
            for g in range(DFT_G):
                r0 = nl * DFT_N2 + g * SUBLANES
                out_ref[j, g, nl * SUBLANES:(nl + 1) * SUBLANES, :] = (
                    val[r0:r0 + SUBLANES, lane0 + j * LANES:lane0 + (j + 1) * LANES])


def _fold_lanes(row):
    parts = [row[:, j * LANES:(j + 1) * LANES] for j in range(row.shape[1] // LANES)]
    return functools.reduce(jnp.maximum, parts)


def _inproj_kernel(xp_ref, x_ref, xn_ref, gpre_ref, win_ref, cw_ref, cb_ref, qg_ref, wqat_ref,
                   wqbt_ref, kvg_ref, wk_ref, pk_ref, wvt_ref, qcos_ref, qsin_ref, kcs_ref,
                   v_out, x1_out, x2_out, qt_out, k_out, vt_out, st_out, h_scr, pe_scr,
                   *, tm, tiles_per_seq, scale):
    i = pl.program_id(0)
    t_idx = i % tiles_per_seq
    g = gpre_ref[...]
    h_scr[0:HALO, :] = _rms(xp_ref[...], g).astype(BF16)
    h_scr[HALO:HALO + tm, :] = _rms(x_ref[...], g).astype(BF16)
    h_scr[HALO + tm:2 * HALO + tm, :] = _rms(xn_ref[...], g).astype(BF16)
    proj = _dot(h_scr[...], win_ref[...])

    hw3 = 3 * HY_WIDTH
    pe_scr[...] = proj[:, :hw3]
    row = lax.broadcasted_iota(jnp.int32, (tm, 1), 0)
    prev = pe_scr[HALO - 1:HALO - 1 + tm, :]
    cur = pe_scr[HALO:HALO + tm, :]
    nxt = pe_scr[HALO + 1:HALO + 1 + tm, :]
    prev = jnp.where(jnp.logical_and(row == 0, t_idx == 0), 0.0, prev)
    nxt = jnp.where(jnp.logical_and(row == tm - 1, t_idx == tiles_per_seq - 1), 0.0, nxt)
    u = prev * cw_ref[0:1, :] + cur * cw_ref[1:2, :] + nxt * cw_ref[2:3, :] + cb_ref[...]
    for o, out in enumerate((v_out, x1_out, x2_out)):
        _store_grouped(out, u, o * HY_WIDTH, HY_CB)

    core = proj[HALO:HALO + tm, :]
    cq = core[:, hw3:hw3 + Q_RANK]
    cqn = _rms(cq, qg_ref[...]).astype(BF16)
    qat = _dot_nt(wqat_ref[...], cqn)
    qbt = _dot_nt(wqbt_ref[...], cqn)
    qcos = qcos_ref[...]
    qsin = qsin_ref[...]
    for h in range(MLA_HEADS):
        sl = slice(h * HEAD_PAD, (h + 1) * HEAD_PAD)
        qh = ((qat[sl, :] * qcos + qbt[sl, :] * qsin) * scale).astype(BF16)
        qt_out[h] = qh
        qf = qh.astype(F32)
        st_out[h:h + 1, :] = _fold_lanes(jnp.sum(qf * qf, axis=0, keepdims=True))

    ckv = core[:, hw3 + Q_RANK:hw3 + Q_RANK + KV_RANK]
    ckvn = _rms(ckv, kvg_ref[...]).astype(BF16)
    kp = core[:, hw3 + Q_RANK + KV_RANK:] * kcs_ref[...]
    kp = kp + pltpu.roll(kp, HEAD_PAD - QK_ROPE, axis=1)
    kk = _dot(ckvn, wk_ref[...]) + _dot(kp.astype(BF16), pk_ref[...])
    for h in range(MLA_HEADS):
        kh = kk[:, h * HEAD_PAD:(h + 1) * HEAD_PAD].astype(BF16)
        k_out[h] = kh
        kf = kh.astype(F32)
        kn = jnp.max(jnp.sum(kf * kf, axis=1, keepdims=True), axis=0, keepdims=True)
        st_out[MLA_HEADS + h:MLA_HEADS + h + 1, :] = jnp.broadcast_to(kn, (1, LANES))
    vt = _dot_nt(wvt_ref[...], ckvn)
    st_out[2 * MLA_HEADS:2 * MLA_HEADS + 1, :] = _fold_lanes(jnp.max(jnp.abs(vt), axis=0, keepdims=True))
    st_out[2 * MLA_HEADS + 1:, :] = jnp.zeros((STAT_ROWS - 2 * MLA_HEADS - 1, LANES), F32)
    frow = lax.broadcasted_iota(jnp.int32, (MLA_HEADS * V_PAD, 1), 0)
    vt = vt + jnp.where(frow % V_PAD == V_HEAD, 1.0, 0.0)
    for h in range(MLA_HEADS):
        vt_out[h] = vt[h * V_PAD:(h + 1) * V_PAD, :].astype(BF16)


def _inproj(x, lw, tabs, nb, L):
    M, D = x.shape
    tm = min(512, L)
    tps = L // tm
    nt = M // tm
    hb = tm // HALO
    nhb = M // HALO
    H = MLA_HEADS
    wcols = lw["win"].shape[1]
    const = lambda *shape: pl.BlockSpec(shape, lambda i: (0,) * len(shape))
    tab_t = pl.BlockSpec((HEAD_PAD, tm), lambda i: (0, i % tps))
    in_specs = [
        pl.BlockSpec((HALO, D), lambda i: (jnp.maximum(i * hb - 1, 0), 0)),
        pl.BlockSpec((tm, D), lambda i: (i, 0)),
        pl.BlockSpec((HALO, D), lambda i: (jnp.minimum((i + 1) * hb, nhb - 1), 0)),
        const(1, D), const(D, wcols), const(3, 3 * HY_WIDTH), const(1, 3 * HY_WIDTH),
        const(1, Q_RANK), const(H * HEAD_PAD, Q_RANK), const(H * HEAD_PAD, Q_RANK),
        const(1, KV_RANK), const(KV_RANK, H * HEAD_PAD), const(HEAD_PAD, H * HEAD_PAD),
        const(H * V_PAD, KV_RANK),
        tab_t, tab_t, pl.BlockSpec((tm, HEAD_PAD), lambda i: (i % tps, 0)),
    ]
    hy_spec = pl.BlockSpec((None, HY_CB, DFT_G, tm // DFT_G, LANES), lambda i: (i // tps, 0, 0, i % tps, 0))
    out_specs = [
        hy_spec, hy_spec, hy_spec,
        pl.BlockSpec((None, H, HEAD_PAD, tm), lambda i: (i // tps, 0, 0, i % tps)),
        pl.BlockSpec((None, H, tm, HEAD_PAD), lambda i: (i // tps, 0, i % tps, 0)),
        pl.BlockSpec((None, H, None, V_PAD, tm), lambda i: (i // tps, 0, i % tps, 0, 0)),
        pl.BlockSpec((None, None, STAT_ROWS, LANES), lambda i: (i // tps, i % tps, 0, 0)),
    ]
    hy_shape = jax.ShapeDtypeStruct((nb, HY_CB, DFT_G, L // DFT_G, LANES), F32)
    out_shape = [
        hy_shape, hy_shape, hy_shape,
        jax.ShapeDtypeStruct((nb, H, HEAD_PAD, L), BF16),
        jax.ShapeDtypeStruct((nb, H, L, HEAD_PAD), BF16),
        jax.ShapeDtypeStruct((nb, H, tps, V_PAD, tm), BF16),
        jax.ShapeDtypeStruct((nb, tps, STAT_ROWS, LANES), F32),
    ]
    scale = float((QK_NOPE + QK_ROPE) ** -0.5 * math.log2(math.e))
    return pl.pallas_call(
        functools.partial(_inproj_kernel, tm=tm, tiles_per_seq=tps, scale=scale),
        grid=(nt,), in_specs=in_specs, out_specs=out_specs, out_shape=out_shape,
        scratch_shapes=[pltpu.VMEM((tm + 2 * HALO, D), BF16),
                        pltpu.VMEM((tm + 2 * HALO, 3 * HY_WIDTH), F32)],
        compiler_params=_cparams(("parallel",)), name="inproj",
    )(x, x, x, lw["g_pre"], lw["win"], lw["conv_w"], lw["conv_b"], lw["q_g"], lw["wqat"], lw["wqbt"],
      lw["kv_g"], lw["wk"], tabs["pk"], lw["wvt"], tabs["qcos_t"], tabs["qsin_t"], tabs["kcs"])


def _filter_kernel(z_ref, w1_ref, b1_ref, sf_ref, w2_ref, b2_ref, w3_ref, dec_ref, out_ref, *, rb, L):
    hi = lax.Precision.HIGHEST
    r = pl.program_id(1)
    z = z_ref[...]
    h = jnp.sin(sf_ref[0:1, :] * (jnp.dot(z, w1_ref[...], precision=hi, preferred_element_type=F32) + b1_ref[...]))
    h = jnp.sin(sf_ref[1:2, :] * (jnp.dot(h, w2_ref[...], precision=hi, preferred_element_type=F32) + b2_ref[...]))
    k = jnp.dot(h, w3_ref[...], precision=hi, preferred_element_type=F32)
    k = k * jnp.exp(-z[:, 0:1] * jnp.abs(dec_ref[...]))
    row = r * rb + lax.broadcasted_iota(jnp.int32, (rb, 1), 0)
    k = jnp.where(row == L, 0.0, k)
    _store_grouped(out_ref, k, 0, out_ref.shape[0])


def _filters(ztab, fw, L):
    depth = fw["w1"].shape[0]
    rows = 2 * L
    rb = min(512, L)
    nblk = rows // rb
    half = nblk // 2
    oc = HY_ORDER * HY_WIDTH
    lay = lambda *shape: pl.BlockSpec((None,) + shape, lambda l, r: (l,) + (0,) * len(shape))
    in_specs = [
        pl.BlockSpec((rb, HY_EMB_PAD), lambda l, r: (r, 0)),
        lay(HY_EMB_PAD, HY_FFN), lay(1, HY_FFN), lay(2, HY_FFN), lay(HY_FFN, HY_FFN), lay(1, HY_FFN),
        pl.BlockSpec((None, None, HY_FFN, oc), lambda l, r: (l, r // half, 0, 0)),
        pl.BlockSpec((None, None, 1, oc), lambda l, r: (l, r // half, 0, 0)),
    ]
    return pl.pallas_call(
        functools.partial(_filter_kernel, rb=rb, L=L),
        grid=(depth, nblk), in_specs=in_specs,
        out_specs=pl.BlockSpec((None, oc // LANES, DFT_G, rb // DFT_G, LANES), lambda l, r: (l, 0, 0, r, 0)),
        out_shape=jax.ShapeDtypeStruct((depth, oc // LANES, DFT_G, rows // DFT_G, LANES), F32),
        compiler_params=_cparams(("parallel", "parallel")), name="hyena_filter",
    )(ztab, fw["w1"], fw["b1"], fw["sf"], fw["w2"], fw["b2"], fw["w3"], fw["dec"])


def _rows_of(ref, p, n, lead=()):
    return ref[lead + (pl.ds(p, n, stride=SUBLANES), slice(None))]


def _dft1_block(f_ref, xs, out_ref):
    kp = f_ref.shape[0] // 2
    res = _dot(f_ref[...], jnp.concatenate([x.astype(BF16) for x in xs], axis=1))
    for ri in range(2):
        for kg in range(kp // SUBLANES):
            r0 = ri * kp + kg * SUBLANES
            for p in range(SUBLANES):
                out_ref[ri, kg, p * SUBLANES:(p + 1) * SUBLANES, :] = (
                    res[r0:r0 + SUBLANES, p * LANES:(p + 1) * LANES])


def _dft1_kernel(f_ref, x_ref, out_ref):
    R = f_ref.shape[1]
    _dft1_block(f_ref, [_rows_of(x_ref, p, R) for p in range(SUBLANES)], out_ref)


def _dft1_spec(kg):
    return pl.BlockSpec((None, None, 2, kg, SUBLANES * SUBLANES, LANES), lambda b, c, g: (b, c, 0, 0, g, 0))


def _dft1(x, f1):
    nb, ncb, _, rows, _ = x.shape
    kp = f1.shape[0] // 2
    kg = kp // SUBLANES
    return pl.pallas_call(
        _dft1_kernel, grid=(nb, ncb, DFT_G),
        in_specs=[pl.BlockSpec(f1.shape, lambda b, c, g: (0, 0)),
                  pl.BlockSpec((None, None, None, rows, LANES), lambda b, c, g: (b, c, g, 0, 0))],
        out_specs=_dft1_spec(kg),
        out_shape=jax.ShapeDtypeStruct((nb, ncb, 2, kg, DFT_N2 * SUBLANES, LANES), F32),
        compiler_params=_cparams(("parallel", "parallel", "parallel")), name="dft_outer",
    )(f1, x)


def _inner_fwd(m1_ref, a_ref, kk):
    ar = _rows_of(a_ref, kk, DFT_N2, (0,)).astype(BF16)
    ai = _rows_of(a_ref, kk, DFT_N2, (1,)).astype(BF16)
    return _dot(m1_ref[kk], jnp.concatenate([ar, ai], axis=0))


def _spec_kernel(m1_ref, a_ref, out_ref):
    for kk in range(SUBLANES):
        x = _inner_fwd(m1_ref, a_ref, kk)
        out_ref[kk, 0] = x[:DFT_N2]
        out_ref[kk, 1] = x[DFT_N2:]


def _filter_spectrum(a, m1):
    depth, ncb, _, kg, rows, _ = a.shape
    n2 = DFT_N2
    return pl.pallas_call(
        _spec_kernel, grid=(kg, depth, ncb),
        in_specs=[pl.BlockSpec((SUBLANES, 2 * n2, 2 * n2), lambda k, l, c: (k, 0, 0)),
                  pl.BlockSpec((None, None, 2, None, rows, LANES), lambda k, l, c: (l, c, 0, k, 0, 0))],
        out_specs=pl.BlockSpec((None, None, SUBLANES, 2, n2, LANES), lambda k, l, c: (l, c, k, 0, 0, 0)),
        out_shape=jax.ShapeDtypeStruct((depth, ncb, kg * SUBLANES, 2, n2, LANES), F32),
        compiler_params=_cparams(("parallel", "parallel", "parallel")), name="filter_spectrum",
    )(m1, a)


def _mid_kernel(m1_ref, m2_ref, kf_ref, a_ref, out_ref):
    ncb = a_ref.shape[0]
    lanes = lambda parts: jnp.concatenate(parts, axis=1)
    for kk in range(SUBLANES):
        ar = lanes([_rows_of(a_ref, kk, DFT_N2, (c, 0)).astype(BF16) for c in range(ncb)])
        ai = lanes([_rows_of(a_ref, kk, DFT_N2, (c, 1)).astype(BF16) for c in range(ncb)])
        x = _dot(m1_ref[kk], jnp.concatenate([ar, ai], axis=0))
        xr, xi = x[:DFT_N2], x[DFT_N2:]
        kr = lanes([kf_ref[c, kk, 0] for c in range(ncb)])
        ki = lanes([kf_ref[c, kk, 1] for c in range(ncb)])
        yr = (xr * kr - xi * ki).astype(BF16)
        yi = (xr * ki + xi * kr).astype(BF16)
        y = _dot(m2_ref[kk], jnp.concatenate([yr, yi], axis=0))
        for c in range(ncb):
            for ri in range(2):
                for g in range(DFT_G):
                    r0 = ri * DFT_N2 + g * SUBLANES
                    out_ref[c, ri, g, kk * SUBLANES:(kk + 1) * SUBLANES, :] = (
                        y[r0:r0 + SUBLANES, c * LANES:(c + 1) * LANES])


def _conv_mid(a, kf, m1, m2, layer, order):
    nb, ncb, _, kg, rows, _ = a.shape
    n2 = DFT_N2
    mspec = pl.BlockSpec((SUBLANES, 2 * n2, 2 * n2), lambda k, c, b: (k, 0, 0))
    return pl.pallas_call(
        _mid_kernel, grid=(kg, ncb // MID_CB, nb),
        in_specs=[mspec, mspec,
                  pl.BlockSpec((None, MID_CB, SUBLANES, 2, n2, LANES),
                               lambda k, c, b: (layer, order * (ncb // MID_CB) + c, k, 0, 0, 0)),
                  pl.BlockSpec((None, MID_CB, 2, None, rows, LANES), lambda k, c, b: (b, c, 0, k, 0, 0))],
        out_specs=pl.BlockSpec((None, MID_CB, 2, DFT_G, SUBLANES * SUBLANES, LANES),
                               lambda k, c, b: (b, c, 0, 0, k, 0)),
        out_shape=jax.ShapeDtypeStruct((nb, ncb, 2, DFT_G, kg * SUBLANES * SUBLANES, LANES), F32),
        compiler_params=_cparams(("parallel", "parallel", "arbitrary")), name="conv_mid",
    )(m1, m2, kf, a)


def _gate_kernel(gre_ref, gim_ref, f_ref, b_ref, z_ref, gate_ref, bias_ref, out_ref, *next_ref):
    R, kp = gre_ref.shape
    bre = jnp.concatenate([_rows_of(b_ref, p, kp, (0,)).astype(BF16) for p in range(SUBLANES)], axis=1)
    bim = jnp.concatenate([_rows_of(b_ref, p, kp, (1,)).astype(BF16) for p in range(SUBLANES)], axis=1)
    y = _dot(gre_ref[...], bre) + _dot(gim_ref[...], bim)
    bias = bias_ref[...]
    vals = []
    for p in range(SUBLANES):
        yp = y[:, p * LANES:(p + 1) * LANES]
        vals.append(_rows_of(gate_ref, p, R) * (yp + _rows_of(z_ref, p, R) * bias))
        out_ref[pl.ds(p, R, stride=SUBLANES), :] = vals[p]
    if next_ref:
        _dft1_block(f_ref, vals, next_ref[0])


def _conv_out(bsp, tabs, z, gate, bias, with_next):
    nb, ncb, _, _, krows, _ = bsp.shape
    rows = z.shape[3]
    gre, gim, f1 = tabs["gre"], tabs["gim"], tabs["f1d"]
    kg = f1.shape[0] // 2 // SUBLANES
    full = lambda t: pl.BlockSpec(t.shape, lambda b, c, g: (0, 0))
    tile = pl.BlockSpec((None, None, None, rows, LANES), lambda b, c, g: (b, c, g, 0, 0))
    out_specs, out_shape = [tile], [jax.ShapeDtypeStruct(z.shape, F32)]
    if with_next:
        out_specs.append(_dft1_spec(kg))
        out_shape.append(jax.ShapeDtypeStruct((nb, ncb, 2, kg, DFT_N2 * SUBLANES, LANES), F32))
    return pl.pallas_call(
        _gate_kernel, grid=(nb, ncb, DFT_G),
        in_specs=[full(gre), full(gim), full(f1),
                  pl.BlockSpec((None, None, 2, None, krows, LANES), lambda b, c, g: (b, c, 0, g, 0, 0)),
                  tile, tile, pl.BlockSpec((None, 1, LANES), lambda b, c, g: (c, 0, 0))],
        out_specs=out_specs, out_shape=out_shape,
        compiler_params=_cparams(("parallel", "parallel", "parallel")), name="conv_gate",
    )(gre, gim, f1, bsp, z, gate, bias)


def _hyena_convs(v, x1, x2, kf, bias, tabs, layer):
    b0, b1 = bias[0].reshape(HY_CB, 1, LANES), bias[1].reshape(HY_CB, 1, LANES)
    bsp = _conv_mid(_dft1(v, tabs["f1d"]), kf, tabs["m1"], tabs["m2"], layer, 0)
    z1, a2 = _conv_out(bsp, tabs, v, x1, b0, True)
    bsp = _conv_mid(a2, kf, tabs["m1"], tabs["m2"], layer, 1)
    return _conv_out(bsp, tabs, z1, x2, b1, False)[0]


def _attn_kernel(flag_ref, qt_ref, k_ref, vt_ref, g_ref, o_ref, acc_scr, m_scr, s_scr, *, nk, tk, unroll):
    pair = pl.program_id(0) * pl.num_programs(1) + pl.program_id(1)
    fast = jnp.logical_and(flag_ref[2 * pair] == 1, flag_ref[2 * pair + 1] == 1)
    acc_scr[...] = jnp.zeros(acc_scr.shape, F32)

    @pl.when(fast)
    def _():
        last = 2 * nk - 1

        def scores(cc):
            hd = cc // nk
            c = cc - hd * nk
            return _dot(k_ref[hd, pl.ds(pl.multiple_of(c * tk, tk), tk), :], qt_ref[hd])

        s_scr[0] = scores(0)

        def body(j, carry):
            hd = (j * unroll) // nk
            c0 = j * unroll - hd * nk
            pv = None
            for u in range(unroll):
                s_scr[(u + 1) % 2] = scores(jnp.minimum(j * unroll + u + 1, last))
                pt = jnp.exp2(s_scr[u % 2]).astype(BF16)
                d = _dot(vt_ref[hd, c0 + u], pt)
                pv = d if pv is None else pv + d
            acc_scr[hd] += pv
            return carry
        lax.fori_loop(0, 2 * nk // unroll, body, 0)

    @pl.when(jnp.logical_not(fast))
    def _():
        for hh in range(2):
            qt = qt_ref[hh]
            m_scr[...] = jnp.full(m_scr.shape, -jnp.inf, F32)

            def body(j, carry, hh=hh, qt=qt):
                kk = k_ref[hh, pl.ds(pl.multiple_of(j * tk, tk), tk), :]
                s = _dot(kk, qt)
                m_prev = m_scr[...]
                m_new = jnp.maximum(m_prev, jnp.max(s, axis=0, keepdims=True))
                pt = jnp.exp2(s - m_new).astype(BF16)
                acc_scr[hh] = jnp.exp2(m_prev - m_new) * acc_scr[hh] + _dot(vt_ref[hh, j], pt)
                m_scr[...] = m_new
                return carry
            lax.fori_loop(0, nk, body, 0)

    outs = []
    for hh in range(2):
        acc = acc_scr[hh]
        o = acc[:V_HEAD] / acc[V_HEAD:V_HEAD + 1]
        ms = jnp.mean(o * o, axis=0, keepdims=True)
        outs.append(o * lax.rsqrt(ms + EPS))
    ot = jnp.concatenate(outs, axis=0)
    o_ref[...] = (ot.T * g_ref[...]).astype(o_ref.dtype)


def _attention(flags, qt, k, vt, g_attn, nb, L):
    H = MLA_HEADS
    tq = min(512, L)
    nk, tk = vt.shape[2], vt.shape[4]
    grid_spec = pltpu.PrefetchScalarGridSpec(
        num_scalar_prefetch=1, grid=(nb, H // 2, L // tq),
        in_specs=[pl.BlockSpec((None, 2, HEAD_PAD, tq), lambda b, h, i, f: (b, h, 0, i)),
                  pl.BlockSpec((None, 2, L, HEAD_PAD), lambda b, h, i, f: (b, h, 0, 0)),
                  pl.BlockSpec((None, 2, nk, V_PAD, tk), lambda b, h, i, f: (b, h, 0, 0, 0)),
                  pl.BlockSpec((1, 2 * V_HEAD), lambda b, h, i, f: (0, h))],
        out_specs=pl.BlockSpec((None, tq, 2 * V_HEAD), lambda b, h, i, f: (b, i, h)),
        scratch_shapes=[pltpu.VMEM((2, V_PAD, tq), F32), pltpu.VMEM((1, tq), F32), pltpu.VMEM((2, tk, tq), F32)])
    unroll = math.gcd(nk, ATTN_UNROLL)
    assert unroll % 2 == 0
    return pl.pallas_call(
        functools.partial(_attn_kernel, nk=nk, tk=tk, unroll=unroll), grid_spec=grid_spec,
        out_shape=jax.ShapeDtypeStruct((nb, L, ATTN_WIDTH), BF16),
        compiler_params=_cparams(("parallel", "parallel", "arbitrary")), name="attention",
    )(flags, qt, k, vt, g_attn)


def _fast_flags(stats):
    H = MLA_HEADS
    qn = jnp.max(stats[:, :, 0:H, :], axis=(1, 3))
    kn = jnp.max(stats[:, :, H:2 * H, :], axis=(1, 3))
    vm = jnp.max(stats[:, :, 2 * H, :], axis=(1, 2))
    ok = jnp.logical_and(qn * kn <= FAST_S_MAX * FAST_S_MAX, (vm <= FAST_V_MAX)[:, None])
    return ok.astype(jnp.int32).reshape(-1)


def _mix_mlp_kernel(x_ref, zh_ref, an_ref, ghy_ref, gsum_ref, wo_ref, gpost_ref, gmpre_ref, wup_ref,
                    wdn_ref, gmpost_ref, out_ref):
    zh = jnp.concatenate(
        [jnp.concatenate([zh_ref[j, g, nl * SUBLANES:(nl + 1) * SUBLANES, :] for j in range(HY_CB)], axis=1)
         for nl in range(zh_ref.shape[2] // SUBLANES) for g in range(DFT_G)], axis=0)
    ms = _dot((zh * zh).astype(BF16), gsum_ref[...]) * (HY_GROUPS / HY_WIDTH)
    hn = (zh * lax.rsqrt(ms + EPS) * ghy_ref[...]).astype(BF16)
    mix = _dot(hn, wo_ref[0:HY_WIDTH, :]) + _dot(an_ref[...], wo_ref[HY_WIDTH:, :])
    x = x_ref[...] + _rms(mix, gpost_ref[...])
    h = _rms(x, gmpre_ref[...]).astype(BF16)
    up = jnp.maximum(_dot(h, wup_ref[...]), 0.0)
    m = _dot((up * up).astype(BF16), wdn_ref[...])
    out_ref[...] = x + _rms(m, gmpost_ref[...])


def _mix_mlp(x, zh, an, lw, tabs, L):
    M, D = x.shape
    tm = min(256, L)
    tps = L // tm
    dff = lw["wup"].shape[1]
    mw = HY_WIDTH + ATTN_WIDTH
    const = lambda *shape: pl.BlockSpec(shape, lambda i: (0,) * len(shape), pipeline_mode=pl.Buffered(1))
    rows = lambda w: pl.BlockSpec((tm, w), lambda i: (i, 0))
    return pl.pallas_call(
        _mix_mlp_kernel, grid=(M // tm,),
        in_specs=[rows(D), pl.BlockSpec((None, HY_CB, DFT_G, tm // DFT_G, LANES),
                                        lambda i: (i // tps, 0, 0, i % tps, 0)),
                  rows(ATTN_WIDTH), const(1, HY_WIDTH), const(HY_WIDTH, HY_WIDTH),
                  const(mw, D), const(1, D), const(1, D), const(D, dff), const(dff, D), const(1, D)],
        out_specs=rows(D), out_shape=jax.ShapeDtypeStruct((M, D), F32),
        compiler_params=_cparams(("parallel",)), name="mix_mlp",
    )(x, zh, an, lw["g_hy"], tabs["gsum"], lw["wo"], lw["g_post"], lw["g_mpre"], lw["wup"], lw["wdn"],
      lw["g_mpost"])


def _tables(L):
    n = 2 * L
    n2 = DFT_N2
    n1 = n // n2
    nh = n1 // 2
    kp = -(-(nh + 1) // SUBLANES) * SUBLANES
    two_pi = 2.0 * math.pi

    k1 = jnp.arange(kp, dtype=jnp.int32)
    valid = (k1 <= nh)
    def outer(ncols):
        nn = jnp.arange(ncols, dtype=jnp.int32)
        ang = ((k1[:, None] * nn[None, :]) % n1).astype(F32) * (two_pi / n1)
        c = jnp.where(valid[:, None], jnp.cos(ang), 0.0)
        s = jnp.where(valid[:, None], -jnp.sin(ang), 0.0)
        return jnp.concatenate([c, s], axis=0).astype(BF16)
    f1d = outer(nh)
    f1f = outer(n1)
    nn = jnp.arange(nh, dtype=jnp.int32)
    ang = ((nn[:, None] * k1[None, :]) % n1).astype(F32) * (two_pi / n1)
    wgt = jnp.where(valid, jnp.where((k1 == 0) | (k1 == nh), 1.0, 2.0), 0.0) / n
    gre = (jnp.cos(ang) * wgt[None, :]).astype(BF16)
    gim = (-jnp.sin(ang) * wgt[None, :]).astype(BF16)
    a2 = jnp.arange(n2, dtype=jnp.int32)
    idx = (a2[None, :, None] * a2[None, None, :] * n1 + a2[None, None, :] * k1[:, None, None]) % n
    ph = idx.astype(F32) * (two_pi / n)
    gr, gi = jnp.cos(ph), -jnp.sin(ph)
    m1 = jnp.concatenate([jnp.concatenate([gr, -gi], axis=2), jnp.concatenate([gi, gr], axis=2)], axis=1)
    m2 = jnp.swapaxes(m1, 1, 2)
    inv = 1.0 / (ROPE_BASE ** (jnp.arange(0, QK_ROPE, 2, dtype=F32) / QK_ROPE))
    ang = jnp.arange(L, dtype=F32)[:, None] * inv[None, :]
    cos, sin = jnp.cos(ang), jnp.sin(ang)
    pad = HEAD_PAD - QK_NOPE - QK_ROPE
    qcos = jnp.concatenate([jnp.ones((L, QK_NOPE), F32), cos, cos, jnp.ones((L, pad), F32)], axis=1)
    qsin = jnp.concatenate([jnp.zeros((L, QK_NOPE), F32), sin, sin, jnp.zeros((L, pad), F32)], axis=1)
    kcs = jnp.concatenate([cos, cos, sin, sin, jnp.zeros((L, HEAD_PAD - 2 * QK_ROPE), F32)], axis=1)
    cc = jnp.arange(MLA_HEADS * HEAD_PAD)
    src = jnp.arange(HEAD_PAD)
    pk = ((cc[None, :] % HEAD_PAD) - QK_NOPE == src[:, None]) & (src[:, None] < QK_ROPE)
    grp = jnp.arange(HY_WIDTH) // (HY_WIDTH // HY_GROUPS)
    gsum = (grp[:, None] == grp[None, :]).astype(BF16)
    t = jnp.linspace(0.0, 1.0, L, dtype=F32)[:, None]
    omega = (two_pi / L) * jnp.arange(L, dtype=F32)
    bands = jnp.linspace(1e-4, HY_BANDS - 1, HY_BANDS, dtype=F32)
    phase = omega[:, None] * bands[None, :]
    z = jnp.concatenate([t, jnp.cos(phase), -jnp.sin(phase), jnp.zeros((L, HY_EMB_PAD - HY_EMB), F32)], axis=-1)
    ztab = jnp.concatenate([z, z[:1], z[:0:-1]], axis=0)
    return dict(f1d=f1d, f1f=f1f, gre=gre, gim=gim, m1=m1.astype(BF16), m2=m2.astype(BF16), qcos_t=qcos.T,
                qsin_t=qsin.T, kcs=kcs, pk=pk.astype(BF16), gsum=gsum, ztab=ztab)


def _rot_half_cols(w):
    half = QK_ROPE // 2
    return jnp.concatenate([-w[..., half:], w[..., :half]], axis=-1)


def _layer_weights(i, p):
    D = p["w_in"].shape[1]
    H = MLA_HEADS
    hw3 = 3 * HY_WIDTH
    w_in = p["w_in"][i]
    kpe = w_in[:, hw3 + Q_RANK + KV_RANK:]
    win = jnp.concatenate([w_in, _rot_half_cols(kpe),
                           jnp.zeros((D, HEAD_PAD - 2 * QK_ROPE), F32)], axis=1).astype(BF16)
    dq = QK_NOPE + QK_ROPE
    wq = p["mla_w_uq"][i].reshape(Q_RANK, H, dq)
    zq = jnp.zeros((Q_RANK, H, HEAD_PAD - dq), F32)
    wqa = jnp.concatenate([wq, zq], axis=2).reshape(Q_RANK, H * HEAD_PAD)
    wqb = jnp.concatenate([jnp.zeros((Q_RANK, H, QK_NOPE), F32), _rot_half_cols(wq[..., QK_NOPE:]), zq],
                          axis=2).reshape(Q_RANK, H * HEAD_PAD)
    wkv = p["mla_w_ukv"][i].reshape(KV_RANK, H, QK_NOPE + V_HEAD)
    wk = jnp.concatenate([wkv[..., :QK_NOPE], jnp.zeros((KV_RANK, H, HEAD_PAD - QK_NOPE), F32)], axis=2)
    wv = jnp.concatenate([wkv[..., QK_NOPE:], jnp.zeros((KV_RANK, H, V_PAD - V_HEAD), F32)], axis=2)
    row = lambda a: a.reshape(1, -1)
    return dict(
        win=win, g_pre=row(p["norm_mix_pre"][i]), conv_w=p["hy_conv_w"][i], conv_b=row(p["hy_conv_b"][i]),
        q_g=row(p["mla_q_norm"][i]), wqat=wqa.T.astype(BF16), wqbt=wqb.T.astype(BF16),
        kv_g=row(p["mla_kv_norm"][i]), wk=wk.reshape(KV_RANK, H * HEAD_PAD).astype(BF16),
        wvt=wv.reshape(KV_RANK, H * V_PAD).T.astype(BF16),
        g_hy=row(p["grp_norm_hy"][i]), g_attn=row(p["grp_norm_attn"][i]), wo=p["w_out"][i].astype(BF16),
        g_post=row(p["norm_mix_post"][i]), g_mpre=row(p["norm_mlp_pre"][i]), g_mpost=row(p["norm_mlp_post"][i]),
        wup=p["w_mlp_up"][i].astype(BF16), wdn=p["w_mlp_down"][i].astype(BF16),
    )


def _filter_weights(p):
    depth = p["hy_ffn_w1"].shape[0]
    oc = HY_ORDER * HY_WIDTH
    w1 = jnp.pad(p["hy_ffn_w1"], ((0, 0), (0, HY_EMB_PAD - HY_EMB), (0, 0)))
    w3 = p["hy_ffn_w3"].reshape(depth, HY_FFN, HY_ORDER, 2, HY_WIDTH).transpose(0, 3, 1, 2, 4)
    dec = p["hy_decay"].transpose(0, 2, 1, 3).reshape(depth, 2, 1, oc)
    return dict(w1=w1, b1=p["hy_ffn_b1"][:, None, :], sf=p["hy_sin_freq"], w2=p["hy_ffn_w2"],
                b2=p["hy_ffn_b2"][:, None, :], w3=w3.reshape(depth, 2, HY_FFN, oc), dec=dec)


def kernel(x_prompt, x_sample, w_in, hy_conv_w, hy_conv_b, hy_ffn_w1, hy_ffn_b1, hy_ffn_w2, hy_ffn_b2,
           hy_ffn_w3, hy_sin_freq, hy_decay, hy_bias, mla_q_norm, mla_w_uq, mla_kv_norm, mla_w_ukv,
           grp_norm_hy, grp_norm_attn, w_out, norm_mix_pre, norm_mix_post, norm_mlp_pre, norm_mlp_post,
           w_mlp_up, w_mlp_down):
    p = dict(w_in=w_in, hy_conv_w=hy_conv_w, hy_conv_b=hy_conv_b, hy_ffn_w1=hy_ffn_w1, hy_ffn_b1=hy_ffn_b1,
             hy_ffn_w2=hy_ffn_w2, hy_ffn_b2=hy_ffn_b2, hy_ffn_w3=hy_ffn_w3, hy_sin_freq=hy_sin_freq,
             hy_decay=hy_decay, hy_bias=hy_bias, mla_q_norm=mla_q_norm, mla_w_uq=mla_w_uq,
             mla_kv_norm=mla_kv_norm, mla_w_ukv=mla_w_ukv, grp_norm_hy=grp_norm_hy,
             grp_norm_attn=grp_norm_attn, w_out=w_out, norm_mix_pre=norm_mix_pre, norm_mix_post=norm_mix_post,
             norm_mlp_pre=norm_mlp_pre, norm_mlp_post=norm_mlp_post, w_mlp_up=w_mlp_up, w_mlp_down=w_mlp_down)
    bp, L, D = x_prompt.shape
    bs, Ls, _ = x_sample.shape
    assert L == Ls and L % (DFT_N2 * SUBLANES) == 0
    nb = bp + bs
    depth = w_in.shape[0]

    tabs = _tables(L)
    kc = _filters(tabs["ztab"], _filter_weights(p), L)
    kf = _filter_spectrum(_dft1(kc, tabs["f1f"]), tabs["m1"])

    x = jnp.concatenate([x_prompt.reshape(bp * L, D), x_sample.reshape(bs * L, D)], axis=0)
    for i in range(depth):
        lw = _layer_weights(i, p)
        v, x1, x2, qt, k, vt, stats = _inproj(x, lw, tabs, nb, L)
        z = _hyena_convs(v, x1, x2, kf, hy_bias[i], tabs, i)
        an = _attention(_fast_flags(stats), qt, k, vt, lw["g_attn"], nb, L).reshape(nb * L, ATTN_WIDTH)
        x = _mix_mlp(x, z, an, lw, tabs, L)
    return (x[:bp * L].reshape(bp, L, D), x[bp * L:].reshape(bs, L, D))
```

```python
import functools
import math

import jax
import jax.numpy as jnp
from jax import lax
from jax.experimental import pallas as pl
from jax.experimental.pallas import tpu as pltpu

F32 = jnp.float32
BF16 = jnp.bfloat16

EPS = 1e-6
HY_WIDTH = 512
HY_GROUPS = 8
HY_ORDER = 2
HY_BANDS = 16
HY_EMB = 2 * HY_BANDS + 1
HY_EMB_PAD = 40
HY_FFN = 64
MLA_HEADS = 8
QK_NOPE = 64
QK_ROPE = 32
V_HEAD = 64
Q_RANK = 256
KV_RANK = 128
ROPE_BASE = 10000.0
HEAD_PAD = 128
V_PAD = 80
ATTN_WIDTH = MLA_HEADS * V_HEAD

LANES = 128
SUBLANES = 8
HY_CB = HY_WIDTH // LANES
DFT_N2 = 128
DFT_G = DFT_N2 // SUBLANES
MID_CB = 2
HALO = 16
VMEM_LIMIT = 56 * 1024 * 1024
ATTN_UNROLL = 32
STAT_ROWS = 24

FAST_S_MAX = 64.0
FAST_V_MAX = 2.0 ** 30


def _cparams(sem):
    return pltpu.CompilerParams(dimension_semantics=sem, vmem_limit_bytes=VMEM_LIMIT)


def _dot(a, b):
    return jnp.dot(a, b, preferred_element_type=F32)


def _dot_nt(a, b):
    return lax.dot_general(a, b, (((1,), (1,)), ((), ())), preferred_element_type=F32)


def _rms(x, g):
    return x * lax.rsqrt(jnp.mean(x * x, axis=-1, keepdims=True) + EPS) * g


def _store_grouped(out_ref, val, lane0, ncb):
    for j in range(ncb):
        for nl in range(val.shape[0] // DFT_N2):
            for g in range(DFT_G):
                r0 = nl * DFT_N2 + g * SUBLANES
                out_ref[j, g, nl * SUBLANES:(nl + 1) * SUBLANES, :] = (
                    val[r0:r0 + SUBLANES, lane0 + j * LANES:lane0 + (j + 1) * LANES])


def _fold_lanes(row):
    parts = [row[:, j * LANES:(j + 1) * LANES] for j in range(row.shape[1] // LANES)]
    return functools.reduce(jnp.maximum, parts)


def _inproj_kernel(xp_ref, x_ref, xn_ref, gpre_ref, win_ref, cw_ref, cb_ref, qg_ref, wqat_ref,
                   wqbt_ref, kvg_ref, wk_ref, pk_ref, wvt_ref, qcos_ref, qsin_ref, kcs_ref,
                   v_out, x1_out, x2_out, qt_out, k_out, vt_out, st_out, h_scr, pe_scr,
                   *, tm, tiles_per_seq, scale):
    i = pl.program_id(0)
    t_idx = i % tiles_per_seq
    g = gpre_ref[...]
    h_scr[0:HALO, :] = _rms(xp_ref[...], g).astype(BF16)
    h_scr[HALO:HALO + tm, :] = _rms(x_ref[...], g).astype(BF16)
    h_scr[HALO + tm:2 * HALO + tm, :] = _rms(xn_ref[...], g).astype(BF16)
    proj = _dot(h_scr[...], win_ref[...])

    hw3 = 3 * HY_WIDTH
    pe_scr[...] = proj[:, :hw3]
    row = lax.broadcasted_iota(jnp.int32, (tm, 1), 0)
    prev = pe_scr[HALO - 1:HALO - 1 + tm, :]
    cur = pe_scr[HALO:HALO + tm, :]
    nxt = pe_scr[HALO + 1:HALO + 1 + tm, :]
    prev = jnp.where(jnp.logical_and(row == 0, t_idx == 0), 0.0, prev)
    nxt = jnp.where(jnp.logical_and(row == tm - 1, t_idx == tiles_per_seq - 1), 0.0, nxt)
    u = prev * cw_ref[0:1, :] + cur * cw_ref[1:2, :] + nxt * cw_ref[2:3, :] + cb_ref[...]
    for o, out in enumerate((v_out, x1_out, x2_out)):
        _store_grouped(out, u, o * HY_WIDTH, HY_CB)

    core = proj[HALO:HALO + tm, :]
    cq = core[:, hw3:hw3 + Q_RANK]
    cqn = _rms(cq, qg_ref[...]).astype(BF16)
    qat = _dot_nt(wqat_ref[...], cqn)
    qbt = _dot_nt(wqbt_ref[...], cqn)
    qcos = qcos_ref[...]
    qsin = qsin_ref[...]
    for h in range(MLA_HEADS):
        sl = slice(h * HEAD_PAD, (h + 1) * HEAD_PAD)
        qh = ((qat[sl, :] * qcos + qbt[sl, :] * qsin) * scale).astype(BF16)
        qt_out[h] = qh
        qf = qh.astype(F32)
        st_out[h:h + 1, :] = _fold_lanes(jnp.sum(qf * qf, axis=0, keepdims=True))

    ckv = core[:, hw3 + Q_RANK:hw3 + Q_RANK + KV_RANK]
    ckvn = _rms(ckv, kvg_ref[...]).astype(BF16)
    kp = core[:, hw3 + Q_RANK + KV_RANK:] * kcs_ref[...]
    kp = kp + pltpu.roll(kp, HEAD_PAD - QK_ROPE, axis=1)
    kk = _dot(ckvn, wk_ref[...]) + _dot(kp.astype(BF16), pk_ref[...])
    for h in range(MLA_HEADS):
        kh = kk[:, h * HEAD_PAD:(h + 1) * HEAD_PAD].astype(BF16)
        k_out[h] = kh
        kf = kh.astype(F32)
        kn = jnp.max(jnp.sum(kf * kf, axis=1, keepdims=True), axis=0, keepdims=True)
        st_out[MLA_HEADS + h:MLA_HEADS + h + 1, :] = jnp.broadcast_to(kn, (1, LANES))
    vt = _dot_nt(wvt_ref[...], ckvn)
    st_out[2 * MLA_HEADS:2 * MLA_HEADS + 1, :] = _fold_lanes(jnp.max(jnp.abs(vt), axis=0, keepdims=True))
    st_out[2 * MLA_HEADS + 1:, :] = jnp.zeros((STAT_ROWS - 2 * MLA_HEADS - 1, LANES), F32)
    frow = lax.broadcasted_iota(jnp.int32, (MLA_HEADS * V_PAD, 1), 0)
    vt = vt + jnp.where(frow % V_PAD == V_HEAD, 1.0, 0.0)
    for h in range(MLA_HEADS):
        vt_out[h] = vt[h * V_PAD:(h + 1) * V_PAD, :].astype(BF16)


def _inproj(x, lw, tabs, nb, L):
    M, D = x.shape
    tm = min(512, L)
    tps = L // tm
    nt = M // tm
    hb = tm // HALO
    nhb = M // HALO
    H = MLA_HEADS
    wcols = lw["win"].shape[1]
    const = lambda *shape: pl.BlockSpec(shape, lambda i: (0,) * len(shape))
    tab_t = pl.BlockSpec((HEAD_PAD, tm), lambda i: (0, i % tps))
    in_specs = [
        pl.BlockSpec((HALO, D), lambda i: (jnp.maximum(i * hb - 1, 0), 0)),
        pl.BlockSpec((tm, D), lambda i: (i, 0)),
        pl.BlockSpec((HALO, D), lambda i: (jnp.minimum((i + 1) * hb, nhb - 1), 0)),
        const(1, D), const(D, wcols), const(3, 3 * HY_WIDTH), const(1, 3 * HY_WIDTH),
        const(1, Q_RANK), const(H * HEAD_PAD, Q_RANK), const(H * HEAD_PAD, Q_RANK),
        const(1, KV_RANK), const(KV_RANK, H * HEAD_PAD), const(HEAD_PAD, H * HEAD_PAD),
        const(H * V_PAD, KV_RANK),
        tab_t, tab_t, pl.BlockSpec((tm, HEAD_PAD), lambda i: (i % tps, 0)),
    ]
    hy_spec = pl.BlockSpec((None, HY_CB, DFT_G, tm // DFT_G, LANES), lambda i: (i // tps, 0, 0, i % tps, 0))
    out_specs = [
        hy_spec, hy_spec, hy_spec,
        pl.BlockSpec((None, H, HEAD_PAD, tm), lambda i: (i // tps, 0, 0, i % tps)),
        pl.BlockSpec((None, H, tm, HEAD_PAD), lambda i: (i // tps, 0, i % tps, 0)),
        pl.BlockSpec((None, H, None, V_PAD, tm), lambda i: (i // tps, 0, i % tps, 0, 0)),
        pl.BlockSpec((None, None, STAT_ROWS, LANES), lambda i: (i // tps, i % tps, 0, 0)),
    ]
    hy_shape = jax.ShapeDtypeStruct((nb, HY_CB, DFT_G, L // DFT_G, LANES), F32)
    out_shape = [
        hy_shape, hy_shape, hy_shape,
        jax.ShapeDtypeStruct((nb, H, HEAD_PAD, L), BF16),
        jax.ShapeDtypeStruct((nb, H, L, HEAD_PAD), BF16),
        jax.ShapeDtypeStruct((nb, H, tps, V_PAD, tm), BF16),
        jax.ShapeDtypeStruct((nb, tps, STAT_ROWS, LANES), F32),
    ]
    scale = float((QK_NOPE + QK_ROPE) ** -0.5 * math.log2(math.e))
    return pl.pallas_call(
        functools.partial(_inproj_kernel, tm=tm, tiles_per_seq=tps, scale=scale),
        grid=(nt,), in_specs=in_specs, out_specs=out_specs, out_shape=out_shape,
        scratch_shapes=[pltpu.VMEM((tm + 2 * HALO, D), BF16),
                        pltpu.VMEM((tm + 2 * HALO, 3 * HY_WIDTH), F32)],
        compiler_params=_cparams(("parallel",)), name="inproj",
    )(x, x, x, lw["g_pre"], lw["win"], lw["conv_w"], lw["conv_b"], lw["q_g"], lw["wqat"], lw["wqbt"],
      lw["kv_g"], lw["wk"], tabs["pk"], lw["wvt"], tabs["qcos_t"], tabs["qsin_t"], tabs["kcs"])


def _filter_kernel(z_ref, w1_ref, b1_ref, sf_ref, w2_ref, b2_ref, w3_ref, dec_ref, out_ref, *, rb, L):
    hi = lax.Precision.HIGHEST
    r = pl.program_id(1)
    z = z_ref[...]
    h = jnp.sin(sf_ref[0:1, :] * (jnp.dot(z, w1_ref[...], precision=hi, preferred_element_type=F32) + b1_ref[...]))
    h = jnp.sin(sf_ref[1:2, :] * (jnp.dot(h, w2_ref[...], precision=hi, preferred_element_type=F32) + b2_ref[...]))
    k = jnp.dot(h, w3_ref[...], precision=hi, preferred_element_type=F32)
    k = k * jnp.exp(-z[:, 0:1] * jnp.abs(dec_ref[...]))
    row = r * rb + lax.broadcasted_iota(jnp.int32, (rb, 1), 0)
    k = jnp.where(row == L, 0.0, k)
    _store_grouped(out_ref, k, 0, out_ref.shape[0])


def _filters(ztab, fw, L):
    depth = fw["w1"].shape[0]
    rows = 2 * L
    rb = min(512, L)
    nblk = rows // rb
    half = nblk // 2
    oc = HY_ORDER * HY_WIDTH
    lay = lambda *shape: pl.BlockSpec((None,) + shape, lambda l, r: (l,) + (0,) * len(shape))
    in_specs = [
        pl.BlockSpec((rb, HY_EMB_PAD), lambda l, r: (r, 0)),
        lay(HY_EMB_PAD, HY_FFN), lay(1, HY_FFN), lay(2, HY_FFN), lay(HY_FFN, HY_FFN), lay(1, HY_FFN),
        pl.BlockSpec((None, None, HY_FFN, oc), lambda l, r: (l, r // half, 0, 0)),
        pl.BlockSpec((None, None, 1, oc), lambda l, r: (l, r // half, 0, 0)),
    ]
    return pl.pallas_call(
        functools.partial(_filter_kernel, rb=rb, L=L),
        grid=(depth, nblk), in_specs=in_specs,
        out_specs=pl.BlockSpec((None, oc // LANES, DFT_G, rb // DFT_G, LANES), lambda l, r: (l, 0, 0, r, 0)),
        out_shape=jax.ShapeDtypeStruct((depth, oc // LANES, DFT_G, rows // DFT_G, LANES), F32),
        compiler_params=_cparams(("parallel", "parallel")), name="hyena_filter",
    )(ztab, fw["w1"], fw["b1"], fw["sf"], fw["w2"], fw["b2"], fw["w3"], fw["dec"])


def _rows_of(ref, p, n, lead=()):
    return ref[lead + (pl.ds(p, n, stride=SUBLANES), slice(None))]


def _pack_pair(re, im):
    hi = lax.bitcast_convert_type(re.astype(BF16).astype(F32), jnp.uint32)
    lo = lax.bitcast_convert_type(im.astype(BF16).astype(F32), jnp.uint32)
    return hi | (lo >> 16)


def _unpack_pair(w):
    re = lax.bitcast_convert_type(w & jnp.uint32(0xFFFF0000), F32)
    im = lax.bitcast_convert_type(w << 16, F32)
    return re.astype(BF16), im.astype(BF16)


def _dft1_block(f_ref, xs, out_ref):
    kp = f_ref.shape[0] // 2
    res = _dot(f_ref[...], jnp.concatenate([x.astype(BF16) for x in xs], axis=1))
    w = _pack_pair(res[:kp], res[kp:])
    for kg in range(kp // SUBLANES):
        for p in range(SUBLANES):
            out_ref[kg, p * SUBLANES:(p + 1) * SUBLANES, :] = (
                w[kg * SUBLANES:(kg + 1) * SUBLANES, p * LANES:(p + 1) * LANES])


def _dft1_kernel(f_ref, x_ref, out_ref):
    R = f_ref.shape[1]
    _dft1_block(f_ref, [_rows_of(x_ref, p, R) for p in range(SUBLANES)], out_ref)


def _dft1_spec(kg):
    return pl.BlockSpec((None, None, kg, SUBLANES * SUBLANES, LANES), lambda b, c, g: (b, c, 0, g, 0))


def _dft1(x, f1):
    nb, ncb, _, rows, _ = x.shape
    kp = f1.shape[0] // 2
    kg = kp // SUBLANES
    return pl.pallas_call(
        _dft1_kernel, grid=(nb, ncb, DFT_G),
        in_specs=[pl.BlockSpec(f1.shape, lambda b, c, g: (0, 0)),
                  pl.BlockSpec((None, None, None, rows, LANES), lambda b, c, g: (b, c, g, 0, 0))],
        out_specs=_dft1_spec(kg),
        out_shape=jax.ShapeDtypeStruct((nb, ncb, kg, DFT_N2 * SUBLANES, LANES), jnp.uint32),
        compiler_params=_cparams(("parallel", "parallel", "parallel")), name="dft_outer",
    )(f1, x)


def _inner_fwd(m1_ref, a_ref, kk):
    ar, ai = _unpack_pair(_rows_of(a_ref, kk, DFT_N2))
    return _dot(m1_ref[kk], jnp.concatenate([ar, ai], axis=0))


def _spec_kernel(m1_ref, a_ref, out_ref):
    for kk in range(SUBLANES):
        x = _inner_fwd(m1_ref, a_ref, kk)
        out_ref[kk, 0] = x[:DFT_N2]
        out_ref[kk, 1] = x[DFT_N2:]


def _filter_spectrum(a, m1):
    depth, ncb, kg, rows, _ = a.shape
    n2 = DFT_N2
    return pl.pallas_call(
        _spec_kernel, grid=(kg, depth, ncb),
        in_specs=[pl.BlockSpec((SUBLANES, 2 * n2, 2 * n2), lambda k, l, c: (k, 0, 0)),
                  pl.BlockSpec((None, None, None, rows, LANES), lambda k, l, c: (l, c, k, 0, 0))],
        out_specs=pl.BlockSpec((None, None, SUBLANES, 2, n2, LANES), lambda k, l, c: (l, c, k, 0, 0, 0)),
        out_shape=jax.ShapeDtypeStruct((depth, ncb, kg * SUBLANES, 2, n2, LANES), F32),
        compiler_params=_cparams(("parallel", "parallel", "parallel")), name="filter_spectrum",
    )(m1, a)


def _mid_kernel(m1_ref, m2_ref, kf_ref, a_ref, out_ref):
    ncb = a_ref.shape[0]
    lanes = lambda parts: jnp.concatenate(parts, axis=1)
    for kk in range(SUBLANES):
        ar, ai = _unpack_pair(lanes([_rows_of(a_ref, kk, DFT_N2, (c,)) for c in range(ncb)]))
        x = _dot(m1_ref[kk], jnp.concatenate([ar, ai], axis=0))
        xr, xi = x[:DFT_N2], x[DFT_N2:]
        kr = lanes([kf_ref[c, kk, 0] for c in range(ncb)])
        ki = lanes([kf_ref[c, kk, 1] for c in range(ncb)])
        yr = (xr * kr - xi * ki).astype(BF16)
        yi = (xr * ki + xi * kr).astype(BF16)
        y = _dot(m2_ref[kk], jnp.concatenate([yr, yi], axis=0))
        w = _pack_pair(y[:DFT_N2], y[DFT_N2:])
        for c in range(ncb):
            for g in range(DFT_G):
                out_ref[c, g, kk * SUBLANES:(kk + 1) * SUBLANES, :] = (
                    w[g * SUBLANES:(g + 1) * SUBLANES, c * LANES:(c + 1) * LANES])


def _conv_mid(a, kf, m1, m2, layer, order):
    nb, ncb, kg, rows, _ = a.shape
    n2 = DFT_N2
    mspec = pl.BlockSpec((SUBLANES, 2 * n2, 2 * n2), lambda k, c, b: (k, 0, 0))
    return pl.pallas_call(
        _mid_kernel, grid=(kg, ncb // MID_CB, nb),
        in_specs=[mspec, mspec,
                  pl.BlockSpec((None, MID_CB, SUBLANES, 2, n2, LANES),
                               lambda k, c, b: (layer, order * (ncb // MID_CB) + c, k, 0, 0, 0)),
                  pl.BlockSpec((None, MID_CB, None, rows, LANES), lambda k, c, b: (b, c, k, 0, 0))],
        out_specs=pl.BlockSpec((None, MID_CB, DFT_G, SUBLANES * SUBLANES, LANES), lambda k, c, b: (b, c, 0, k, 0)),
        out_shape=jax.ShapeDtypeStruct((nb, ncb, DFT_G, kg * SUBLANES * SUBLANES, LANES), jnp.uint32),
        compiler_params=_cparams(("parallel", "parallel", "arbitrary")), name="conv_mid",
    )(m1, m2, kf, a)


def _gate_kernel(gre_ref, gim_ref, f_ref, b_ref, z_ref, gate_ref, bias_ref, out_ref, *next_ref):
    R, kp = gre_ref.shape
    bre, bim = _unpack_pair(jnp.concatenate([_rows_of(b_ref, p, kp) for p in range(SUBLANES)], axis=1))
    y = _dot(gre_ref[...], bre) + _dot(gim_ref[...], bim)
    bias = bias_ref[...]
    vals = []
    for p in range(SUBLANES):
        yp = y[:, p * LANES:(p + 1) * LANES]
        vals.append(_rows_of(gate_ref, p, R) * (yp + _rows_of(z_ref, p, R) * bias))
        out_ref[pl.ds(p, R, stride=SUBLANES), :] = vals[p]
    if next_ref:
        _dft1_block(f_ref, vals, next_ref[0])


def _conv_out(bsp, tabs, z, gate, bias, with_next):
    nb, ncb, _, krows, _ = bsp.shape
    rows = z.shape[3]
    gre, gim, f1 = tabs["gre"], tabs["gim"], tabs["f1d"]
    kg = f1.shape[0] // 2 // SUBLANES
    full = lambda t: pl.BlockSpec(t.shape, lambda b, c, g: (0, 0))
    tile = pl.BlockSpec((None, None, None, rows, LANES), lambda b, c, g: (b, c, g, 0, 0))
    out_specs, out_shape = [tile], [jax.ShapeDtypeStruct(z.shape, F32)]
    if with_next:
        out_specs.append(_dft1_spec(kg))
        out_shape.append(jax.ShapeDtypeStruct((nb, ncb, kg, DFT_N2 * SUBLANES, LANES), jnp.uint32))
    return pl.pallas_call(
        _gate_kernel, grid=(nb, ncb, DFT_G),
        in_specs=[full(gre), full(gim), full(f1),
                  pl.BlockSpec((None, None, None, krows, LANES), lambda b, c, g: (b, c, g, 0, 0)),
                  tile, tile, pl.BlockSpec((None, 1, LANES), lambda b, c, g: (c, 0, 0))],
        out_specs=out_specs, out_shape=out_shape,
        compiler_params=_cparams(("parallel", "parallel", "parallel")), name="conv_gate",
    )(gre, gim, f1, bsp, z, gate, bias)


def _hyena_convs(v, x1, x2, kf, bias, tabs, layer):
    b0, b1 = bias[0].reshape(HY_CB, 1, LANES), bias[1].reshape(HY_CB, 1, LANES)
    bsp = _conv_mid(_dft1(v, tabs["f1d"]), kf, tabs["m1"], tabs["m2"], layer, 0)
    z1, a2 = _conv_out(bsp, tabs, v, x1, b0, True)
    bsp = _conv_mid(a2, kf, tabs["m1"], tabs["m2"], layer, 1)
    return _conv_out(bsp, tabs, z1, x2, b1, False)[0]


def _attn_kernel(flag_ref, qt_ref, k_ref, vt_ref, g_ref, o_ref, acc_scr, m_scr, s_scr, *, nk, tk, unroll):
    pair = pl.program_id(0) * pl.num_programs(1) + pl.program_id(1)
    fast = jnp.logical_and(flag_ref[2 * pair] == 1, flag_ref[2 * pair + 1] == 1)
    acc_scr[...] = jnp.zeros(acc_scr.shape, F32)

    @pl.when(fast)
    def _():
        last = 2 * nk - 1

        def scores(cc):
            hd = cc // nk
            c = cc - hd * nk
            return _dot(k_ref[hd, pl.ds(pl.multiple_of(c * tk, tk), tk), :], qt_ref[hd])

        s_scr[0] = scores(0)

        def body(j, carry):
            hd = (j * unroll) // nk
            c0 = j * unroll - hd * nk
            pv = None
            for u in range(unroll):
                s_scr[(u + 1) % 2] = scores(jnp.minimum(j * unroll + u + 1, last))
                pt = jnp.exp2(s_scr[u % 2]).astype(BF16)
                d = _dot(vt_ref[hd, c0 + u], pt)
                pv = d if pv is None else pv + d
            acc_scr[hd] += pv
            return carry
        lax.fori_loop(0, 2 * nk // unroll, body, 0)

    @pl.when(jnp.logical_not(fast))
    def _():
        for hh in range(2):
            qt = qt_ref[hh]
            m_scr[...] = jnp.full(m_scr.shape, -jnp.inf, F32)

            def body(j, carry, hh=hh, qt=qt):
                kk = k_ref[hh, pl.ds(pl.multiple_of(j * tk, tk), tk), :]
                s = _dot(kk, qt)
                m_prev = m_scr[...]
                m_new = jnp.maximum(m_prev, jnp.max(s, axis=0, keepdims=True))
                pt = jnp.exp2(s - m_new).astype(BF16)
                acc_scr[hh] = jnp.exp2(m_prev - m_new) * acc_scr[hh] + _dot(vt_ref[hh, j], pt)
                m_scr[...] = m_new
                return carry
            lax.fori_loop(0, nk, body, 0)

    outs = []
    for hh in range(2):
        acc = acc_scr[hh]
        o = acc[:V_HEAD] / acc[V_HEAD:V_HEAD + 1]
        ms = jnp.mean(o * o, axis=0, keepdims=True)
        outs.append(o * lax.rsqrt(ms + EPS))
    ot = jnp.concatenate(outs, axis=0)
    o_ref[...] = (ot.T * g_ref[...]).astype(o_ref.dtype)


def _attention(flags, qt, k, vt, g_attn, nb, L):
    H = MLA_HEADS
    tq = min(512, L)
    nk, tk = vt.shape[2], vt.shape[4]
    grid_spec = pltpu.PrefetchScalarGridSpec(
        num_scalar_prefetch=1, grid=(nb, H // 2, L // tq),
        in_specs=[pl.BlockSpec((None, 2, HEAD_PAD, tq), lambda b, h, i, f: (b, h, 0, i)),
                  pl.BlockSpec((None, 2, L, HEAD_PAD), lambda b, h, i, f: (b, h, 0, 0)),
                  pl.BlockSpec((None, 2, nk, V_PAD, tk), lambda b, h, i, f: (b, h, 0, 0, 0)),
                  pl.BlockSpec((1, 2 * V_HEAD), lambda b, h, i, f: (0, h))],
        out_specs=pl.BlockSpec((None, tq, 2 * V_HEAD), lambda b, h, i, f: (b, i, h)),
        scratch_shapes=[pltpu.VMEM((2, V_PAD, tq), F32), pltpu.VMEM((1, tq), F32), pltpu.VMEM((2, tk, tq), F32)])
    unroll = math.gcd(nk, ATTN_UNROLL)
    assert unroll % 2 == 0
    return pl.pallas_call(
        functools.partial(_attn_kernel, nk=nk, tk=tk, unroll=unroll), grid_spec=grid_spec,
        out_shape=jax.ShapeDtypeStruct((nb, L, ATTN_WIDTH), BF16),
        compiler_params=_cparams(("parallel", "parallel", "arbitrary")), name="attention",
    )(flags, qt, k, vt, g_attn)


def _fast_flags(stats):
    H = MLA_HEADS
    qn = jnp.max(stats[:, :, 0:H, :], axis=(1, 3))
    kn = jnp.max(stats[:, :, H:2 * H, :], axis=(1, 3))
    vm = jnp.max(stats[:, :, 2 * H, :], axis=(1, 2))
    ok = jnp.logical_and(qn * kn <= FAST_S_MAX * FAST_S_MAX, (vm <= FAST_V_MAX)[:, None])
    return ok.astype(jnp.int32).reshape(-1)


def _mix_mlp_kernel(x_ref, zh_ref, an_ref, ghy_ref, gsum_ref, wo_ref, gpost_ref, gmpre_ref, wup_ref,
                    wdn_ref, gmpost_ref, out_ref):
    zh = jnp.concatenate(
        [jnp.concatenate([zh_ref[j, g, nl * SUBLANES:(nl + 1) * SUBLANES, :] for j in range(HY_CB)], axis=1)
         for nl in range(zh_ref.shape[2] // SUBLANES) for g in range(DFT_G)], axis=0)
    ms = _dot((zh * zh).astype(BF16), gsum_ref[...]) * (HY_GROUPS / HY_WIDTH)
    hn = (zh * lax.rsqrt(ms + EPS) * ghy_ref[...]).astype(BF16)
    mix = _dot(hn, wo_ref[0:HY_WIDTH, :]) + _dot(an_ref[...], wo_ref[HY_WIDTH:, :])
    x = x_ref[...] + _rms(mix, gpost_ref[...])
    h = _rms(x, gmpre_ref[...]).astype(BF16)
    up = jnp.maximum(_dot(h, wup_ref[...]), 0.0)
    m = _dot((up * up).astype(BF16), wdn_ref[...])
    out_ref[...] = x + _rms(m, gmpost_ref[...])


def _mix_mlp(x, zh, an, lw, tabs, L):
    M, D = x.shape
    tm = min(256, L)
    tps = L // tm
    dff = lw["wup"].shape[1]
    mw = HY_WIDTH + ATTN_WIDTH
    const = lambda *shape: pl.BlockSpec(shape, lambda i: (0,) * len(shape), pipeline_mode=pl.Buffered(1))
    rows = lambda w: pl.BlockSpec((tm, w), lambda i: (i, 0))
    return pl.pallas_call(
        _mix_mlp_kernel, grid=(M // tm,),
        in_specs=[rows(D), pl.BlockSpec((None, HY_CB, DFT_G, tm // DFT_G, LANES),
                                        lambda i: (i // tps, 0, 0, i % tps, 0)),
                  rows(ATTN_WIDTH), const(1, HY_WIDTH), const(HY_WIDTH, HY_WIDTH),
                  const(mw, D), const(1, D), const(1, D), const(D, dff), const(dff, D), const(1, D)],
        out_specs=rows(D), out_shape=jax.ShapeDtypeStruct((M, D), F32),
        compiler_params=_cparams(("parallel",)), name="mix_mlp",
    )(x, zh, an, lw["g_hy"], tabs["gsum"], lw["wo"], lw["g_post"], lw["g_mpre"], lw["wup"], lw["wdn"],
      lw["g_mpost"])


def _tables(L):
    n = 2 * L
    n2 = DFT_N2
    n1 = n // n2
    nh = n1 // 2
    kp = -(-(nh + 1) // SUBLANES) * SUBLANES
    two_pi = 2.0 * math.pi

    k1 = jnp.arange(kp, dtype=jnp.int32)
    valid = (k1 <= nh)
    def outer(ncols):
        nn = jnp.arange(ncols, dtype=jnp.int32)
        ang = ((k1[:, None] * nn[None, :]) % n1).astype(F32) * (two_pi / n1)
        c = jnp.where(valid[:, None], jnp.cos(ang), 0.0)
        s = jnp.where(valid[:, None], -jnp.sin(ang), 0.0)
        return jnp.concatenate([c, s], axis=0).astype(BF16)
    f1d = outer(nh)
    f1f = outer(n1)
    nn = jnp.arange(nh, dtype=jnp.int32)
    ang = ((nn[:, None] * k1[None, :]) % n1).astype(F32) * (two_pi / n1)
    wgt = jnp.where(valid, jnp.where((k1 == 0) | (k1 == nh), 1.0, 2.0), 0.0) / n
    gre = (jnp.cos(ang) * wgt[None, :]).astype(BF16)
    gim = (-jnp.sin(ang) * wgt[None, :]).astype(BF16)
    a2 = jnp.arange(n2, dtype=jnp.int32)
    idx = (a2[None, :, None] * a2[None, None, :] * n1 + a2[None, None, :] * k1[:, None, None]) % n
    ph = idx.astype(F32) * (two_pi / n)
    gr, gi = jnp.cos(ph), -jnp.sin(ph)
    m1 = jnp.concatenate([jnp.concatenate([gr, -gi], axis=2), jnp.concatenate([gi, gr], axis=2)], axis=1)
    m2 = jnp.swapaxes(m1, 1, 2)
    inv = 1.0 / (ROPE_BASE ** (jnp.arange(0, QK_ROPE, 2, dtype=F32) / QK_ROPE))
    ang = jnp.arange(L, dtype=F32)[:, None] * inv[None, :]
    cos, sin = jnp.cos(ang), jnp.sin(ang)
    pad = HEAD_PAD - QK_NOPE - QK_ROPE
    qcos = jnp.concatenate([jnp.ones((L, QK_NOPE), F32), cos, cos, jnp.ones((L, pad), F32)], axis=1)
    qsin = jnp.concatenate([jnp.zeros((L, QK_NOPE), F32), sin, sin, jnp.zeros((L, pad), F32)], axis=1)
    kcs = jnp.concatenate([cos, cos, sin, sin, jnp.zeros((L, HEAD_PAD - 2 * QK_ROPE), F32)], axis=1)
    cc = jnp.arange(MLA_HEADS * HEAD_PAD)
    src = jnp.arange(HEAD_PAD)
    pk = ((cc[None, :] % HEAD_PAD) - QK_NOPE == src[:, None]) & (src[:, None] < QK_ROPE)
    grp = jnp.arange(HY_WIDTH) // (HY_WIDTH // HY_GROUPS)
    gsum = (grp[:, None] == grp[None, :]).astype(BF16)
    t = jnp.linspace(0.0, 1.0, L, dtype=F32)[:, None]
    omega = (two_pi / L) * jnp.arange(L, dtype=F32)
    bands = jnp.linspace(1e-4, HY_BANDS - 1, HY_BANDS, dtype=F32)
    phase = omega[:, None] * bands[None, :]
    z = jnp.concatenate([t, jnp.cos(phase), -jnp.sin(phase), jnp.zeros((L, HY_EMB_PAD - HY_EMB), F32)], axis=-1)
    ztab = jnp.concatenate([z, z[:1], z[:0:-1]], axis=0)
    return dict(f1d=f1d, f1f=f1f, gre=gre, gim=gim, m1=m1.astype(BF16), m2=m2.astype(BF16), qcos_t=qcos.T,
                qsin_t=qsin.T, kcs=kcs, pk=pk.astype(BF16), gsum=gsum, ztab=ztab)


def _rot_half_cols(w):
    half = QK_ROPE // 2
    return jnp.concatenate([-w[..., half:], w[..., :half]], axis=-1)


def _layer_weights(i, p):
    D = p["w_in"].shape[1]
    H = MLA_HEADS
    hw3 = 3 * HY_WIDTH
    w_in = p["w_in"][i]
    kpe = w_in[:, hw3 + Q_RANK + KV_RANK:]
    win = jnp.concatenate([w_in, _rot_half_cols(kpe),
                           jnp.zeros((D, HEAD_PAD - 2 * QK_ROPE), F32)], axis=1).astype(BF16)
    dq = QK_NOPE + QK_ROPE
    wq = p["mla_w_uq"][i].reshape(Q_RANK, H, dq)
    zq = jnp.zeros((Q_RANK, H, HEAD_PAD - dq), F32)
    wqa = jnp.concatenate([wq, zq], axis=2).reshape(Q_RANK, H * HEAD_PAD)
    wqb = jnp.concatenate([jnp.zeros((Q_RANK, H, QK_NOPE), F32), _rot_half_cols(wq[..., QK_NOPE:]), zq],
                          axis=2).reshape(Q_RANK, H * HEAD_PAD)
    wkv = p["mla_w_ukv"][i].reshape(KV_RANK, H, QK_NOPE + V_HEAD)
    wk = jnp.concatenate([wkv[..., :QK_NOPE], jnp.zeros((KV_RANK, H, HEAD_PAD - QK_NOPE), F32)], axis=2)
    wv = jnp.concatenate([wkv[..., QK_NOPE:], jnp.zeros((KV_RANK, H, V_PAD - V_HEAD), F32)], axis=2)
    row = lambda a: a.reshape(1, -1)
    return dict(
        win=win, g_pre=row(p["norm_mix_pre"][i]), conv_w=p["hy_conv_w"][i], conv_b=row(p["hy_conv_b"][i]),
        q_g=row(p["mla_q_norm"][i]), wqat=wqa.T.astype(BF16), wqbt=wqb.T.astype(BF16),
        kv_g=row(p["mla_kv_norm"][i]), wk=wk.reshape(KV_RANK, H * HEAD_PAD).astype(BF16),
        wvt=wv.reshape(KV_RANK, H * V_PAD).T.astype(BF16),
        g_hy=row(p["grp_norm_hy"][i]), g_attn=row(p["grp_norm_attn"][i]), wo=p["w_out"][i].astype(BF16),
        g_post=row(p["norm_mix_post"][i]), g_mpre=row(p["norm_mlp_pre"][i]), g_mpost=row(p["norm_mlp_post"][i]),
        wup=p["w_mlp_up"][i].astype(BF16), wdn=p["w_mlp_down"][i].astype(BF16),
    )


def _filter_weights(p):
    depth = p["hy_ffn_w1"].shape[0]
    oc = HY_ORDER * HY_WIDTH
    w1 = jnp.pad(p["hy_ffn_w1"], ((0, 0), (0, HY_EMB_PAD - HY_EMB), (0, 0)))
    w3 = p["hy_ffn_w3"].reshape(depth, HY_FFN, HY_ORDER, 2, HY_WIDTH).transpose(0, 3, 1, 2, 4)
    dec = p["hy_decay"].transpose(0, 2, 1, 3).reshape(depth, 2, 1, oc)
    return dict(w1=w1, b1=p["hy_ffn_b1"][:, None, :], sf=p["hy_sin_freq"], w2=p["hy_ffn_w2"],
                b2=p["hy_ffn_b2"][:, None, :], w3=w3.reshape(depth, 2, HY_FFN, oc), dec=dec)


def kernel(x_prompt, x_sample, w_in, hy_conv_w, hy_conv_b, hy_ffn_w1, hy_ffn_b1, hy_ffn_w2, hy_ffn_b2,
           hy_ffn_w3, hy_sin_freq, hy_decay, hy_bias, mla_q_norm, mla_w_uq, mla_kv_norm, mla_w_ukv,
           grp_norm_hy, grp_norm_attn, w_out, norm_mix_pre, norm_mix_post, norm_mlp_pre, norm_mlp_post,
           w_mlp_up, w_mlp_down):
    p = dict(w_in=w_in, hy_conv_w=hy_conv_w, hy_conv_b=hy_conv_b, hy_ffn_w1=hy_ffn_w1, hy_ffn_b1=hy_ffn_b1,
             hy_ffn_w2=hy_ffn_w2, hy_ffn_b2=hy_ffn_b2, hy_ffn_w3=hy_ffn_w3, hy_sin_freq=hy_sin_freq,
             hy_decay=hy_decay, hy_bias=hy_bias, mla_q_norm=mla_q_norm, mla_w_uq=mla_w_uq,
             mla_kv_norm=mla_kv_norm, mla_w_ukv=mla_w_ukv, grp_norm_hy=grp_norm_hy,
             grp_norm_attn=grp_norm_attn, w_out=w_out, norm_mix_pre=norm_mix_pre, norm_mix_post=norm_mix_post,
             norm_mlp_pre=norm_mlp_pre, norm_mlp_post=norm_mlp_post, w_mlp_up=w_mlp_up, w_mlp_down=w_mlp_down)
    bp, L, D = x_prompt.shape
    bs, Ls, _ = x_sample.shape
    assert L == Ls and L % (DFT_N2 * SUBLANES) == 0
    nb = bp + bs
    depth = w_in.shape[0]

    tabs = _tables(L)
    kc = _filters(tabs["ztab"], _filter_weights(p), L)
    kf = _filter_spectrum(_dft1(kc, tabs["f1f"]), tabs["m1"])

    x = jnp.concatenate([x_prompt.reshape(bp * L, D), x_sample.reshape(bs * L, D)], axis=0)
    for i in range(depth):
        lw = _layer_weights(i, p)
        v, x1, x2, qt, k, vt, stats = _inproj(x, lw, tabs, nb, L)
        z = _hyena_convs(v, x1, x2, kf, hy_bias[i], tabs, i)
        an = _attention(_fast_flags(stats), qt, k, vt, lw["g_attn"], nb, L).reshape(nb * L, ATTN_WIDTH)
        x = _mix_mlp(x, z, an, lw, tabs, L)
    return (x[:bp * L].reshape(bp, L, D), x[bp * L:].reshape(bs, L, D))
```

```python
import functools
import math

import jax
import jax.numpy as jnp
from jax import lax
from jax.experimental import pallas as pl
from jax.experimental.pallas import tpu as pltpu

F32 = jnp.float32
BF16 = jnp.bfloat16

EPS = 1e-6
HY_WIDTH = 512
HY_GROUPS = 8
HY_ORDER = 2
HY_BANDS = 16
HY_EMB = 2 * HY_BANDS + 1
HY_EMB_PAD = 40
HY_FFN = 64
MLA_HEADS = 8
QK_NOPE = 64
QK_ROPE = 32
V_HEAD = 64
Q_RANK = 256
KV_RANK = 128
ROPE_BASE = 10000.0
HEAD_PAD = 128
V_PAD = 80
ATTN_WIDTH = MLA_HEADS * V_HEAD

LANES = 128
SUBLANES = 8
HY_CB = HY_WIDTH // LANES
DFT_N2 = 128
DFT_G = DFT_N2 // SUBLANES
MID_CB = 2
OUTER_GB = 2
HALO = 16
VMEM_LIMIT = 56 * 1024 * 1024
ATTN_UNROLL = 32
STAT_ROWS = 24

FAST_S_MAX = 64.0
FAST_V_MAX = 2.0 ** 30


def _cparams(sem):
    return pltpu.CompilerParams(dimension_semantics=sem, vmem_limit_bytes=VMEM_LIMIT)


def _dot(a, b):
    return jnp.dot(a, b, preferred_element_type=F32)


def _dot_nt(a, b):
    return lax.dot_general(a, b, (((1,), (1,)), ((), ())), preferred_element_type=F32)


def _rms(x, g):
    return x * lax.rsqrt(jnp.mean(x * x, axis=-1, keepdims=True) + EPS) * g


def _store_grouped(out_ref, val, lane0, ncb):
    for j in range(ncb):
        for nl in range(val.shape[0] // DFT_N2):
            for g in range(DFT_G):
                r0 = nl * DFT_N2 + g * SUBLANES
                out_ref[j, g, nl * SUBLANES:(nl + 1) * SUBLANES, :] = (
                    val[r0:r0 + SUBLANES, lane0 + j * LANES:lane0 + (j + 1) * LANES])


def _fold_lanes(row):
    parts = [row[:, j * LANES:(j + 1) * LANES] for j in range(row.shape[1] // LANES)]
    return functools.reduce(jnp.maximum, parts)


def _inproj_kernel(xp_ref, x_ref, xn_ref, gpre_ref, win_ref, cw_ref, cb_ref, qg_ref, wqt_ref,
                   kvg_ref, wkp_ref, wvt_ref, qcos_ref, qsin_ref, kcs_ref,
                   v_out, x1_out, x2_out, qt_out, k_out, vt_out, st_out, h_scr, pe_scr,
                   *, tm, tiles_per_seq, scale):
    i = pl.program_id(0)
    t_idx = i % tiles_per_seq
    g = gpre_ref[...]
    h_scr[0:HALO, :] = _rms(xp_ref[...], g).astype(BF16)
    h_scr[HALO:HALO + tm, :] = _rms(x_ref[...], g).astype(BF16)
    h_scr[HALO + tm:2 * HALO + tm, :] = _rms(xn_ref[...], g).astype(BF16)
    proj = _dot(h_scr[...], win_ref[...])

    hw3 = 3 * HY_WIDTH
    pe_scr[...] = proj[:, :hw3]
    row = lax.broadcasted_iota(jnp.int32, (tm, 1), 0)
    prev = pe_scr[HALO - 1:HALO - 1 + tm, :]
    cur = pe_scr[HALO:HALO + tm, :]
    nxt = pe_scr[HALO + 1:HALO + 1 + tm, :]
    prev = jnp.where(jnp.logical_and(row == 0, t_idx == 0), 0.0, prev)
    nxt = jnp.where(jnp.logical_and(row == tm - 1, t_idx == tiles_per_seq - 1), 0.0, nxt)
    u = prev * cw_ref[0:1, :] + cur * cw_ref[1:2, :] + nxt * cw_ref[2:3, :] + cb_ref[...]
    for o, out in enumerate((v_out, x1_out, x2_out)):
        _store_grouped(out, u, o * HY_WIDTH, HY_CB)

    core = proj[HALO:HALO + tm, :]
    cq = core[:, hw3:hw3 + Q_RANK]
    cqn = _rms(cq, qg_ref[...]).astype(BF16)
    dq = QK_NOPE + QK_ROPE
    qall = _dot_nt(wqt_ref[...], cqn)
    rcos = qcos_ref[...]
    rsin = qsin_ref[...]
    zpad = jnp.zeros((HEAD_PAD - dq, tm), F32)
    for h in range(MLA_HEADS):
        rot = qall[MLA_HEADS * dq + h * QK_ROPE:MLA_HEADS * dq + (h + 1) * QK_ROPE]
        rope = qall[h * dq + QK_NOPE:(h + 1) * dq] * rcos + rot * rsin
        qh = (jnp.concatenate([qall[h * dq:h * dq + QK_NOPE], rope, zpad], axis=0) * scale).astype(BF16)
        qt_out[h] = qh
        qf = qh.astype(F32)
        st_out[h:h + 1, :] = _fold_lanes(jnp.sum(qf * qf, axis=0, keepdims=True))

    ckv = core[:, hw3 + Q_RANK:hw3 + Q_RANK + KV_RANK]
    ckvn = _rms(ckv, kvg_ref[...]).astype(BF16)
    kp = core[:, hw3 + Q_RANK + KV_RANK:] * kcs_ref[...]
    kp = kp + pltpu.roll(kp, HEAD_PAD - QK_ROPE, axis=1)
    kk = _dot(jnp.concatenate([ckvn, kp.astype(BF16)], axis=1), wkp_ref[...])
    for h in range(MLA_HEADS):
        kh = kk[:, h * HEAD_PAD:(h + 1) * HEAD_PAD].astype(BF16)
        k_out[h] = kh
        kf = kh.astype(F32)
        kn = jnp.max(jnp.sum(kf * kf, axis=1, keepdims=True), axis=0, keepdims=True)
        st_out[MLA_HEADS + h:MLA_HEADS + h + 1, :] = jnp.broadcast_to(kn, (1, LANES))
    vt = _dot_nt(wvt_ref[...], ckvn)
    st_out[2 * MLA_HEADS:2 * MLA_HEADS + 1, :] = _fold_lanes(jnp.max(jnp.abs(vt), axis=0, keepdims=True))
    st_out[2 * MLA_HEADS + 1:, :] = jnp.zeros((STAT_ROWS - 2 * MLA_HEADS - 1, LANES), F32)
    frow = lax.broadcasted_iota(jnp.int32, (MLA_HEADS * V_PAD, 1), 0)
    vt = vt + jnp.where(frow % V_PAD == V_HEAD, 1.0, 0.0)
    for h in range(MLA_HEADS):
        vt_out[h] = vt[h * V_PAD:(h + 1) * V_PAD, :].astype(BF16)


def _inproj(x, lw, tabs, nb, L):
    M, D = x.shape
    tm = min(512, L)
    tps = L // tm
    nt = M // tm
    hb = tm // HALO
    nhb = M // HALO
    H = MLA_HEADS
    wcols = lw["win"].shape[1]
    const = lambda *shape: pl.BlockSpec(shape, lambda i: (0,) * len(shape))
    tab_t = pl.BlockSpec((QK_ROPE, tm), lambda i: (0, i % tps))
    in_specs = [
        pl.BlockSpec((HALO, D), lambda i: (jnp.maximum(i * hb - 1, 0), 0)),
        pl.BlockSpec((tm, D), lambda i: (i, 0)),
        pl.BlockSpec((HALO, D), lambda i: (jnp.minimum((i + 1) * hb, nhb - 1), 0)),
        const(1, D), const(D, wcols), const(3, 3 * HY_WIDTH), const(1, 3 * HY_WIDTH),
        const(1, Q_RANK), const(H * HEAD_PAD, Q_RANK),
        const(1, KV_RANK), const(KV_RANK + HEAD_PAD, H * HEAD_PAD),
        const(H * V_PAD, KV_RANK),
        tab_t, tab_t, pl.BlockSpec((tm, HEAD_PAD), lambda i: (i % tps, 0)),
    ]
    hy_spec = pl.BlockSpec((None, HY_CB, DFT_G, tm // DFT_G, LANES), lambda i: (i // tps, 0, 0, i % tps, 0))
    out_specs = [
        hy_spec, hy_spec, hy_spec,
        pl.BlockSpec((None, H, HEAD_PAD, tm), lambda i: (i // tps, 0, 0, i % tps)),
        pl.BlockSpec((None, H, tm, HEAD_PAD), lambda i: (i // tps, 0, i % tps, 0)),
        pl.BlockSpec((None, H, None, V_PAD, tm), lambda i: (i // tps, 0, i % tps, 0, 0)),
        pl.BlockSpec((None, None, STAT_ROWS, LANES), lambda i: (i // tps, i % tps, 0, 0)),
    ]
    hy_shape = jax.ShapeDtypeStruct((nb, HY_CB, DFT_G, L // DFT_G, LANES), F32)
    out_shape = [
        hy_shape, hy_shape, hy_shape,
        jax.ShapeDtypeStruct((nb, H, HEAD_PAD, L), BF16),
        jax.ShapeDtypeStruct((nb, H, L, HEAD_PAD), BF16),
        jax.ShapeDtypeStruct((nb, H, tps, V_PAD, tm), BF16),
        jax.ShapeDtypeStruct((nb, tps, STAT_ROWS, LANES), F32),
    ]
    scale = float((QK_NOPE + QK_ROPE) ** -0.5 * math.log2(math.e))
    return pl.pallas_call(
        functools.partial(_inproj_kernel, tm=tm, tiles_per_seq=tps, scale=scale),
        grid=(nt,), in_specs=in_specs, out_specs=out_specs, out_shape=out_shape,
        scratch_shapes=[pltpu.VMEM((tm + 2 * HALO, D), BF16),
                        pltpu.VMEM((tm + 2 * HALO, 3 * HY_WIDTH), F32)],
        compiler_params=_cparams(("parallel",)), name="inproj",
    )(x, x, x, lw["g_pre"], lw["win"], lw["conv_w"], lw["conv_b"], lw["q_g"], lw["wqt"],
      lw["kv_g"], lw["wkp"], lw["wvt"], tabs["rcos_t"], tabs["rsin_t"], tabs["kcs"])


def _filter_kernel(t_ref, zt_ref, w1_ref, b1_ref, sf_ref, w2_ref, b2_ref, w3_ref, dec_ref, out_ref, *, rb, L):
    hi = lax.Precision.HIGHEST
    r = pl.program_id(1)
    h = jnp.sin(sf_ref[:, 0:1] * (jnp.dot(w1_ref[...], zt_ref[...], precision=hi, preferred_element_type=F32)
                                  + b1_ref[...]))
    h = jnp.sin(sf_ref[:, 1:2] * (jnp.dot(w2_ref[...], h, precision=hi, preferred_element_type=F32) + b2_ref[...]))
    tn = lambda a, b: lax.dot_general(a, b, (((0,), (0,)), ((), ())), preferred_element_type=F32)
    h_hi = h.astype(BF16)
    h_lo = (h - h_hi.astype(F32)).astype(BF16)
    k = tn(h_hi, w3_ref[0]) + (tn(h_hi, w3_ref[1]) + tn(h_lo, w3_ref[0]))
    k = k * jnp.exp(-t_ref[...] * jnp.abs(dec_ref[...]))
    row = r * rb + lax.broadcasted_iota(jnp.int32, (rb, 1), 0)
    k = jnp.where(row == L, 0.0, k)
    _store_grouped(out_ref, k, 0, out_ref.shape[0])


def _filters(ztab, fw, L):
    depth = fw["w1"].shape[0]
    rows = 2 * L
    rb = min(512, L)
    nblk = rows // rb
    half = nblk // 2
    oc = HY_ORDER * HY_WIDTH
    lay = lambda *shape: pl.BlockSpec((None,) + shape, lambda l, r: (l,) + (0,) * len(shape))
    in_specs = [
        pl.BlockSpec((rb, 1), lambda l, r: (r, 0)), pl.BlockSpec((HY_EMB_PAD, rb), lambda l, r: (0, r)),
        lay(HY_FFN, HY_EMB_PAD), lay(HY_FFN, 1), lay(HY_FFN, 2), lay(HY_FFN, HY_FFN), lay(HY_FFN, 1),
        pl.BlockSpec((None, None, 2, HY_FFN, oc), lambda l, r: (l, r // half, 0, 0, 0)),
        pl.BlockSpec((None, None, 1, oc), lambda l, r: (l, r // half, 0, 0)),
    ]
    return pl.pallas_call(
        functools.partial(_filter_kernel, rb=rb, L=L),
        grid=(depth, nblk), in_specs=in_specs,
        out_specs=pl.BlockSpec((None, oc // LANES, DFT_G, rb // DFT_G, LANES), lambda l, r: (l, 0, 0, r, 0)),
        out_shape=jax.ShapeDtypeStruct((depth, oc // LANES, DFT_G, rows // DFT_G, LANES), F32),
        compiler_params=_cparams(("parallel", "parallel")), name="hyena_filter",
    )(ztab[:, 0:1], ztab.T, fw["w1"], fw["b1"], fw["sf"], fw["w2"], fw["b2"], fw["w3"], fw["dec"])


def _rows_of(ref, p, n, lead=()):
    return ref[lead + (pl.ds(p, n, stride=SUBLANES), slice(None))]


def _pack_pair(re, im):
    hi = lax.bitcast_convert_type(re.astype(BF16).astype(F32), jnp.uint32)
    lo = lax.bitcast_convert_type(im.astype(BF16).astype(F32), jnp.uint32)
    return hi | (lo >> 16)


def _unpack_pair(w):
    re = lax.bitcast_convert_type(w & jnp.uint32(0xFFFF0000), F32)
    im = lax.bitcast_convert_type(w << 16, F32)
    return re.astype(BF16), im.astype(BF16)


def _dft1_block(f_ref, xs, out_ref, gg):
    kp = f_ref.shape[0] // 2
    res = _dot(f_ref[...], jnp.concatenate([x.astype(BF16) for x in xs], axis=1))
    w = _pack_pair(res[:kp], res[kp:])
    for kg in range(kp // SUBLANES):
        for p in range(SUBLANES):
            r0 = (gg * SUBLANES + p) * SUBLANES
            out_ref[kg, r0:r0 + SUBLANES, :] = w[kg * SUBLANES:(kg + 1) * SUBLANES, p * LANES:(p + 1) * LANES]


def _dft1_kernel(f_ref, x_ref, out_ref):
    R = f_ref.shape[1]
    for gg in range(OUTER_GB):
        _dft1_block(f_ref, [_rows_of(x_ref, p, R, (gg,)) for p in range(SUBLANES)], out_ref, gg)


def _dft1_spec(kg):
    return pl.BlockSpec((None, None, kg, OUTER_GB * SUBLANES * SUBLANES, LANES), lambda b, c, g: (b, c, 0, g, 0))


def _dft1(x, f1):
    nb, ncb, _, rows, _ = x.shape
    kp = f1.shape[0] // 2
    kg = kp // SUBLANES
    return pl.pallas_call(
        _dft1_kernel, grid=(nb, ncb, DFT_G // OUTER_GB),
        in_specs=[pl.BlockSpec(f1.shape, lambda b, c, g: (0, 0)),
                  pl.BlockSpec((None, None, OUTER_GB, rows, LANES), lambda b, c, g: (b, c, g, 0, 0))],
        out_specs=_dft1_spec(kg),
        out_shape=jax.ShapeDtypeStruct((nb, ncb, kg, DFT_N2 * SUBLANES, LANES), jnp.uint32),
        compiler_params=_cparams(("parallel", "parallel", "parallel")), name="dft_outer",
    )(f1, x)


def _inner_fwd(m1_ref, a_ref, kk):
    ar, ai = _unpack_pair(_rows_of(a_ref, kk, DFT_N2))
    return _dot(m1_ref[kk], jnp.concatenate([ar, ai], axis=0))


def _spec_kernel(m1_ref, a_ref, out_ref):
    for kk in range(SUBLANES):
        x = _inner_fwd(m1_ref, a_ref, kk)
        out_ref[kk, 0] = x[:DFT_N2]
        out_ref[kk, 1] = x[DFT_N2:]


def _filter_spectrum(a, m1):
    depth, ncb, kg, rows, _ = a.shape
    n2 = DFT_N2
    return pl.pallas_call(
        _spec_kernel, grid=(kg, depth, ncb),
        in_specs=[pl.BlockSpec((SUBLANES, 2 * n2, 2 * n2), lambda k, l, c: (k, 0, 0)),
                  pl.BlockSpec((None, None, None, rows, LANES), lambda k, l, c: (l, c, k, 0, 0))],
        out_specs=pl.BlockSpec((None, None, SUBLANES, 2, n2, LANES), lambda k, l, c: (l, c, k, 0, 0, 0)),
        out_shape=jax.ShapeDtypeStruct((depth, ncb, kg * SUBLANES, 2, n2, LANES), F32),
        compiler_params=_cparams(("parallel", "parallel", "parallel")), name="filter_spectrum",
    )(m1, a)


def _mid_kernel(m1_ref, m2_ref, kf_ref, a_ref, out_ref):
    ncb = a_ref.shape[0]
    lanes = lambda parts: jnp.concatenate(parts, axis=1)
    for kk in range(SUBLANES):
        ar, ai = _unpack_pair(lanes([_rows_of(a_ref, kk, DFT_N2, (c,)) for c in range(ncb)]))
        x = _dot(m1_ref[kk], jnp.concatenate([ar, ai], axis=0))
        xr, xi = x[:DFT_N2], x[DFT_N2:]
        kr = lanes([kf_ref[c, kk, 0] for c in range(ncb)])
        ki = lanes([kf_ref[c, kk, 1] for c in range(ncb)])
        yr = (xr * kr - xi * ki).astype(BF16)
        yi = (xr * ki + xi * kr).astype(BF16)
        y = _dot(m2_ref[kk], jnp.concatenate([yr, yi], axis=0))
        w = _pack_pair(y[:DFT_N2], y[DFT_N2:])
        for c in range(ncb):
            for g in range(DFT_G):
                out_ref[c, g, kk * SUBLANES:(kk + 1) * SUBLANES, :] = (
                    w[g * SUBLANES:(g + 1) * SUBLANES, c * LANES:(c + 1) * LANES])


def _conv_mid(a, kf, m1, m2, layer, order):
    nb, ncb, kg, rows, _ = a.shape
    n2 = DFT_N2
    mspec = pl.BlockSpec((SUBLANES, 2 * n2, 2 * n2), lambda k, c, b: (k, 0, 0))
    return pl.pallas_call(
        _mid_kernel, grid=(kg, ncb // MID_CB, nb),
        in_specs=[mspec, mspec,
                  pl.BlockSpec((None, MID_CB, SUBLANES, 2, n2, LANES),
                               lambda k, c, b: (layer, order * (ncb // MID_CB) + c, k, 0, 0, 0)),
                  pl.BlockSpec((None, MID_CB, None, rows, LANES), lambda k, c, b: (b, c, k, 0, 0))],
        out_specs=pl.BlockSpec((None, MID_CB, DFT_G, SUBLANES * SUBLANES, LANES), lambda k, c, b: (b, c, 0, k, 0)),
        out_shape=jax.ShapeDtypeStruct((nb, ncb, DFT_G, kg * SUBLANES * SUBLANES, LANES), jnp.uint32),
        compiler_params=_cparams(("parallel", "parallel", "arbitrary")), name="conv_mid",
    )(m1, m2, kf, a)


def _gate_kernel(gre_ref, gim_ref, f_ref, b_ref, z_ref, gate_ref, bias_ref, out_ref, *next_ref):
    R, kp = gre_ref.shape
    bias = bias_ref[...]
    for gg in range(OUTER_GB):
        bre, bim = _unpack_pair(jnp.concatenate([_rows_of(b_ref, p, kp, (gg,)) for p in range(SUBLANES)], axis=1))
        y = _dot(gre_ref[...], bre) + _dot(gim_ref[...], bim)
        vals = []
        for p in range(SUBLANES):
            yp = y[:, p * LANES:(p + 1) * LANES]
            vals.append(_rows_of(gate_ref, p, R, (gg,)) * (yp + _rows_of(z_ref, p, R, (gg,)) * bias))
            out_ref[gg, pl.ds(p, R, stride=SUBLANES), :] = vals[p]
        if next_ref:
            _dft1_block(f_ref, vals, next_ref[0], gg)


def _conv_out(bsp, tabs, z, gate, bias, with_next):
    nb, ncb, _, krows, _ = bsp.shape
    rows = z.shape[3]
    gre, gim, f1 = tabs["gre"], tabs["gim"], tabs["f1d"]
    kg = f1.shape[0] // 2 // SUBLANES
    full = lambda t: pl.BlockSpec(t.shape, lambda b, c, g: (0, 0))
    tile = pl.BlockSpec((None, None, OUTER_GB, rows, LANES), lambda b, c, g: (b, c, g, 0, 0))
    out_specs, out_shape = [tile], [jax.ShapeDtypeStruct(z.shape, F32)]
    if with_next:
        out_specs.append(_dft1_spec(kg))
        out_shape.append(jax.ShapeDtypeStruct((nb, ncb, kg, DFT_N2 * SUBLANES, LANES), jnp.uint32))
    return pl.pallas_call(
        _gate_kernel, grid=(nb, ncb, DFT_G // OUTER_GB),
        in_specs=[full(gre), full(gim), full(f1),
                  pl.BlockSpec((None, None, OUTER_GB, krows, LANES), lambda b, c, g: (b, c, g, 0, 0)),
                  tile, tile, pl.BlockSpec((None, 1, LANES), lambda b, c, g: (c, 0, 0))],
        out_specs=out_specs, out_shape=out_shape,
        compiler_params=_cparams(("parallel", "parallel", "parallel")), name="conv_gate",
    )(gre, gim, f1, bsp, z, gate, bias)


def _hyena_convs(v, x1, x2, kf, bias, tabs, layer):
    b0, b1 = bias[0].reshape(HY_CB, 1, LANES), bias[1].reshape(HY_CB, 1, LANES)
    bsp = _conv_mid(_dft1(v, tabs["f1d"]), kf, tabs["m1"], tabs["m2"], layer, 0)
    z1, a2 = _conv_out(bsp, tabs, v, x1, b0, True)
    bsp = _conv_mid(a2, kf, tabs["m1"], tabs["m2"], layer, 1)
    return _conv_out(bsp, tabs, z1, x2, b1, False)[0]


def _attn_kernel(flag_ref, qt_ref, k_ref, vt_ref, g_ref, o_ref, acc_scr, m_scr, s_scr, *, nk, tk, unroll):
    pair = pl.program_id(0) * pl.num_programs(1) + pl.program_id(1)
    fast = jnp.logical_and(flag_ref[2 * pair] == 1, flag_ref[2 * pair + 1] == 1)
    acc_scr[...] = jnp.zeros(acc_scr.shape, F32)

    @pl.when(fast)
    def _():
        last = 2 * nk - 1

        def scores(cc):
            hd = cc // nk
            c = cc - hd * nk
            return _dot(k_ref[hd, pl.ds(pl.multiple_of(c * tk, tk), tk), :], qt_ref[hd])

        s_scr[0] = scores(0)

        def body(j, carry):
            hd = (j * unroll) // nk
            c0 = j * unroll - hd * nk
            pv = None
            for u in range(unroll):
                s_scr[(u + 1) % 2] = scores(jnp.minimum(j * unroll + u + 1, last))
                pt = jnp.exp2(s_scr[u % 2]).astype(BF16)
                d = _dot(vt_ref[hd, c0 + u], pt)
                pv = d if pv is None else pv + d
            acc_scr[hd] += pv
            return carry
        lax.fori_loop(0, 2 * nk // unroll, body, 0)

    @pl.when(jnp.logical_not(fast))
    def _():
        for hh in range(2):
            qt = qt_ref[hh]
            m_scr[...] = jnp.full(m_scr.shape, -jnp.inf, F32)

            def body(j, carry, hh=hh, qt=qt):
                kk = k_ref[hh, pl.ds(pl.multiple_of(j * tk, tk), tk), :]
                s = _dot(kk, qt)
                m_prev = m_scr[...]
                m_new = jnp.maximum(m_prev, jnp.max(s, axis=0, keepdims=True))
                pt = jnp.exp2(s - m_new).astype(BF16)
                acc_scr[hh] = jnp.exp2(m_prev - m_new) * acc_scr[hh] + _dot(vt_ref[hh, j], pt)
                m_scr[...] = m_new
                return carry
            lax.fori_loop(0, nk, body, 0)

    outs = []
    for hh in range(2):
        acc = acc_scr[hh]
        o = acc[:V_HEAD] / acc[V_HEAD:V_HEAD + 1]
        ms = jnp.mean(o * o, axis=0, keepdims=True)
        outs.append(o * lax.rsqrt(ms + EPS))
    ot = jnp.concatenate(outs, axis=0)
    o_ref[...] = (ot.T * g_ref[...]).astype(o_ref.dtype)


def _attention(flags, qt, k, vt, g_attn, nb, L):
    H = MLA_HEADS
    tq = min(512, L)
    nk, tk = vt.shape[2], vt.shape[4]
    grid_spec = pltpu.PrefetchScalarGridSpec(
        num_scalar_prefetch=1, grid=(nb, H // 2, L // tq),
        in_specs=[pl.BlockSpec((None, 2, HEAD_PAD, tq), lambda b, h, i, f: (b, h, 0, i)),
                  pl.BlockSpec((None, 2, L, HEAD_PAD), lambda b, h, i, f: (b, h, 0, 0)),
                  pl.BlockSpec((None, 2, nk, V_PAD, tk), lambda b, h, i, f: (b, h, 0, 0, 0)),
                  pl.BlockSpec((1, 2 * V_HEAD), lambda b, h, i, f: (0, h))],
        out_specs=pl.BlockSpec((None, tq, 2 * V_HEAD), lambda b, h, i, f: (b, i, h)),
        scratch_shapes=[pltpu.VMEM((2, V_PAD, tq), F32), pltpu.VMEM((1, tq), F32), pltpu.VMEM((2, tk, tq), F32)])
    unroll = math.gcd(nk, ATTN_UNROLL)
    assert unroll % 2 == 0
    return pl.pallas_call(
        functools.partial(_attn_kernel, nk=nk, tk=tk, unroll=unroll), grid_spec=grid_spec,
        out_shape=jax.ShapeDtypeStruct((nb, L, ATTN_WIDTH), BF16),
        compiler_params=_cparams(("parallel", "parallel", "arbitrary")), name="attention",
    )(flags, qt, k, vt, g_attn)


def _fast_flags(stats):
    H = MLA_HEADS
    qn = jnp.max(stats[:, :, 0:H, :], axis=(1, 3))
    kn = jnp.max(stats[:, :, H:2 * H, :], axis=(1, 3))
    vm = jnp.max(stats[:, :, 2 * H, :], axis=(1, 2))
    ok = jnp.logical_and(qn * kn <= FAST_S_MAX * FAST_S_MAX, (vm <= FAST_V_MAX)[:, None])
    return ok.astype(jnp.int32).reshape(-1)


def _mix_mlp_kernel(x_ref, zh_ref, an_ref, ghy_ref, gsum_ref, wo_ref, gpost_ref, gmpre_ref, wup_ref,
                    wdn_ref, gmpost_ref, out_ref):
    zh = jnp.concatenate(
        [jnp.concatenate([zh_ref[j, g, nl * SUBLANES:(nl + 1) * SUBLANES, :] for j in range(HY_CB)], axis=1)
         for nl in range(zh_ref.shape[2] // SUBLANES) for g in range(DFT_G)], axis=0)
    ms = _dot((zh * zh).astype(BF16), gsum_ref[...]) * (HY_GROUPS / HY_WIDTH)
    hn = (zh * lax.rsqrt(ms + EPS) * ghy_ref[...]).astype(BF16)
    mix = _dot(hn, wo_ref[0:HY_WIDTH, :]) + _dot(an_ref[...], wo_ref[HY_WIDTH:, :])
    x = x_ref[...] + _rms(mix, gpost_ref[...])
    h = _rms(x, gmpre_ref[...]).astype(BF16)
    up = jnp.maximum(_dot(h, wup_ref[...]), 0.0)
    m = _dot((up * up).astype(BF16), wdn_ref[...])
    out_ref[...] = x + _rms(m, gmpost_ref[...])


def _mix_mlp(x, zh, an, lw, tabs, L):
    M, D = x.shape
    tm = min(256, L)
    tps = L // tm
    dff = lw["wup"].shape[1]
    mw = HY_WIDTH + ATTN_WIDTH
    const = lambda *shape: pl.BlockSpec(shape, lambda i: (0,) * len(shape), pipeline_mode=pl.Buffered(1))
    rows = lambda w: pl.BlockSpec((tm, w), lambda i: (i, 0))
    return pl.pallas_call(
        _mix_mlp_kernel, grid=(M // tm,),
        in_specs=[rows(D), pl.BlockSpec((None, HY_CB, DFT_G, tm // DFT_G, LANES),
                                        lambda i: (i // tps, 0, 0, i % tps, 0)),
                  rows(ATTN_WIDTH), const(1, HY_WIDTH), const(HY_WIDTH, HY_WIDTH),
                  const(mw, D), const(1, D), const(1, D), const(D, dff), const(dff, D), const(1, D)],
        out_specs=rows(D), out_shape=jax.ShapeDtypeStruct((M, D), F32),
        compiler_params=_cparams(("parallel",)), name="mix_mlp",
    )(x, zh, an, lw["g_hy"], tabs["gsum"], lw["wo"], lw["g_post"], lw["g_mpre"], lw["wup"], lw["wdn"],
      lw["g_mpost"])


def _tables(L):
    n = 2 * L
    n2 = DFT_N2
    n1 = n // n2
    nh = n1 // 2
    kp = -(-(nh + 1) // SUBLANES) * SUBLANES
    two_pi = 2.0 * math.pi

    k1 = jnp.arange(kp, dtype=jnp.int32)
    valid = (k1 <= nh)
    def outer(ncols):
        nn = jnp.arange(ncols, dtype=jnp.int32)
        ang = ((k1[:, None] * nn[None, :]) % n1).astype(F32) * (two_pi / n1)
        c = jnp.where(valid[:, None], jnp.cos(ang), 0.0)
        s = jnp.where(valid[:, None], -jnp.sin(ang), 0.0)
        return jnp.concatenate([c, s], axis=0).astype(BF16)
    f1d = outer(nh)
    f1f = outer(n1)
    nn = jnp.arange(nh, dtype=jnp.int32)
    ang = ((nn[:, None] * k1[None, :]) % n1).astype(F32) * (two_pi / n1)
    wgt = jnp.where(valid, jnp.where((k1 == 0) | (k1 == nh), 1.0, 2.0), 0.0) / n
    gre = (jnp.cos(ang) * wgt[None, :]).astype(BF16)
    gim = (-jnp.sin(ang) * wgt[None, :]).astype(BF16)
    a2 = jnp.arange(n2, dtype=jnp.int32)
    idx = (a2[None, :, None] * a2[None, None, :] * n1 + a2[None, None, :] * k1[:, None, None]) % n
    ph = idx.astype(F32) * (two_pi / n)
    gr, gi = jnp.cos(ph), -jnp.sin(ph)
    m1 = jnp.concatenate([jnp.concatenate([gr, -gi], axis=2), jnp.concatenate([gi, gr], axis=2)], axis=1)
    m2 = jnp.swapaxes(m1, 1, 2)
    inv = 1.0 / (ROPE_BASE ** (jnp.arange(0, QK_ROPE, 2, dtype=F32) / QK_ROPE))
    ang = jnp.arange(L, dtype=F32)[:, None] * inv[None, :]
    cos, sin = jnp.cos(ang), jnp.sin(ang)
    rcos = jnp.concatenate([cos, cos], axis=1)
    rsin = jnp.concatenate([sin, sin], axis=1)
    kcs =jnp.concatenate([cos, cos, sin, sin, jnp.zeros((L, HEAD_PAD - 2 * QK_ROPE), F32)], axis=1)
    cc = jnp.arange(MLA_HEADS * HEAD_PAD)
    src = jnp.arange(HEAD_PAD)
    pk = ((cc[None, :] % HEAD_PAD) - QK_NOPE == src[:, None]) & (src[:, None] < QK_ROPE)
    grp = jnp.arange(HY_WIDTH) // (HY_WIDTH // HY_GROUPS)
    gsum = (grp[:, None] == grp[None, :]).astype(BF16)
    t = jnp.linspace(0.0, 1.0, L, dtype=F32)[:, None]
    omega = (two_pi / L) * jnp.arange(L, dtype=F32)
    bands = jnp.linspace(1e-4, HY_BANDS - 1, HY_BANDS, dtype=F32)
    phase = omega[:, None] * bands[None, :]
    z = jnp.concatenate([t, jnp.cos(phase), -jnp.sin(phase), jnp.zeros((L, HY_EMB_PAD - HY_EMB), F32)], axis=-1)
    ztab = jnp.concatenate([z, z[:1], z[:0:-1]], axis=0)
    return dict(f1d=f1d, f1f=f1f, gre=gre, gim=gim, m1=m1.astype(BF16), m2=m2.astype(BF16), rcos_t=rcos.T,
                rsin_t=rsin.T, kcs=kcs, pk=pk.astype(BF16), gsum=gsum, ztab=ztab)


def _rot_half_cols(w):
    half = QK_ROPE // 2
    return jnp.concatenate([-w[..., half:], w[..., :half]], axis=-1)


def _layer_weights(i, p, tabs):
    D = p["w_in"].shape[1]
    H = MLA_HEADS
    hw3 = 3 * HY_WIDTH
    w_in = p["w_in"][i]
    kpe = w_in[:, hw3 + Q_RANK + KV_RANK:]
    win = jnp.concatenate([w_in, _rot_half_cols(kpe),
                           jnp.zeros((D, HEAD_PAD - 2 * QK_ROPE), F32)], axis=1).astype(BF16)
    dq = QK_NOPE + QK_ROPE
    wq = p["mla_w_uq"][i].reshape(Q_RANK, H, dq)
    wqt = jnp.concatenate([wq.reshape(Q_RANK, H * dq),
                           _rot_half_cols(wq[..., QK_NOPE:]).reshape(Q_RANK, H * QK_ROPE)], axis=1).T
    wkv =p["mla_w_ukv"][i].reshape(KV_RANK, H, QK_NOPE + V_HEAD)
    wk = jnp.concatenate([wkv[..., :QK_NOPE], jnp.zeros((KV_RANK, H, HEAD_PAD - QK_NOPE), F32)], axis=2)
    wv = jnp.concatenate([wkv[..., QK_NOPE:], jnp.zeros((KV_RANK, H, V_PAD - V_HEAD), F32)], axis=2)
    row = lambda a: a.reshape(1, -1)
    return dict(
        win=win, g_pre=row(p["norm_mix_pre"][i]), conv_w=p["hy_conv_w"][i], conv_b=row(p["hy_conv_b"][i]),
        q_g=row(p["mla_q_norm"][i]), wqt=wqt.astype(BF16), kv_g=row(p["mla_kv_norm"][i]),
        wkp=jnp.concatenate([wk.reshape(KV_RANK, H * HEAD_PAD).astype(BF16), tabs["pk"]], axis=0),
        wvt=wv.reshape(KV_RANK, H * V_PAD).T.astype(BF16),
        g_hy=row(p["grp_norm_hy"][i]), g_attn=row(p["grp_norm_attn"][i]), wo=p["w_out"][i].astype(BF16),
        g_post=row(p["norm_mix_post"][i]), g_mpre=row(p["norm_mlp_pre"][i]), g_mpost=row(p["norm_mlp_post"][i]),
        wup=p["w_mlp_up"][i].astype(BF16), wdn=p["w_mlp_down"][i].astype(BF16),
    )


def _filter_weights(p):
    depth = p["hy_ffn_w1"].shape[0]
    oc = HY_ORDER * HY_WIDTH
    w1 = jnp.pad(p["hy_ffn_w1"], ((0, 0), (0, HY_EMB_PAD - HY_EMB), (0, 0)))
    w3 = p["hy_ffn_w3"].reshape(depth, HY_FFN, HY_ORDER, 2, HY_WIDTH).transpose(0, 3, 1, 2, 4)
    dec = p["hy_decay"].transpose(0, 2, 1, 3).reshape(depth, 2, 1, oc)
    tr = lambda a: jnp.swapaxes(a, 1, 2)
    w3 = w3.reshape(depth, 2, HY_FFN, oc)
    w3_hi = w3.astype(BF16)
    w3_lo = (w3 - w3_hi.astype(F32)).astype(BF16)
    return dict(w1=tr(w1), b1=p["hy_ffn_b1"][:, :, None], sf=tr(p["hy_sin_freq"]), w2=tr(p["hy_ffn_w2"]),
                b2=p["hy_ffn_b2"][:, :, None], w3=jnp.stack([w3_hi, w3_lo], axis=2), dec=dec)


def kernel(x_prompt, x_sample, w_in, hy_conv_w, hy_conv_b, hy_ffn_w1, hy_ffn_b1, hy_ffn_w2, hy_ffn_b2,
           hy_ffn_w3, hy_sin_freq, hy_decay, hy_bias, mla_q_norm, mla_w_uq, mla_kv_norm, mla_w_ukv,
           grp_norm_hy, grp_norm_attn, w_out, norm_mix_pre, norm_mix_post, norm_mlp_pre, norm_mlp_post,
           w_mlp_up, w_mlp_down):
    p = dict(w_in=w_in, hy_conv_w=hy_conv_w, hy_conv_b=hy_conv_b, hy_ffn_w1=hy_ffn_w1, hy_ffn_b1=hy_ffn_b1,
             hy_ffn_w2=hy_ffn_w2, hy_ffn_b2=hy_ffn_b2, hy_ffn_w3=hy_ffn_w3, hy_sin_freq=hy_sin_freq,
             hy_decay=hy_decay, hy_bias=hy_bias, mla_q_norm=mla_q_norm, mla_w_uq=mla_w_uq,
             mla_kv_norm=mla_kv_norm, mla_w_ukv=mla_w_ukv, grp_norm_hy=grp_norm_hy,
             grp_norm_attn=grp_norm_attn, w_out=w_out, norm_mix_pre=norm_mix_pre, norm_mix_post=norm_mix_post,
             norm_mlp_pre=norm_mlp_pre, norm_mlp_post=norm_mlp_post, w_mlp_up=w_mlp_up, w_mlp_down=w_mlp_down)
    bp, L, D = x_prompt.shape
    bs, Ls, _ = x_sample.shape
    assert L == Ls and L % (DFT_N2 * SUBLANES) == 0
    nb = bp + bs
    depth = w_in.shape[0]

    tabs = _tables(L)
    kc = _filters(tabs["ztab"], _filter_weights(p), L)
    kf = _filter_spectrum(_dft1(kc, tabs["f1f"]), tabs["m1"])

    x = jnp.concatenate([x_prompt.reshape(bp * L, D), x_sample.reshape(bs * L, D)], axis=0)
    for i in range(depth):
        lw = _layer_weights(i, p, tabs)
        v, x1, x2, qt, k, vt, stats = _inproj(x, lw, tabs, nb, L)
        z = _hyena_convs(v, x1, x2, kf, hy_bias[i], tabs, i)
        an = _attention(_fast_flags(stats), qt, k, vt, lw["g_attn"], nb, L).reshape(nb * L, ATTN_WIDTH)
        x = _mix_mlp(x, z, an, lw, tabs, L)
    return (x[:bp * L].reshape(bp, L, D), x[bp * L:].reshape(bs, L, D))
```

```python
import functools
import math

import jax
import jax.numpy as jnp
from jax import lax
from jax.experimental import pallas as pl
from jax.experimental.pallas import tpu as pltpu

F32 = jnp.float32
BF16 = jnp.bfloat16

EPS = 1e-6
HY_WIDTH = 512
HY_GROUPS = 8
HY_ORDER = 2
HY_BANDS = 16
HY_EMB = 2 * HY_BANDS + 1
HY_EMB_PAD = 40
HY_FFN = 64
MLA_HEADS = 8
QK_NOPE = 64
QK_ROPE = 32
V_HEAD = 64
Q_RANK = 256
KV_RANK = 128
ROPE_BASE = 10000.0
HEAD_PAD = 128
V_PAD = 128
ATTN_WIDTH = MLA_HEADS * V_HEAD

LANES = 128
SUBLANES = 8
HY_CB = HY_WIDTH // LANES
DFT_N2 = 128
DFT_G = DFT_N2 // SUBLANES
MID_CB = 2
OUTER_GB = 2
HALO = 16
VMEM_LIMIT = 56 * 1024 * 1024
ATTN_UNROLL = 32
STAT_ROWS = 24

FAST_S_MAX = 64.0
FAST_V_MAX = 2.0 ** 30


def _cparams(sem):
    return pltpu.CompilerParams(dimension_semantics=sem, vmem_limit_bytes=VMEM_LIMIT)


def _dot(a, b):
    return jnp.dot(a, b, preferred_element_type=F32)


def _dot_nt(a, b):
    return lax.dot_general(a, b, (((1,), (1,)), ((), ())), preferred_element_type=F32)


def _rms(x, g):
    return x * lax.rsqrt(jnp.mean(x * x, axis=-1, keepdims=True) + EPS) * g


def _store_grouped(out_ref, val, lane0, ncb):
    for j in range(ncb):
        for nl in range(val.shape[0] // DFT_N2):
            for g in range(DFT_G):
                r0 = nl * DFT_N2 + g * SUBLANES
                out_ref[j, g, nl * SUBLANES:(nl + 1) * SUBLANES, :] = (
                    val[r0:r0 + SUBLANES, lane0 + j * LANES:lane0 + (j + 1) * LANES])


def _fold_lanes(row):
    parts = [row[:, j * LANES:(j + 1) * LANES] for j in range(row.shape[1] // LANES)]
    return functools.reduce(jnp.maximum, parts)


def _inproj_kernel(xp_ref, x_ref, xn_ref, gpre_ref, win_ref, cw_ref, cb_ref, qg_ref, wqt_ref,
                   kvg_ref, wkp_ref, wvt_ref, qcos_ref, qsin_ref, kcs_ref,
                   v_out, x1_out, x2_out, qt_out, k_out, vt_out, st_out, h_scr, pe_scr,
                   *, tm, tiles_per_seq, scale):
    i = pl.program_id(0)
    t_idx = i % tiles_per_seq
    g = gpre_ref[...]
    h_scr[0:HALO, :] = _rms(xp_ref[...], g).astype(BF16)
    h_scr[HALO:HALO + tm, :] = _rms(x_ref[...], g).astype(BF16)
    h_scr[HALO + tm:2 * HALO + tm, :] = _rms(xn_ref[...], g).astype(BF16)
    proj = _dot(h_scr[...], win_ref[...])

    hw3 = 3 * HY_WIDTH
    pe_scr[...] = proj[:, :hw3]
    row = lax.broadcasted_iota(jnp.int32, (tm, 1), 0)
    prev = pe_scr[HALO - 1:HALO - 1 + tm, :]
    cur = pe_scr[HALO:HALO + tm, :]
    nxt = pe_scr[HALO + 1:HALO + 1 + tm, :]
    prev = jnp.where(jnp.logical_and(row == 0, t_idx == 0), 0.0, prev)
    nxt = jnp.where(jnp.logical_and(row == tm - 1, t_idx == tiles_per_seq - 1), 0.0, nxt)
    u = prev * cw_ref[0:1, :] + cur * cw_ref[1:2, :] + nxt * cw_ref[2:3, :] + cb_ref[...]
    for o, out in enumerate((v_out, x1_out, x2_out)):
        _store_grouped(out, u, o * HY_WIDTH, HY_CB)

    core = proj[HALO:HALO + tm, :]
    cq = core[:, hw3:hw3 + Q_RANK]
    cqn = _rms(cq, qg_ref[...]).astype(BF16)
    dq = QK_NOPE + QK_ROPE
    qall = _dot_nt(wqt_ref[...], cqn)
    rcos = qcos_ref[...]
    rsin = qsin_ref[...]
    zpad = jnp.zeros((HEAD_PAD - dq, tm), F32)
    for h in range(MLA_HEADS):
        rot = qall[MLA_HEADS * dq + h * QK_ROPE:MLA_HEADS * dq + (h + 1) * QK_ROPE]
        rope = qall[h * dq + QK_NOPE:(h + 1) * dq] * rcos + rot * rsin
        qh = (jnp.concatenate([qall[h * dq:h * dq + QK_NOPE], rope, zpad], axis=0) * scale).astype(BF16)
        qt_out[h] = qh
        qf = qh.astype(F32)
        st_out[h:h + 1, :] = _fold_lanes(jnp.sum(qf * qf, axis=0, keepdims=True))

    ckv = core[:, hw3 + Q_RANK:hw3 + Q_RANK + KV_RANK]
    ckvn = _rms(ckv, kvg_ref[...]).astype(BF16)
    kp = core[:, hw3 + Q_RANK + KV_RANK:] * kcs_ref[...]
    kp = kp + pltpu.roll(kp, HEAD_PAD - QK_ROPE, axis=1)
    kk = _dot(jnp.concatenate([ckvn, kp.astype(BF16)], axis=1), wkp_ref[...])
    for h in range(MLA_HEADS):
        kh = kk[:, h * HEAD_PAD:(h + 1) * HEAD_PAD].astype(BF16)
        k_out[h] = kh
        kf = kh.astype(F32)
        kn = jnp.max(jnp.sum(kf * kf, axis=1, keepdims=True), axis=0, keepdims=True)
        st_out[MLA_HEADS + h:MLA_HEADS + h + 1, :] = jnp.broadcast_to(kn, (1, LANES))
    vt = _dot_nt(wvt_ref[...], ckvn)
    st_out[2 * MLA_HEADS:2 * MLA_HEADS + 1, :] = _fold_lanes(jnp.max(jnp.abs(vt), axis=0, keepdims=True))
    st_out[2 * MLA_HEADS + 1:, :] = jnp.zeros((STAT_ROWS - 2 * MLA_HEADS - 1, LANES), F32)
    frow = lax.broadcasted_iota(jnp.int32, (MLA_HEADS * V_PAD, 1), 0)
    vt = vt + jnp.where(frow % V_PAD == V_HEAD, 1.0, 0.0)
    for h in range(MLA_HEADS):
        vt_out[h] = vt[h * V_PAD:(h + 1) * V_PAD, :].astype(BF16)


def _inproj(x, lw, tabs, nb, L):
    M, D = x.shape
    tm = min(512, L)
    tps = L // tm
    nt = M // tm
    hb = tm // HALO
    nhb = M // HALO
    H = MLA_HEADS
    wcols = lw["win"].shape[1]
    const = lambda *shape: pl.BlockSpec(shape, lambda i: (0,) * len(shape))
    tab_t = pl.BlockSpec((QK_ROPE, tm), lambda i: (0, i % tps))
    in_specs = [
        pl.BlockSpec((HALO, D), lambda i: (jnp.maximum(i * hb - 1, 0), 0)),
        pl.BlockSpec((tm, D), lambda i: (i, 0)),
        pl.BlockSpec((HALO, D), lambda i: (jnp.minimum((i + 1) * hb, nhb - 1), 0)),
        const(1, D), const(D, wcols), const(3, 3 * HY_WIDTH), const(1, 3 * HY_WIDTH),
        const(1, Q_RANK), const(H * HEAD_PAD, Q_RANK),
        const(1, KV_RANK), const(KV_RANK + HEAD_PAD, H * HEAD_PAD),
        const(H * V_PAD, KV_RANK),
        tab_t, tab_t, pl.BlockSpec((tm, HEAD_PAD), lambda i: (i % tps, 0)),
    ]
    hy_spec = pl.BlockSpec((None, HY_CB, DFT_G, tm // DFT_G, LANES), lambda i: (i // tps, 0, 0, i % tps, 0))
    out_specs = [
        hy_spec, hy_spec, hy_spec,
        pl.BlockSpec((None, H, HEAD_PAD, tm), lambda i: (i // tps, 0, 0, i % tps)),
        pl.BlockSpec((None, H, tm, HEAD_PAD), lambda i: (i // tps, 0, i % tps, 0)),
        pl.BlockSpec((None, H, None, V_PAD, tm), lambda i: (i // tps, 0, i % tps, 0, 0)),
        pl.BlockSpec((None, None, STAT_ROWS, LANES), lambda i: (i // tps, i % tps, 0, 0)),
    ]
    hy_shape = jax.ShapeDtypeStruct((nb, HY_CB, DFT_G, L // DFT_G, LANES), F32)
    out_shape = [
        hy_shape, hy_shape, hy_shape,
        jax.ShapeDtypeStruct((nb, H, HEAD_PAD, L), BF16),
        jax.ShapeDtypeStruct((nb, H, L, HEAD_PAD), BF16),
        jax.ShapeDtypeStruct((nb, H, tps, V_PAD, tm), BF16),
        jax.ShapeDtypeStruct((nb, tps, STAT_ROWS, LANES), F32),
    ]
    scale = float((QK_NOPE + QK_ROPE) ** -0.5 * math.log2(math.e))
    return pl.pallas_call(
        functools.partial(_inproj_kernel, tm=tm, tiles_per_seq=tps, scale=scale),
        grid=(nt,), in_specs=in_specs, out_specs=out_specs, out_shape=out_shape,
        scratch_shapes=[pltpu.VMEM((tm + 2 * HALO, D), BF16),
                        pltpu.VMEM((tm + 2 * HALO, 3 * HY_WIDTH), F32)],
        compiler_params=_cparams(("parallel",)), name="inproj",
    )(x, x, x, lw["g_pre"], lw["win"], lw["conv_w"], lw["conv_b"], lw["q_g"], lw["wqt"],
      lw["kv_g"], lw["wkp"], lw["wvt"], tabs["rcos_t"], tabs["rsin_t"], tabs["kcs"])


def _filter_kernel(t_ref, zt_ref, w1_ref, b1_ref, sf_ref, w2_ref, b2_ref, w3_ref, dec_ref, out_ref, *, rb, L):
    hi = lax.Precision.HIGHEST
    r = pl.program_id(1)
    h = jnp.sin(sf_ref[:, 0:1] * (jnp.dot(w1_ref[...], zt_ref[...], precision=hi, preferred_element_type=F32)
                                  + b1_ref[...]))
    h = jnp.sin(sf_ref[:, 1:2] * (jnp.dot(w2_ref[...], h, precision=hi, preferred_element_type=F32) + b2_ref[...]))
    tn = lambda a, b: lax.dot_general(a, b, (((0,), (0,)), ((), ())), preferred_element_type=F32)
    h_hi = h.astype(BF16)
    h_lo = (h - h_hi.astype(F32)).astype(BF16)
    k = tn(h_hi, w3_ref[0]) + (tn(h_hi, w3_ref[1]) + tn(h_lo, w3_ref[0]))
    k = k * jnp.exp(-t_ref[...] * jnp.abs(dec_ref[...]))
    row = r * rb + lax.broadcasted_iota(jnp.int32, (rb, 1), 0)
    k = jnp.where(row == L, 0.0, k)
    _store_grouped(out_ref, k, 0, out_ref.shape[0])


def _filters(ztab, fw, L):
    depth = fw["w1"].shape[0]
    rows = 2 * L
    rb = min(512, L)
    nblk = rows // rb
    half = nblk // 2
    oc = HY_ORDER * HY_WIDTH
    lay = lambda *shape: pl.BlockSpec((None,) + shape, lambda l, r: (l,) + (0,) * len(shape))
    in_specs = [
        pl.BlockSpec((rb, 1), lambda l, r: (r, 0)), pl.BlockSpec((HY_EMB_PAD, rb), lambda l, r: (0, r)),
        lay(HY_FFN, HY_EMB_PAD), lay(HY_FFN, 1), lay(HY_FFN, 2), lay(HY_FFN, HY_FFN), lay(HY_FFN, 1),
        pl.BlockSpec((None, None, 2, HY_FFN, oc), lambda l, r: (l, r // half, 0, 0, 0)),
        pl.BlockSpec((None, None, 1, oc), lambda l, r: (l, r // half, 0, 0)),
    ]
    return pl.pallas_call(
        functools.partial(_filter_kernel, rb=rb, L=L),
        grid=(depth, nblk), in_specs=in_specs,
        out_specs=pl.BlockSpec((None, oc // LANES, DFT_G, rb // DFT_G, LANES), lambda l, r: (l, 0, 0, r, 0)),
        out_shape=jax.ShapeDtypeStruct((depth, oc // LANES, DFT_G, rows // DFT_G, LANES), F32),
        compiler_params=_cparams(("parallel", "parallel")), name="hyena_filter",
    )(ztab[:, 0:1], ztab.T, fw["w1"], fw["b1"], fw["sf"], fw["w2"], fw["b2"], fw["w3"], fw["dec"])


def _rows_of(ref, p, n, lead=()):
    return ref[lead + (pl.ds(p, n, stride=SUBLANES), slice(None))]


def _pack_pair(re, im):
    hi = lax.bitcast_convert_type(re.astype(BF16).astype(F32), jnp.uint32)
    lo = lax.bitcast_convert_type(im.astype(BF16).astype(F32), jnp.uint32)
    return hi | (lo >> 16)


def _unpack_pair(w):
    re = lax.bitcast_convert_type(w & jnp.uint32(0xFFFF0000), F32)
    im = lax.bitcast_convert_type(w << 16, F32)
    return re.astype(BF16), im.astype(BF16)


def _dft1_block(f_ref, xs, out_ref, gg):
    kp = f_ref.shape[0] // 2
    res = _dot(f_ref[...], jnp.concatenate([x.astype(BF16) for x in xs], axis=1))
    w = _pack_pair(res[:kp], res[kp:])
    for kg in range(kp // SUBLANES):
        for p in range(SUBLANES):
            r0 = (gg * SUBLANES + p) * SUBLANES
            out_ref[kg, r0:r0 + SUBLANES, :] = w[kg * SUBLANES:(kg + 1) * SUBLANES, p * LANES:(p + 1) * LANES]


def _dft1_kernel(f_ref, x_ref, out_ref):
    R = f_ref.shape[1]
    for gg in range(OUTER_GB):
        _dft1_block(f_ref, [_rows_of(x_ref, p, R, (gg,)) for p in range(SUBLANES)], out_ref, gg)


def _dft1_spec(kg):
    return pl.BlockSpec((None, None, kg, OUTER_GB * SUBLANES * SUBLANES, LANES), lambda b, c, g: (b, c, 0, g, 0))


def _dft1(x, f1):
    nb, ncb, _, rows, _ = x.shape
    kp = f1.shape[0] // 2
    kg = kp // SUBLANES
    return pl.pallas_call(
        _dft1_kernel, grid=(nb, ncb, DFT_G // OUTER_GB),
        in_specs=[pl.BlockSpec(f1.shape, lambda b, c, g: (0, 0)),
                  pl.BlockSpec((None, None, OUTER_GB, rows, LANES), lambda b, c, g: (b, c, g, 0, 0))],
        out_specs=_dft1_spec(kg),
        out_shape=jax.ShapeDtypeStruct((nb, ncb, kg, DFT_N2 * SUBLANES, LANES), jnp.uint32),
        compiler_params=_cparams(("parallel", "parallel", "parallel")), name="dft_outer",
    )(f1, x)


def _inner_fwd(m1_ref, a_ref, kk):
    w = jnp.concatenate([_rows_of(a_ref, kk, DFT_N2, (c,)) for c in range(a_ref.shape[0])], axis=1)
    ar, ai = _unpack_pair(w)
    return _dot(m1_ref[kk], jnp.concatenate([ar, ai], axis=0))


def _spec_kernel(m1_ref, a_ref, out_ref):
    for kk in range(SUBLANES):
        x = _inner_fwd(m1_ref, a_ref, kk)
        for c in range(a_ref.shape[0]):
            out_ref[c, kk, 0] = x[:DFT_N2, c * LANES:(c + 1) * LANES]
            out_ref[c, kk, 1] = x[DFT_N2:, c * LANES:(c + 1) * LANES]


def _filter_spectrum(a, m1):
    depth, ncb, kg, rows, _ = a.shape
    n2 = DFT_N2
    return pl.pallas_call(
        _spec_kernel, grid=(kg, depth, ncb // MID_CB),
        in_specs=[pl.BlockSpec((SUBLANES, 2 * n2, 2 * n2), lambda k, l, c: (k, 0, 0)),
                  pl.BlockSpec((None, MID_CB, None, rows, LANES), lambda k, l, c: (l, c, k, 0, 0))],
        out_specs=pl.BlockSpec((None, MID_CB, SUBLANES, 2, n2, LANES), lambda k, l, c: (l, c, k, 0, 0, 0)),
        out_shape=jax.ShapeDtypeStruct((depth, ncb, kg * SUBLANES, 2, n2, LANES), F32),
        compiler_params=_cparams(("parallel", "parallel", "parallel")), name="filter_spectrum",
    )(m1, a)


def _mid_kernel(m1_ref, m2_ref, kf_ref, a_ref, out_ref):
    ncb = a_ref.shape[0]
    lanes = lambda parts: jnp.concatenate(parts, axis=1)
    for kk in range(SUBLANES):
        x = _inner_fwd(m1_ref, a_ref, kk)
        xr, xi = x[:DFT_N2], x[DFT_N2:]
        kr = lanes([kf_ref[c, kk, 0] for c in range(ncb)])
        ki = lanes([kf_ref[c, kk, 1] for c in range(ncb)])
        yr = (xr * kr - xi * ki).astype(BF16)
        yi = (xr * ki + xi * kr).astype(BF16)
        y = _dot(m2_ref[kk], jnp.concatenate([yr, yi], axis=0))
        w = _pack_pair(y[:DFT_N2], y[DFT_N2:])
        for c in range(ncb):
            for g in range(DFT_G):
                out_ref[c, g, kk * SUBLANES:(kk + 1) * SUBLANES, :] = (
                    w[g * SUBLANES:(g + 1) * SUBLANES, c * LANES:(c + 1) * LANES])


def _conv_mid(a, kf, m1, m2, layer, order):
    nb, ncb, kg, rows, _ = a.shape
    n2 = DFT_N2
    mspec = pl.BlockSpec((SUBLANES, 2 * n2, 2 * n2), lambda k, c, b: (k, 0, 0))
    return pl.pallas_call(
        _mid_kernel, grid=(kg, ncb // MID_CB, nb),
        in_specs=[mspec, mspec,
                  pl.BlockSpec((None, MID_CB, SUBLANES, 2, n2, LANES),
                               lambda k, c, b: (layer, order * (ncb // MID_CB) + c, k, 0, 0, 0)),
                  pl.BlockSpec((None, MID_CB, None, rows, LANES), lambda k, c, b: (b, c, k, 0, 0))],
        out_specs=pl.BlockSpec((None, MID_CB, DFT_G, SUBLANES * SUBLANES, LANES), lambda k, c, b: (b, c, 0, k, 0)),
        out_shape=jax.ShapeDtypeStruct((nb, ncb, DFT_G, kg * SUBLANES * SUBLANES, LANES), jnp.uint32),
        compiler_params=_cparams(("parallel", "parallel", "arbitrary")), name="conv_mid",
    )(m1, m2, kf, a)


def _gate_kernel(gre_ref, gim_ref, f_ref, b_ref, z_ref, gate_ref, bias_ref, out_ref, *next_ref):
    R, kp = gre_ref.shape
    bias = bias_ref[...]
    for gg in range(OUTER_GB):
        bre, bim = _unpack_pair(jnp.concatenate([_rows_of(b_ref, p, kp, (gg,)) for p in range(SUBLANES)], axis=1))
        y = _dot(gre_ref[...], bre) + _dot(gim_ref[...], bim)
        vals = []
        for p in range(SUBLANES):
            yp = y[:, p * LANES:(p + 1) * LANES]
            vals.append(_rows_of(gate_ref, p, R, (gg,)) * (yp + _rows_of(z_ref, p, R, (gg,)) * bias))
            out_ref[gg, pl.ds(p, R, stride=SUBLANES), :] = vals[p]
        if next_ref:
            _dft1_block(f_ref, vals, next_ref[0], gg)


def _conv_out(bsp, tabs, z, gate, bias, with_next):
    nb, ncb, _, krows, _ = bsp.shape
    rows = z.shape[3]
    gre, gim, f1 = tabs["gre"], tabs["gim"], tabs["f1d"]
    kg = f1.shape[0] // 2 // SUBLANES
    full = lambda t: pl.BlockSpec(t.shape, lambda b, c, g: (0, 0))
    tile = pl.BlockSpec((None, None, OUTER_GB, rows, LANES), lambda b, c, g: (b, c, g, 0, 0))
    out_specs, out_shape = [tile], [jax.ShapeDtypeStruct(z.shape, F32)]
    if with_next:
        out_specs.append(_dft1_spec(kg))
        out_shape.append(jax.ShapeDtypeStruct((nb, ncb, kg, DFT_N2 * SUBLANES, LANES), jnp.uint32))
    return pl.pallas_call(
        _gate_kernel, grid=(nb, ncb, DFT_G // OUTER_GB),
        in_specs=[full(gre), full(gim), full(f1),
                  pl.BlockSpec((None, None, OUTER_GB, krows, LANES), lambda b, c, g: (b, c, g, 0, 0)),
                  tile, tile, pl.BlockSpec((None, 1, LANES), lambda b, c, g: (c, 0, 0))],
        out_specs=out_specs, out_shape=out_shape,
        compiler_params=_cparams(("parallel", "parallel", "parallel")), name="conv_gate",
    )(gre, gim, f1, bsp, z, gate, bias)


def _hyena_convs(v, x1, x2, kf, bias, tabs, layer):
    b0, b1 = bias[0].reshape(HY_CB, 1, LANES), bias[1].reshape(HY_CB, 1, LANES)
    bsp = _conv_mid(_dft1(v, tabs["f1d"]), kf, tabs["m1"], tabs["m2"], layer, 0)
    z1, a2 = _conv_out(bsp, tabs, v, x1, b0, True)
    bsp = _conv_mid(a2, kf, tabs["m1"], tabs["m2"], layer, 1)
    return _conv_out(bsp, tabs, z1, x2, b1, False)[0]


def _attn_kernel(flag_ref, qt_ref, k_ref, vt_ref, g_ref, o_ref, acc_scr, m_scr, s_scr, *, nk, tk, unroll):
    pair = pl.program_id(0) * pl.num_programs(1) + pl.program_id(1)
    fast = jnp.logical_and(flag_ref[2 * pair] == 1, flag_ref[2 * pair + 1] == 1)
    acc_scr[...] = jnp.zeros(acc_scr.shape, F32)

    @pl.when(fast)
    def _():
        last = 2 * nk - 1

        def scores(cc):
            hd = cc // nk
            c = cc - hd * nk
            return _dot(k_ref[hd, pl.ds(pl.multiple_of(c * tk, tk), tk), :], qt_ref[hd])

        s_scr[0] = scores(0)

        def body(j, carry):
            hd = (j * unroll) // nk
            c0 = j * unroll - hd * nk
            pv = None
            for u in range(unroll):
                s_scr[(u + 1) % 2] = scores(jnp.minimum(j * unroll + u + 1, last))
                pt = jnp.exp2(s_scr[u % 2]).astype(BF16)
                d = _dot(vt_ref[hd, c0 + u], pt)
                pv = d if pv is None else pv + d
            acc_scr[hd] += pv
            return carry
        lax.fori_loop(0, 2 * nk // unroll, body, 0)

    @pl.when(jnp.logical_not(fast))
    def _():
        for hh in range(2):
            qt = qt_ref[hh]
            m_scr[...] = jnp.full(m_scr.shape, -jnp.inf, F32)

            def body(j, carry, hh=hh, qt=qt):
                kk = k_ref[hh, pl.ds(pl.multiple_of(j * tk, tk), tk), :]
                s = _dot(kk, qt)
                m_prev = m_scr[...]
                m_new = jnp.maximum(m_prev, jnp.max(s, axis=0, keepdims=True))
                pt = jnp.exp2(s - m_new).astype(BF16)
                acc_scr[hh] = jnp.exp2(m_prev - m_new) * acc_scr[hh] + _dot(vt_ref[hh, j], pt)
                m_scr[...] = m_new
                return carry
            lax.fori_loop(0, nk, body, 0)

    outs = []
    for hh in range(2):
        acc = acc_scr[hh]
        o = acc[:V_HEAD] / acc[V_HEAD:V_HEAD + 1]
        ms = jnp.mean(o * o, axis=0, keepdims=True)
        outs.append(o * lax.rsqrt(ms + EPS))
    ot = jnp.concatenate(outs, axis=0)
    o_ref[...] = (ot.T * g_ref[...]).astype(o_ref.dtype)


def _attention(flags, qt, k, vt, g_attn, nb, L):
    H = MLA_HEADS
    tq = min(512, L)
    nk, tk = vt.shape[2], vt.shape[4]
    grid_spec = pltpu.PrefetchScalarGridSpec(
        num_scalar_prefetch=1, grid=(nb, H // 2, L // tq),
        in_specs=[pl.BlockSpec((None, 2, HEAD_PAD, tq), lambda b, h, i, f: (b, h, 0, i)),
                  pl.BlockSpec((None, 2, L, HEAD_PAD), lambda b, h, i, f: (b, h, 0, 0)),
                  pl.BlockSpec((None, 2, nk, V_PAD, tk), lambda b, h, i, f: (b, h, 0, 0, 0)),
                  pl.BlockSpec((1, 2 * V_HEAD), lambda b, h, i, f: (0, h))],
        out_specs=pl.BlockSpec((None, tq, 2 * V_HEAD), lambda b, h, i, f: (b, i, h)),
        scratch_shapes=[pltpu.VMEM((2, V_PAD, tq), F32), pltpu.VMEM((1, tq), F32), pltpu.VMEM((2, tk, tq), F32)])
    unroll = math.gcd(nk, ATTN_UNROLL)
    assert unroll % 2 == 0
    return pl.pallas_call(
        functools.partial(_attn_kernel, nk=nk, tk=tk, unroll=unroll), grid_spec=grid_spec,
        out_shape=jax.ShapeDtypeStruct((nb, L, ATTN_WIDTH), BF16),
        compiler_params=_cparams(("parallel", "parallel", "arbitrary")), name="attention",
    )(flags, qt, k, vt, g_attn)


def _fast_flags(stats):
    H = MLA_HEADS
    qn = jnp.max(stats[:, :, 0:H, :], axis=(1, 3))
    kn = jnp.max(stats[:, :, H:2 * H, :], axis=(1, 3))
    vm = jnp.max(stats[:, :, 2 * H, :], axis=(1, 2))
    ok = jnp.logical_and(qn * kn <= FAST_S_MAX * FAST_S_MAX, (vm <= FAST_V_MAX)[:, None])
    return ok.astype(jnp.int32).reshape(-1)


def _mix_mlp_kernel(x_ref, zh_ref, an_ref, ghy_ref, gsum_ref, wo_ref, gpost_ref, gmpre_ref, wup_ref,
                    wdn_ref, gmpost_ref, out_ref):
    zh = jnp.concatenate(
        [jnp.concatenate([zh_ref[j, g, nl * SUBLANES:(nl + 1) * SUBLANES, :] for j in range(HY_CB)], axis=1)
         for nl in range(zh_ref.shape[2] // SUBLANES) for g in range(DFT_G)], axis=0)
    ms = _dot((zh * zh).astype(BF16), gsum_ref[...]) * (HY_GROUPS / HY_WIDTH)
    hn = (zh * lax.rsqrt(ms + EPS) * ghy_ref[...]).astype(BF16)
    mix = _dot(hn, wo_ref[0:HY_WIDTH, :]) + _dot(an_ref[...], wo_ref[HY_WIDTH:, :])
    x = x_ref[...] + _rms(mix, gpost_ref[...])
    h = _rms(x, gmpre_ref[...]).astype(BF16)
    up = jnp.maximum(_dot(h, wup_ref[...]), 0.0)
    m = _dot((up * up).astype(BF16), wdn_ref[...])
    out_ref[...] = x + _rms(m, gmpost_ref[...])


def _mix_mlp(x, zh, an, lw, tabs, L):
    M, D = x.shape
    tm = min(512, L)
    tps = L // tm
    dff = lw["wup"].shape[1]
    mw = HY_WIDTH + ATTN_WIDTH
    const = lambda *shape: pl.BlockSpec(shape, lambda i: (0,) * len(shape), pipeline_mode=pl.Buffered(1))
    rows = lambda w: pl.BlockSpec((tm, w), lambda i: (i, 0))
    return pl.pallas_call(
        _mix_mlp_kernel, grid=(M // tm,),
        in_specs=[rows(D), pl.BlockSpec((None, HY_CB, DFT_G, tm // DFT_G, LANES),
                                        lambda i: (i // tps, 0, 0, i % tps, 0)),
                  rows(ATTN_WIDTH), const(1, HY_WIDTH), const(HY_WIDTH, HY_WIDTH),
                  const(mw, D), const(1, D), const(1, D), const(D, dff), const(dff, D), const(1, D)],
        out_specs=rows(D), out_shape=jax.ShapeDtypeStruct((M, D), F32),
        compiler_params=_cparams(("parallel",)), name="mix_mlp",
    )(x, zh, an, lw["g_hy"], tabs["gsum"], lw["wo"], lw["g_post"], lw["g_mpre"], lw["wup"], lw["wdn"],
      lw["g_mpost"])


def _tables(L):
    n = 2 * L
    n2 = DFT_N2
    n1 = n // n2
    nh = n1 // 2
    kp = -(-(nh + 1) // SUBLANES) * SUBLANES
    two_pi = 2.0 * math.pi

    k1 = jnp.arange(kp, dtype=jnp.int32)
    valid = (k1 <= nh)
    def outer(ncols):
        nn = jnp.arange(ncols, dtype=jnp.int32)
        ang = ((k1[:, None] * nn[None, :]) % n1).astype(F32) * (two_pi / n1)
        c = jnp.where(valid[:, None], jnp.cos(ang), 0.0)
        s = jnp.where(valid[:, None], -jnp.sin(ang), 0.0)
        return jnp.concatenate([c, s], axis=0).astype(BF16)
    f1d = outer(nh)
    f1f = outer(n1)
    nn = jnp.arange(nh, dtype=jnp.int32)
    ang = ((nn[:, None] * k1[None, :]) % n1).astype(F32) * (two_pi / n1)
    wgt = jnp.where(valid, jnp.where((k1 == 0) | (k1 == nh), 1.0, 2.0), 0.0) / n
    gre = (jnp.cos(ang) * wgt[None, :]).astype(BF16)
    gim = (-jnp.sin(ang) * wgt[None, :]).astype(BF16)
    a2 = jnp.arange(n2, dtype=jnp.int32)
    idx = (a2[None, :, None] * a2[None, None, :] * n1 + a2[None, None, :] * k1[:, None, None]) % n
    ph = idx.astype(F32) * (two_pi / n)
    gr, gi = jnp.cos(ph), -jnp.sin(ph)
    m1 = jnp.concatenate([jnp.concatenate([gr, -gi], axis=2), jnp.concatenate([gi, gr], axis=2)], axis=1)
    m2 = jnp.swapaxes(m1, 1, 2)
    inv = 1.0 / (ROPE_BASE ** (jnp.arange(0, QK_ROPE, 2, dtype=F32) / QK_ROPE))
    ang = jnp.arange(L, dtype=F32)[:, None] * inv[None, :]
    cos, sin = jnp.cos(ang), jnp.sin(ang)
    rcos = jnp.concatenate([cos, cos], axis=1)
    rsin = jnp.concatenate([sin, sin], axis=1)
    kcs =jnp.concatenate([cos, cos, sin, sin, jnp.zeros((L, HEAD_PAD - 2 * QK_ROPE), F32)], axis=1)
    cc = jnp.arange(MLA_HEADS * HEAD_PAD)
    src = jnp.arange(HEAD_PAD)
    pk = ((cc[None, :] % HEAD_PAD) - QK_NOPE == src[:, None]) & (src[:, None] < QK_ROPE)
    grp = jnp.arange(HY_WIDTH) // (HY_WIDTH // HY_GROUPS)
    gsum = (grp[:, None] == grp[None, :]).astype(BF16)
    t = jnp.linspace(0.0, 1.0, L, dtype=F32)[:, None]
    omega = (two_pi / L) * jnp.arange(L, dtype=F32)
    bands = jnp.linspace(1e-4, HY_BANDS - 1, HY_BANDS, dtype=F32)
    phase = omega[:, None] * bands[None, :]
    z = jnp.concatenate([t, jnp.cos(phase), -jnp.sin(phase), jnp.zeros((L, HY_EMB_PAD - HY_EMB), F32)], axis=-1)
    ztab = jnp.concatenate([z, z[:1], z[:0:-1]], axis=0)
    return dict(f1d=f1d, f1f=f1f, gre=gre, gim=gim, m1=m1.astype(BF16), m2=m2.astype(BF16), rcos_t=rcos.T,
                rsin_t=rsin.T, kcs=kcs, pk=pk.astype(BF16), gsum=gsum, ztab=ztab)


def _rot_half_cols(w):
    half = QK_ROPE // 2
    return jnp.concatenate([-w[..., half:], w[..., :half]], axis=-1)


def _layer_weights(i, p, tabs):
    D = p["w_in"].shape[1]
    H = MLA_HEADS
    hw3 = 3 * HY_WIDTH
    w_in = p["w_in"][i]
    kpe = w_in[:, hw3 + Q_RANK + KV_RANK:]
    win = jnp.concatenate([w_in, _rot_half_cols(kpe),
                           jnp.zeros((D, HEAD_PAD - 2 * QK_ROPE), F32)], axis=1).astype(BF16)
    dq = QK_NOPE + QK_ROPE
    wq = p["mla_w_uq"][i].reshape(Q_RANK, H, dq)
    wqt = jnp.concatenate([wq.reshape(Q_RANK, H * dq),
                           _rot_half_cols(wq[..., QK_NOPE:]).reshape(Q_RANK, H * QK_ROPE)], axis=1).T
    wkv =p["mla_w_ukv"][i].reshape(KV_RANK, H, QK_NOPE + V_HEAD)
    wk = jnp.concatenate([wkv[..., :QK_NOPE], jnp.zeros((KV_RANK, H, HEAD_PAD - QK_NOPE), F32)], axis=2)
    wv = jnp.concatenate([wkv[..., QK_NOPE:], jnp.zeros((KV_RANK, H, V_PAD - V_HEAD), F32)], axis=2)
    row = lambda a: a.reshape(1, -1)
    return dict(
        win=win, g_pre=row(p["norm_mix_pre"][i]), conv_w=p["hy_conv_w"][i], conv_b=row(p["hy_conv_b"][i]),
        q_g=row(p["mla_q_norm"][i]), wqt=wqt.astype(BF16), kv_g=row(p["mla_kv_norm"][i]),
        wkp=jnp.concatenate([wk.reshape(KV_RANK, H * HEAD_PAD).astype(BF16), tabs["pk"]], axis=0),
        wvt=wv.reshape(KV_RANK, H * V_PAD).T.astype(BF16),
        g_hy=row(p["grp_norm_hy"][i]), g_attn=row(p["grp_norm_attn"][i]), wo=p["w_out"][i].astype(BF16),
        g_post=row(p["norm_mix_post"][i]), g_mpre=row(p["norm_mlp_pre"][i]), g_mpost=row(p["norm_mlp_post"][i]),
        wup=p["w_mlp_up"][i].astype(BF16), wdn=p["w_mlp_down"][i].astype(BF16),
    )


def _filter_weights(p):
    depth = p["hy_ffn_w1"].shape[0]
    oc = HY_ORDER * HY_WIDTH
    w1 = jnp.pad(p["hy_ffn_w1"], ((0, 0), (0, HY_EMB_PAD - HY_EMB), (0, 0)))
    w3 = p["hy_ffn_w3"].reshape(depth, HY_FFN, HY_ORDER, 2, HY_WIDTH).transpose(0, 3, 1, 2, 4)
    dec = p["hy_decay"].transpose(0, 2, 1, 3).reshape(depth, 2, 1, oc)
    tr = lambda a: jnp.swapaxes(a, 1, 2)
    w3 = w3.reshape(depth, 2, HY_FFN, oc)
    w3_hi = w3.astype(BF16)
    w3_lo = (w3 - w3_hi.astype(F32)).astype(BF16)
    return dict(w1=tr(w1), b1=p["hy_ffn_b1"][:, :, None], sf=tr(p["hy_sin_freq"]), w2=tr(p["hy_ffn_w2"]),
                b2=p["hy_ffn_b2"][:, :, None], w3=jnp.stack([w3_hi, w3_lo], axis=2), dec=dec)


def kernel(x_prompt, x_sample, w_in, hy_conv_w, hy_conv_b, hy_ffn_w1, hy_ffn_b1, hy_ffn_w2, hy_ffn_b2,
           hy_ffn_w3, hy_sin_freq, hy_decay, hy_bias, mla_q_norm, mla_w_uq, mla_kv_norm, mla_w_ukv,
           grp_norm_hy, grp_norm_attn, w_out, norm_mix_pre, norm_mix_post, norm_mlp_pre, norm_mlp_post,
           w_mlp_up, w_mlp_down):
    p = dict(w_in=w_in, hy_conv_w=hy_conv_w, hy_conv_b=hy_conv_b, hy_ffn_w1=hy_ffn_w1, hy_ffn_b1=hy_ffn_b1,
             hy_ffn_w2=hy_ffn_w2, hy_ffn_b2=hy_ffn_b2, hy_ffn_w3=hy_ffn_w3, hy_sin_freq=hy_sin_freq,
             hy_decay=hy_decay, hy_bias=hy_bias, mla_q_norm=mla_q_norm, mla_w_uq=mla_w_uq,
             mla_kv_norm=mla_kv_norm, mla_w_ukv=mla_w_ukv, grp_norm_hy=grp_norm_hy,
             grp_norm_attn=grp_norm_attn, w_out=w_out, norm_mix_pre=norm_mix_pre, norm_mix_post=norm_mix_post,
             norm_mlp_pre=norm_mlp_pre, norm_mlp_post=norm_mlp_post, w_mlp_up=w_mlp_up, w_mlp_down=w_mlp_down)
    bp, L, D = x_prompt.shape
    bs, Ls, _ = x_sample.shape
    assert L == Ls and L % (DFT_N2 * SUBLANES) == 0
    nb = bp + bs
    depth = w_in.shape[0]

    tabs = _tables(L)
    kc = _filters(tabs["ztab"], _filter_weights(p), L)
    kf = _filter_spectrum(_dft1(kc, tabs["f1f"]), tabs["m1"])

    x = jnp.concatenate([x_prompt.reshape(bp * L, D), x_sample.reshape(bs * L, D)], axis=0)
    for i in range(depth):
        lw = _layer_weights(i, p, tabs)
        v, x1, x2, qt, k, vt, stats = _inproj(x, lw, tabs, nb, L)
        z = _hyena_convs(v, x1, x2, kf, hy_bias[i], tabs, i)
        an = _attention(_fast_flags(stats), qt, k, vt, lw["g_attn"], nb, L).reshape(nb * L, ATTN_WIDTH)
        x = _mix_mlp(x, z, an, lw, tabs, L)
    return (x[:bp * L].reshape(bp, L, D), x[bp * L:].reshape(bs, L, D))
```

```python
import functools
import math

import jax
import jax.numpy as jnp
from jax import lax
from jax.experimental import pallas as pl
from jax.experimental.pallas import tpu as pltpu

F32 = jnp.float32
BF16 = jnp.bfloat16

EPS = 1e-6
HY_WIDTH = 512
HY_GROUPS = 8
HY_ORDER = 2
HY_BANDS = 16
HY_EMB = 2 * HY_BANDS + 1
HY_EMB_PAD = 40
HY_FFN = 64
MLA_HEADS = 8
QK_NOPE = 64
QK_ROPE = 32
V_HEAD = 64
Q_RANK = 256
KV_RANK = 128
ROPE_BASE = 10000.0
HEAD_PAD = 128
V_PAD = 128
ATTN_WIDTH = MLA_HEADS * V_HEAD

LANES = 128
SUBLANES = 8
HY_CB = HY_WIDTH // LANES
DFT_N2 = 128
DFT_G = DFT_N2 // SUBLANES
MID_CB = 2
OUTER_GB = 4
HALO = 16
VMEM_LIMIT = 56 * 1024 * 1024
ATTN_UNROLL = 32
STAT_ROWS = 24

FAST_S_MAX = 64.0
FAST_V_MAX = 2.0 ** 30


def _cparams(sem):
    return pltpu.CompilerParams(dimension_semantics=sem, vmem_limit_bytes=VMEM_LIMIT)


def _dot(a, b):
    return jnp.dot(a, b, preferred_element_type=F32)


def _dot_nt(a, b):
    return lax.dot_general(a, b, (((1,), (1,)), ((), ())), preferred_element_type=F32)


def _rms(x, g):
    return x * lax.rsqrt(jnp.mean(x * x, axis=-1, keepdims=True) + EPS) * g


def _store_grouped(out_ref, val, lane0, ncb):
    for j in range(ncb):
        for nl in range(val.shape[0] // DFT_N2):
            for g in range(DFT_G):
                r0 = nl * DFT_N2 + g * SUBLANES
                out_ref[j, g, nl * SUBLANES:(nl + 1) * SUBLANES, :] = (
                    val[r0:r0 + SUBLANES, lane0 + j * LANES:lane0 + (j + 1) * LANES])


def _fold_lanes(row):
    parts = [row[:, j * LANES:(j + 1) * LANES] for j in range(row.shape[1] // LANES)]
    return functools.reduce(jnp.maximum, parts)


def _inproj_kernel(xp_ref, x_ref, xn_ref, gpre_ref, win_ref, cw_ref, cb_ref, qg_ref, wqt_ref,
                   kvg_ref, wkp_ref, wvt_ref, qcos_ref, qsin_ref, kcs_ref,
                   v_out, x1_out, x2_out, qt_out, k_out, vt_out, st_out, h_scr, pe_scr,
                   *, tm, tiles_per_seq, scale):
    i = pl.program_id(0)
    t_idx = i % tiles_per_seq
    g = gpre_ref[...]
    h_scr[0:HALO, :] = _rms(xp_ref[...], g).astype(BF16)
    h_scr[HALO:HALO + tm, :] = _rms(x_ref[...], g).astype(BF16)
    h_scr[HALO + tm:2 * HALO + tm, :] = _rms(xn_ref[...], g).astype(BF16)
    hw3 = 3 * HY_WIDTH
    core = _dot(h_scr[HALO:HALO + tm, :], win_ref[:, hw3:])
    pe_scr[...] = _dot(h_scr[...], win_ref[:, 0:hw3])
    row = lax.broadcasted_iota(jnp.int32, (tm, 1), 0)
    prev = pe_scr[HALO - 1:HALO - 1 + tm, :]
    cur = pe_scr[HALO:HALO + tm, :]
    nxt = pe_scr[HALO + 1:HALO + 1 + tm, :]
    prev = jnp.where(jnp.logical_and(row == 0, t_idx == 0), 0.0, prev)
    nxt = jnp.where(jnp.logical_and(row == tm - 1, t_idx == tiles_per_seq - 1), 0.0, nxt)
    u = prev * cw_ref[0:1, :] + cur * cw_ref[1:2, :] + nxt * cw_ref[2:3, :] + cb_ref[...]
    for o, out in enumerate((v_out, x1_out, x2_out)):
        _store_grouped(out, u, o * HY_WIDTH, HY_CB)

    cq = core[:, 0:Q_RANK]
    cqn =_rms(cq, qg_ref[...]).astype(BF16)
    dq = QK_NOPE + QK_ROPE
    qall = _dot_nt(wqt_ref[...], cqn)
    rcos = qcos_ref[...]
    rsin = qsin_ref[...]
    zpad = jnp.zeros((HEAD_PAD - dq, tm), F32)
    for h in range(MLA_HEADS):
        rot = qall[MLA_HEADS * dq + h * QK_ROPE:MLA_HEADS * dq + (h + 1) * QK_ROPE]
        rope = qall[h * dq + QK_NOPE:(h + 1) * dq] * rcos + rot * rsin
        qh = (jnp.concatenate([qall[h * dq:h * dq + QK_NOPE], rope, zpad], axis=0) * scale).astype(BF16)
        qt_out[h] = qh
        qf = qh.astype(F32)
        st_out[h:h + 1, :] = _fold_lanes(jnp.sum(qf * qf, axis=0, keepdims=True))

    ckv = core[:, Q_RANK:Q_RANK + KV_RANK]
    ckvn = _rms(ckv, kvg_ref[...]).astype(BF16)
    kp = core[:, Q_RANK + KV_RANK:] * kcs_ref[...]
    kp = kp + pltpu.roll(kp, HEAD_PAD - QK_ROPE, axis=1)
    kk = _dot(jnp.concatenate([ckvn, kp.astype(BF16)], axis=1), wkp_ref[...])
    for h in range(MLA_HEADS):
        kh = kk[:, h * HEAD_PAD:(h + 1) * HEAD_PAD].astype(BF16)
        k_out[h] = kh
        kf = kh.astype(F32)
        kn = jnp.max(jnp.sum(kf * kf, axis=1, keepdims=True), axis=0, keepdims=True)
        st_out[MLA_HEADS + h:MLA_HEADS + h + 1, :] = jnp.broadcast_to(kn, (1, LANES))
    vt = _dot_nt(wvt_ref[...], ckvn)
    st_out[2 * MLA_HEADS:2 * MLA_HEADS + 1, :] = _fold_lanes(jnp.max(jnp.abs(vt), axis=0, keepdims=True))
    st_out[2 * MLA_HEADS + 1:, :] = jnp.zeros((STAT_ROWS - 2 * MLA_HEADS - 1, LANES), F32)
    frow = lax.broadcasted_iota(jnp.int32, (MLA_HEADS * V_PAD, 1), 0)
    vt = vt + jnp.where(frow % V_PAD == V_HEAD, 1.0, 0.0)
    for h in range(MLA_HEADS):
        vt_out[h] = vt[h * V_PAD:(h + 1) * V_PAD, :].astype(BF16)


def _inproj(x, lw, tabs, nb, L):
    M, D = x.shape
    tm = min(512, L)
    tps = L // tm
    nt = M // tm
    hb = tm // HALO
    nhb = M // HALO
    H = MLA_HEADS
    wcols = lw["win"].shape[1]
    const = lambda *shape: pl.BlockSpec(shape, lambda i: (0,) * len(shape))
    tab_t = pl.BlockSpec((QK_ROPE, tm), lambda i: (0, i % tps))
    in_specs = [
        pl.BlockSpec((HALO, D), lambda i: (jnp.maximum(i * hb - 1, 0), 0)),
        pl.BlockSpec((tm, D), lambda i: (i, 0)),
        pl.BlockSpec((HALO, D), lambda i: (jnp.minimum((i + 1) * hb, nhb - 1), 0)),
        const(1, D), const(D, wcols), const(3, 3 * HY_WIDTH), const(1, 3 * HY_WIDTH),
        const(1, Q_RANK), const(H * HEAD_PAD, Q_RANK),
        const(1, KV_RANK), const(KV_RANK + HEAD_PAD, H * HEAD_PAD),
        const(H * V_PAD, KV_RANK),
        tab_t, tab_t, pl.BlockSpec((tm, HEAD_PAD), lambda i: (i % tps, 0)),
    ]
    hy_spec = pl.BlockSpec((None, HY_CB, DFT_G, tm // DFT_G, LANES), lambda i: (i // tps, 0, 0, i % tps, 0))
    out_specs = [
        hy_spec, hy_spec, hy_spec,
        pl.BlockSpec((None, H, HEAD_PAD, tm), lambda i: (i // tps, 0, 0, i % tps)),
        pl.BlockSpec((None, H, tm, HEAD_PAD), lambda i: (i // tps, 0, i % tps, 0)),
        pl.BlockSpec((None, H, None, V_PAD, tm), lambda i: (i // tps, 0, i % tps, 0, 0)),
        pl.BlockSpec((None, None, STAT_ROWS, LANES), lambda i: (i // tps, i % tps, 0, 0)),
    ]
    hy_shape = jax.ShapeDtypeStruct((nb, HY_CB, DFT_G, L // DFT_G, LANES), F32)
    out_shape = [
        hy_shape, hy_shape, hy_shape,
        jax.ShapeDtypeStruct((nb, H, HEAD_PAD, L), BF16),
        jax.ShapeDtypeStruct((nb, H, L, HEAD_PAD), BF16),
        jax.ShapeDtypeStruct((nb, H, tps, V_PAD, tm), BF16),
        jax.ShapeDtypeStruct((nb, tps, STAT_ROWS, LANES), F32),
    ]
    scale = float((QK_NOPE + QK_ROPE) ** -0.5 * math.log2(math.e))
    return pl.pallas_call(
        functools.partial(_inproj_kernel, tm=tm, tiles_per_seq=tps, scale=scale),
        grid=(nt,), in_specs=in_specs, out_specs=out_specs, out_shape=out_shape,
        scratch_shapes=[pltpu.VMEM((tm + 2 * HALO, D), BF16),
                        pltpu.VMEM((tm + 2 * HALO, 3 * HY_WIDTH), F32)],
        compiler_params=_cparams(("parallel",)), name="inproj",
    )(x, x, x, lw["g_pre"], lw["win"], lw["conv_w"], lw["conv_b"], lw["q_g"], lw["wqt"],
      lw["kv_g"], lw["wkp"], lw["wvt"], tabs["rcos_t"], tabs["rsin_t"], tabs["kcs"])


def _filter_kernel(t_ref, zt_ref, w1_ref, b1_ref, sf_ref, w2_ref, b2_ref, w3_ref, dec_ref, out_ref, *, rb, L):
    hi = lax.Precision.HIGHEST
    r = pl.program_id(1)
    h = jnp.sin(sf_ref[:, 0:1] * (jnp.dot(w1_ref[...], zt_ref[...], precision=hi, preferred_element_type=F32)
                                  + b1_ref[...]))
    h = jnp.sin(sf_ref[:, 1:2] * (jnp.dot(w2_ref[...], h, precision=hi, preferred_element_type=F32) + b2_ref[...]))
    tn = lambda a, b: lax.dot_general(a, b, (((0,), (0,)), ((), ())), preferred_element_type=F32)
    h_hi = h.astype(BF16)
    h_lo = (h - h_hi.astype(F32)).astype(BF16)
    k = tn(h_hi, w3_ref[0]) + (tn(h_hi, w3_ref[1]) + tn(h_lo, w3_ref[0]))
    k = k * jnp.exp(-t_ref[...] * jnp.abs(dec_ref[...]))
    row = r * rb + lax.broadcasted_iota(jnp.int32, (rb, 1), 0)
    k = jnp.where(row == L, 0.0, k)
    _store_grouped(out_ref, k, 0, out_ref.shape[0])


def _filters(ztab, fw, L):
    depth = fw["w1"].shape[0]
    rows = 2 * L
    rb = min(512, L)
    nblk = rows // rb
    half = nblk // 2
    oc = HY_ORDER * HY_WIDTH
    lay = lambda *shape: pl.BlockSpec((None,) + shape, lambda l, r: (l,) + (0,) * len(shape))
    in_specs = [
        pl.BlockSpec((rb, 1), lambda l, r: (r, 0)), pl.BlockSpec((HY_EMB_PAD, rb), lambda l, r: (0, r)),
        lay(HY_FFN, HY_EMB_PAD), lay(HY_FFN, 1), lay(HY_FFN, 2), lay(HY_FFN, HY_FFN), lay(HY_FFN, 1),
        pl.BlockSpec((None, None, 2, HY_FFN, oc), lambda l, r: (l, r // half, 0, 0, 0)),
        pl.BlockSpec((None, None, 1, oc), lambda l, r: (l, r // half, 0, 0)),
    ]
    return pl.pallas_call(
        functools.partial(_filter_kernel, rb=rb, L=L),
        grid=(depth, nblk), in_specs=in_specs,
        out_specs=pl.BlockSpec((None, oc // LANES, DFT_G, rb // DFT_G, LANES), lambda l, r: (l, 0, 0, r, 0)),
        out_shape=jax.ShapeDtypeStruct((depth, oc // LANES, DFT_G, rows // DFT_G, LANES), F32),
        compiler_params=_cparams(("parallel", "parallel")), name="hyena_filter",
    )(ztab[:, 0:1], ztab.T, fw["w1"], fw["b1"], fw["sf"], fw["w2"], fw["b2"], fw["w3"], fw["dec"])


def _rows_of(ref, p, n, lead=()):
    return ref[lead + (pl.ds(p, n, stride=SUBLANES), slice(None))]


def _pack_pair(re, im):
    hi = lax.bitcast_convert_type(re.astype(BF16).astype(F32), jnp.uint32)
    lo = lax.bitcast_convert_type(im.astype(BF16).astype(F32), jnp.uint32)
    return hi | (lo >> 16)


def _unpack_pair(w, dtype=BF16):
    re = lax.bitcast_convert_type(w & jnp.uint32(0xFFFF0000), F32)
    im = lax.bitcast_convert_type(w << 16, F32)
    return re.astype(dtype), im.astype(dtype)


def _dft1_block(f_ref, xs, out_ref, gg):
    kp = f_ref.shape[0] // 2
    res = _dot(f_ref[...], jnp.concatenate([x.astype(BF16) for x in xs], axis=1))
    w = _pack_pair(res[:kp], res[kp:])
    for kg in range(kp // SUBLANES):
        for p in range(SUBLANES):
            r0 = (gg * SUBLANES + p) * SUBLANES
            out_ref[kg, r0:r0 + SUBLANES, :] = w[kg * SUBLANES:(kg + 1) * SUBLANES, p * LANES:(p + 1) * LANES]


def _dft1_kernel(f_ref, x_ref, out_ref):
    R = f_ref.shape[1]
    for gg in range(OUTER_GB):
        _dft1_block(f_ref, [_rows_of(x_ref, p, R, (gg,)) for p in range(SUBLANES)], out_ref, gg)


def _dft1_spec(kg):
    return pl.BlockSpec((None, None, kg, OUTER_GB * SUBLANES * SUBLANES, LANES), lambda b, c, g: (b, c, 0, g, 0))


def _dft1(x, f1):
    nb, ncb, _, rows, _ = x.shape
    kp = f1.shape[0] // 2
    kg = kp // SUBLANES
    return pl.pallas_call(
        _dft1_kernel, grid=(nb, ncb, DFT_G // OUTER_GB),
        in_specs=[pl.BlockSpec(f1.shape, lambda b, c, g: (0, 0)),
                  pl.BlockSpec((None, None, OUTER_GB, rows, LANES), lambda b, c, g: (b, c, g, 0, 0))],
        out_specs=_dft1_spec(kg),
        out_shape=jax.ShapeDtypeStruct((nb, ncb, kg, DFT_N2 * SUBLANES, LANES), jnp.uint32),
        compiler_params=_cparams(("parallel", "parallel", "parallel")), name="dft_outer",
    )(f1, x)


def _inner_fwd(m1_ref, a_ref, kk):
    w = jnp.concatenate([_rows_of(a_ref, kk, DFT_N2, (c,)) for c in range(a_ref.shape[0])], axis=1)
    ar, ai = _unpack_pair(w)
    return _dot(m1_ref[kk], jnp.concatenate([ar, ai], axis=0))


def _spec_kernel(m1_ref, a_ref, out_ref):
    for kk in range(SUBLANES):
        x = _inner_fwd(m1_ref, a_ref, kk)
        for c in range(a_ref.shape[0]):
            out_ref[c, kk] = _pack_pair(x[:DFT_N2, c * LANES:(c + 1) * LANES], x[DFT_N2:, c * LANES:(c + 1) * LANES])


def _filter_spectrum(a, m1):
    depth, ncb, kg, rows, _ = a.shape
    n2 = DFT_N2
    return pl.pallas_call(
        _spec_kernel, grid=(kg, depth, ncb // MID_CB),
        in_specs=[pl.BlockSpec((SUBLANES, 2 * n2, 2 * n2), lambda k, l, c: (k, 0, 0)),
                  pl.BlockSpec((None, MID_CB, None, rows, LANES), lambda k, l, c: (l, c, k, 0, 0))],
        out_specs=pl.BlockSpec((None, MID_CB, SUBLANES, n2, LANES), lambda k, l, c: (l, c, k, 0, 0)),
        out_shape=jax.ShapeDtypeStruct((depth, ncb, kg * SUBLANES, n2, LANES), jnp.uint32),
        compiler_params=_cparams(("parallel", "parallel", "parallel")), name="filter_spectrum",
    )(m1, a)


def _mid_kernel(m1_ref, m2_ref, kf_ref, a_ref, out_ref):
    ncb = a_ref.shape[0]
    lanes = lambda parts: jnp.concatenate(parts, axis=1)
    for kk in range(SUBLANES):
        x = _inner_fwd(m1_ref, a_ref, kk)
        xr, xi = x[:DFT_N2], x[DFT_N2:]
        kr, ki = _unpack_pair(lanes([kf_ref[c, kk] for c in range(ncb)]), F32)
        yr = (xr * kr - xi * ki).astype(BF16)
        yi = (xr * ki + xi * kr).astype(BF16)
        y = _dot(m2_ref[kk], jnp.concatenate([yr, yi], axis=0))
        w = _pack_pair(y[:DFT_N2], y[DFT_N2:])
        for c in range(ncb):
            for g in range(DFT_G):
                out_ref[c, g, kk * SUBLANES:(kk + 1) * SUBLANES, :] = (
                    w[g * SUBLANES:(g + 1) * SUBLANES, c * LANES:(c + 1) * LANES])


def _conv_mid(a, kf, m1, m2, layer, order):
    nb, ncb, kg, rows, _ = a.shape
    n2 = DFT_N2
    mspec = pl.BlockSpec((SUBLANES, 2 * n2, 2 * n2), lambda k, c, b: (k, 0, 0))
    return pl.pallas_call(
        _mid_kernel, grid=(kg, ncb // MID_CB, nb),
        in_specs=[mspec, mspec,
                  pl.BlockSpec((None, MID_CB, SUBLANES, n2, LANES),
                               lambda k, c, b: (layer, order * (ncb // MID_CB) + c, k, 0, 0)),
                  pl.BlockSpec((None, MID_CB, None, rows, LANES), lambda k, c, b: (b, c, k, 0, 0))],
        out_specs=pl.BlockSpec((None, MID_CB, DFT_G, SUBLANES * SUBLANES, LANES), lambda k, c, b: (b, c, 0, k, 0)),
        out_shape=jax.ShapeDtypeStruct((nb, ncb, DFT_G, kg * SUBLANES * SUBLANES, LANES), jnp.uint32),
        compiler_params=_cparams(("parallel", "parallel", "arbitrary")), name="conv_mid",
    )(m1, m2, kf, a)


def _gate_kernel(gre_ref, gim_ref, f_ref, b_ref, z_ref, gate_ref, bias_ref, out_ref, *next_ref):
    R, kp = gre_ref.shape
    bias = bias_ref[...]
    for gg in range(OUTER_GB):
        bre, bim = _unpack_pair(jnp.concatenate([_rows_of(b_ref, p, kp, (gg,)) for p in range(SUBLANES)], axis=1))
        y = _dot(gre_ref[...], bre) + _dot(gim_ref[...], bim)
        vals = []
        for p in range(SUBLANES):
            yp = y[:, p * LANES:(p + 1) * LANES]
            vals.append(_rows_of(gate_ref, p, R, (gg,)) * (yp + _rows_of(z_ref, p, R, (gg,)) * bias))
            out_ref[gg, pl.ds(p, R, stride=SUBLANES), :] = vals[p]
        if next_ref:
            _dft1_block(f_ref, vals, next_ref[0], gg)


def _conv_out(bsp, tabs, z, gate, bias, with_next):
    nb, ncb, _, krows, _ = bsp.shape
    rows = z.shape[3]
    gre, gim, f1 = tabs["gre"], tabs["gim"], tabs["f1d"]
    kg = f1.shape[0] // 2 // SUBLANES
    full = lambda t: pl.BlockSpec(t.shape, lambda b, c, g: (0, 0))
    tile = pl.BlockSpec((None, None, OUTER_GB, rows, LANES), lambda b, c, g: (b, c, g, 0, 0))
    out_specs, out_shape = [tile], [jax.ShapeDtypeStruct(z.shape, F32)]
    if with_next:
        out_specs.append(_dft1_spec(kg))
        out_shape.append(jax.ShapeDtypeStruct((nb, ncb, kg, DFT_N2 * SUBLANES, LANES), jnp.uint32))
    return pl.pallas_call(
        _gate_kernel, grid=(nb, ncb, DFT_G // OUTER_GB),
        in_specs=[full(gre), full(gim), full(f1),
                  pl.BlockSpec((None, None, OUTER_GB, krows, LANES), lambda b, c, g: (b, c, g, 0, 0)),
                  tile, tile, pl.BlockSpec((None, 1, LANES), lambda b, c, g: (c, 0, 0))],
        out_specs=out_specs, out_shape=out_shape,
        compiler_params=_cparams(("parallel", "parallel", "parallel")), name="conv_gate",
    )(gre, gim, f1, bsp, z, gate, bias)


def _hyena_convs(v, x1, x2, kf, bias, tabs, layer):
    b0, b1 = bias[0].reshape(HY_CB, 1, LANES), bias[1].reshape(HY_CB, 1, LANES)
    bsp = _conv_mid(_dft1(v, tabs["f1d"]), kf, tabs["m1"], tabs["m2"], layer, 0)
    z1, a2 = _conv_out(bsp, tabs, v, x1, b0, True)
    bsp = _conv_mid(a2, kf, tabs["m1"], tabs["m2"], layer, 1)
    return _conv_out(bsp, tabs, z1, x2, b1, False)[0]


def _attn_kernel(flag_ref, qt_ref, k_ref, vt_ref, g_ref, o_ref, acc_scr, m_scr, s_scr, *, nk, tk, unroll):
    pair = pl.program_id(0) * pl.num_programs(1) + pl.program_id(1)
    fast = jnp.logical_and(flag_ref[2 * pair] == 1, flag_ref[2 * pair + 1] == 1)
    acc_scr[...] = jnp.zeros(acc_scr.shape, F32)

    @pl.when(fast)
    def _():
        last = 2 * nk - 1

        def scores(cc):
            hd = cc // nk
            c = cc - hd * nk
            return _dot(k_ref[hd, pl.ds(pl.multiple_of(c * tk, tk), tk), :], qt_ref[hd])

        s_scr[0] = scores(0)

        def body(j, carry):
            hd = (j * unroll) // nk
            c0 = j * unroll - hd * nk
            pv = None
            for u in range(unroll):
                s_scr[(u + 1) % 2] = scores(jnp.minimum(j * unroll + u + 1, last))
                pt = jnp.exp2(s_scr[u % 2]).astype(BF16)
                d = _dot(vt_ref[hd, c0 + u], pt)
                pv = d if pv is None else pv + d
            acc_scr[hd] += pv
            return carry
        lax.fori_loop(0, 2 * nk // unroll, body, 0)

    @pl.when(jnp.logical_not(fast))
    def _():
        for hh in range(2):
            qt = qt_ref[hh]
            m_scr[...] = jnp.full(m_scr.shape, -jnp.inf, F32)

            def body(j, carry, hh=hh, qt=qt):
                kk = k_ref[hh, pl.ds(pl.multiple_of(j * tk, tk), tk), :]
                s = _dot(kk, qt)
                m_prev = m_scr[...]
                m_new = jnp.maximum(m_prev, jnp.max(s, axis=0, keepdims=True))
                pt = jnp.exp2(s - m_new).astype(BF16)
                acc_scr[hh] = jnp.exp2(m_prev - m_new) * acc_scr[hh] + _dot(vt_ref[hh, j], pt)
                m_scr[...] = m_new
                return carry
            lax.fori_loop(0, nk, body, 0)

    outs = []
    for hh in range(2):
        acc = acc_scr[hh]
        o = acc[:V_HEAD] / acc[V_HEAD:V_HEAD + 1]
        ms = jnp.mean(o * o, axis=0, keepdims=True)
        outs.append(o * lax.rsqrt(ms + EPS))
    ot = jnp.concatenate(outs, axis=0)
    o_ref[...] = (ot.T * g_ref[...]).astype(o_ref.dtype)


def _attention(flags, qt, k, vt, g_attn, nb, L):
    H = MLA_HEADS
    tq = min(512, L)
    nk, tk = vt.shape[2], vt.shape[4]
    grid_spec = pltpu.PrefetchScalarGridSpec(
        num_scalar_prefetch=1, grid=(nb, H // 2, L // tq),
        in_specs=[pl.BlockSpec((None, 2, HEAD_PAD, tq), lambda b, h, i, f: (b, h, 0, i)),
                  pl.BlockSpec((None, 2, L, HEAD_PAD), lambda b, h, i, f: (b, h, 0, 0)),
                  pl.BlockSpec((None, 2, nk, V_PAD, tk), lambda b, h, i, f: (b, h, 0, 0, 0)),
                  pl.BlockSpec((1, 2 * V_HEAD), lambda b, h, i, f: (0, h))],
        out_specs=pl.BlockSpec((None, tq, 2 * V_HEAD), lambda b, h, i, f: (b, i, h)),
        scratch_shapes=[pltpu.VMEM((2, V_PAD, tq), F32), pltpu.VMEM((1, tq), F32), pltpu.VMEM((2, tk, tq), F32)])
    unroll = math.gcd(nk, ATTN_UNROLL)
    assert unroll % 2 == 0
    return pl.pallas_call(
        functools.partial(_attn_kernel, nk=nk, tk=tk, unroll=unroll), grid_spec=grid_spec,
        out_shape=jax.ShapeDtypeStruct((nb, L, ATTN_WIDTH), BF16),
        compiler_params=_cparams(("parallel", "parallel", "arbitrary")), name="attention",
    )(flags, qt, k, vt, g_attn)


def _fast_flags(stats):
    H = MLA_HEADS
    qn = jnp.max(stats[:, :, 0:H, :], axis=(1, 3))
    kn = jnp.max(stats[:, :, H:2 * H, :], axis=(1, 3))
    vm = jnp.max(stats[:, :, 2 * H, :], axis=(1, 2))
    ok = jnp.logical_and(qn * kn <= FAST_S_MAX * FAST_S_MAX, (vm <= FAST_V_MAX)[:, None])
    return ok.astype(jnp.int32).reshape(-1)


def _mix_mlp_kernel(x_ref, zh_ref, an_ref, ghy_ref, gsum_ref, wo_ref, gpost_ref, gmpre_ref, wup_ref,
                    wdn_ref, gmpost_ref, out_ref):
    zh = jnp.concatenate(
        [jnp.concatenate([zh_ref[j, g, nl * SUBLANES:(nl + 1) * SUBLANES, :] for j in range(HY_CB)], axis=1)
         for nl in range(zh_ref.shape[2] // SUBLANES) for g in range(DFT_G)], axis=0)
    ms = _dot((zh * zh).astype(BF16), gsum_ref[...]) * (HY_GROUPS / HY_WIDTH)
    hn = (zh * lax.rsqrt(ms + EPS) * ghy_ref[...]).astype(BF16)
    mix = _dot(hn, wo_ref[0:HY_WIDTH, :]) + _dot(an_ref[...], wo_ref[HY_WIDTH:, :])
    x = x_ref[...] + _rms(mix, gpost_ref[...])
    h = _rms(x, gmpre_ref[...]).astype(BF16)
    up = jnp.maximum(_dot(h, wup_ref[...]), 0.0)
    m = _dot((up * up).astype(BF16), wdn_ref[...])
    out_ref[...] = x + _rms(m, gmpost_ref[...])


def _mix_mlp(x, zh, an, lw, tabs, L):
    M, D = x.shape
    tm = min(512, L)
    tps = L // tm
    dff = lw["wup"].shape[1]
    mw = HY_WIDTH + ATTN_WIDTH
    const = lambda *shape: pl.BlockSpec(shape, lambda i: (0,) * len(shape), pipeline_mode=pl.Buffered(1))
    rows = lambda w: pl.BlockSpec((tm, w), lambda i: (i, 0))
    return pl.pallas_call(
        _mix_mlp_kernel, grid=(M // tm,),
        in_specs=[rows(D), pl.BlockSpec((None, HY_CB, DFT_G, tm // DFT_G, LANES),
                                        lambda i: (i // tps, 0, 0, i % tps, 0)),
                  rows(ATTN_WIDTH), const(1, HY_WIDTH), const(HY_WIDTH, HY_WIDTH),
                  const(mw, D), const(1, D), const(1, D), const(D, dff), const(dff, D), const(1, D)],
        out_specs=rows(D), out_shape=jax.ShapeDtypeStruct((M, D), F32),
        compiler_params=_cparams(("parallel",)), name="mix_mlp",
    )(x, zh, an, lw["g_hy"], tabs["gsum"], lw["wo"], lw["g_post"], lw["g_mpre"], lw["wup"], lw["wdn"],
      lw["g_mpost"])


def _tables(L):
    n = 2 * L
    n2 = DFT_N2
    n1 = n // n2
    nh = n1 // 2
    kp = -(-(nh + 1) // SUBLANES) * SUBLANES
    two_pi = 2.0 * math.pi

    k1 = jnp.arange(kp, dtype=jnp.int32)
    valid = (k1 <= nh)
    def outer(ncols):
        nn = jnp.arange(ncols, dtype=jnp.int32)
        ang = ((k1[:, None] * nn[None, :]) % n1).astype(F32) * (two_pi / n1)
        c = jnp.where(valid[:, None], jnp.cos(ang), 0.0)
        s = jnp.where(valid[:, None], -jnp.sin(ang), 0.0)
        return jnp.concatenate([c, s], axis=0).astype(BF16)
    f1d = outer(nh)
    f1f = outer(n1)
    nn = jnp.arange(nh, dtype=jnp.int32)
    ang = ((nn[:, None] * k1[None, :]) % n1).astype(F32) * (two_pi / n1)
    wgt = jnp.where(valid, jnp.where((k1 == 0) | (k1 == nh), 1.0, 2.0), 0.0) / n
    gre = (jnp.cos(ang) * wgt[None, :]).astype(BF16)
    gim = (-jnp.sin(ang) * wgt[None, :]).astype(BF16)
    a2 = jnp.arange(n2, dtype=jnp.int32)
    idx = (a2[None, :, None] * a2[None, None, :] * n1 + a2[None, None, :] * k1[:, None, None]) % n
    ph = idx.astype(F32) * (two_pi / n)
    gr, gi = jnp.cos(ph), -jnp.sin(ph)
    m1 = jnp.concatenate([jnp.concatenate([gr, -gi], axis=2), jnp.concatenate([gi, gr], axis=2)], axis=1)
    m2 = jnp.swapaxes(m1, 1, 2)
    inv = 1.0 / (ROPE_BASE ** (jnp.arange(0, QK_ROPE, 2, dtype=F32) / QK_ROPE))
    ang = jnp.arange(L, dtype=F32)[:, None] * inv[None, :]
    cos, sin = jnp.cos(ang), jnp.sin(ang)
    rcos = jnp.concatenate([cos, cos], axis=1)
    rsin = jnp.concatenate([sin, sin], axis=1)
    kcs =jnp.concatenate([cos, cos, sin, sin, jnp.zeros((L, HEAD_PAD - 2 * QK_ROPE), F32)], axis=1)
    cc = jnp.arange(MLA_HEADS * HEAD_PAD)
    src = jnp.arange(HEAD_PAD)
    pk = ((cc[None, :] % HEAD_PAD) - QK_NOPE == src[:, None]) & (src[:, None] < QK_ROPE)
    grp = jnp.arange(HY_WIDTH) // (HY_WIDTH // HY_GROUPS)
    gsum = (grp[:, None] == grp[None, :]).astype(BF16)
    t = jnp.linspace(0.0, 1.0, L, dtype=F32)[:, None]
    omega = (two_pi / L) * jnp.arange(L, dtype=F32)
    bands = jnp.linspace(1e-4, HY_BANDS - 1, HY_BANDS, dtype=F32)
    phase = omega[:, None] * bands[None, :]
    z = jnp.concatenate([t, jnp.cos(phase), -jnp.sin(phase), jnp.zeros((L, HY_EMB_PAD - HY_EMB), F32)], axis=-1)
    ztab = jnp.concatenate([z, z[:1], z[:0:-1]], axis=0)
    return dict(f1d=f1d, f1f=f1f, gre=gre, gim=gim, m1=m1.astype(BF16), m2=m2.astype(BF16), rcos_t=rcos.T,
                rsin_t=rsin.T, kcs=kcs, pk=pk.astype(BF16), gsum=gsum, ztab=ztab)


def _rot_half_cols(w):
    half = QK_ROPE // 2
    return jnp.concatenate([-w[..., half:], w[..., :half]], axis=-1)


def _layer_weights(i, p, tabs):
    D = p["w_in"].shape[1]
    H = MLA_HEADS
    hw3 = 3 * HY_WIDTH
    w_in = p["w_in"][i]
    kpe = w_in[:, hw3 + Q_RANK + KV_RANK:]
    win = jnp.concatenate([w_in, _rot_half_cols(kpe),
                           jnp.zeros((D, HEAD_PAD - 2 * QK_ROPE), F32)], axis=1).astype(BF16)
    dq = QK_NOPE + QK_ROPE
    wq = p["mla_w_uq"][i].reshape(Q_RANK, H, dq)
    wqt = jnp.concatenate([wq.reshape(Q_RANK, H * dq),
                           _rot_half_cols(wq[..., QK_NOPE:]).reshape(Q_RANK, H * QK_ROPE)], axis=1).T
    wkv =p["mla_w_ukv"][i].reshape(KV_RANK, H, QK_NOPE + V_HEAD)
    wk = jnp.concatenate([wkv[..., :QK_NOPE], jnp.zeros((KV_RANK, H, HEAD_PAD - QK_NOPE), F32)], axis=2)
    wv = jnp.concatenate([wkv[..., QK_NOPE:], jnp.zeros((KV_RANK, H, V_PAD - V_HEAD), F32)], axis=2)
    row = lambda a: a.reshape(1, -1)
    return dict(
        win=win, g_pre=row(p["norm_mix_pre"][i]), conv_w=p["hy_conv_w"][i], conv_b=row(p["hy_conv_b"][i]),
        q_g=row(p["mla_q_norm"][i]), wqt=wqt.astype(BF16), kv_g=row(p["mla_kv_norm"][i]),
        wkp=jnp.concatenate([wk.reshape(KV_RANK, H * HEAD_PAD).astype(BF16), tabs["pk"]], axis=0),
        wvt=wv.reshape(KV_RANK, H * V_PAD).T.astype(BF16),
        g_hy=row(p["grp_norm_hy"][i]), g_attn=row(p["grp_norm_attn"][i]), wo=p["w_out"][i].astype(BF16),
        g_post=row(p["norm_mix_post"][i]), g_mpre=row(p["norm_mlp_pre"][i]), g_mpost=row(p["norm_mlp_post"][i]),
        wup=p["w_mlp_up"][i].astype(BF16), wdn=p["w_mlp_down"][i].astype(BF16),
    )


def _filter_weights(p):
    depth = p["hy_ffn_w1"].shape[0]
    oc = HY_ORDER * HY_WIDTH
    w1 = jnp.pad(p["hy_ffn_w1"], ((0, 0), (0, HY_EMB_PAD - HY_EMB), (0, 0)))
    w3 = p["hy_ffn_w3"].reshape(depth, HY_FFN, HY_ORDER, 2, HY_WIDTH).transpose(0, 3, 1, 2, 4)
    dec = p["hy_decay"].transpose(0, 2, 1, 3).reshape(depth, 2, 1, oc)
    tr = lambda a: jnp.swapaxes(a, 1, 2)
    w3 = w3.reshape(depth, 2, HY_FFN, oc)
    w3_hi = w3.astype(BF16)
    w3_lo = (w3 - w3_hi.astype(F32)).astype(BF16)
    return dict(w1=tr(w1), b1=p["hy_ffn_b1"][:, :, None], sf=tr(p["hy_sin_freq"]), w2=tr(p["hy_ffn_w2"]),
                b2=p["hy_ffn_b2"][:, :, None], w3=jnp.stack([w3_hi, w3_lo], axis=2), dec=dec)


def kernel(x_prompt, x_sample, w_in, hy_conv_w, hy_conv_b, hy_ffn_w1, hy_ffn_b1, hy_ffn_w2, hy_ffn_b2,
           hy_ffn_w3, hy_sin_freq, hy_decay, hy_bias, mla_q_norm, mla_w_uq, mla_kv_norm, mla_w_ukv,
           grp_norm_hy, grp_norm_attn, w_out, norm_mix_pre, norm_mix_post, norm_mlp_pre, norm_mlp_post,
           w_mlp_up, w_mlp_down):
    p = dict(w_in=w_in, hy_conv_w=hy_conv_w, hy_conv_b=hy_conv_b, hy_ffn_w1=hy_ffn_w1, hy_ffn_b1=hy_ffn_b1,
             hy_ffn_w2=hy_ffn_w2, hy_ffn_b2=hy_ffn_b2, hy_ffn_w3=hy_ffn_w3, hy_sin_freq=hy_sin_freq,
             hy_decay=hy_decay, hy_bias=hy_bias, mla_q_norm=mla_q_norm, mla_w_uq=mla_w_uq,
             mla_kv_norm=mla_kv_norm, mla_w_ukv=mla_w_ukv, grp_norm_hy=grp_norm_hy,
             grp_norm_attn=grp_norm_attn, w_out=w_out, norm_mix_pre=norm_mix_pre, norm_mix_post=norm_mix_post,
             norm_mlp_pre=norm_mlp_pre, norm_mlp_post=norm_mlp_post, w_mlp_up=w_mlp_up, w_mlp_down=w_mlp_down)
    bp, L, D = x_prompt.shape
    bs, Ls, _ = x_sample.shape
    assert L == Ls and L % (DFT_N2 * SUBLANES) == 0
    nb = bp + bs
    depth = w_in.shape[0]

    tabs = _tables(L)
    kc = _filters(tabs["ztab"], _filter_weights(p), L)
    kf = _filter_spectrum(_dft1(kc, tabs["f1f"]), tabs["m1"])

    x = jnp.concatenate([x_prompt.reshape(bp * L, D), x_sample.reshape(bs * L, D)], axis=0)
    for i in range(depth):
        lw = _layer_weights(i, p, tabs)
        v, x1, x2, qt, k, vt, stats = _inproj(x, lw, tabs, nb, L)
        z = _hyena_convs(v, x1, x2, kf, hy_bias[i], tabs, i)
        an = _attention(_fast_flags(stats), qt, k, vt, lw["g_attn"], nb, L).reshape(nb * L, ATTN_WIDTH)
        x = _mix_mlp(x, z, an, lw, tabs, L)
    return (x[:bp * L].reshape(bp, L, D), x[bp * L:].reshape(bs, L, D))
```

```python
import functools
import math

import jax
import jax.numpy as jnp
from jax import lax
from jax.experimental import pallas as pl
from jax.experimental.pallas import tpu as pltpu

F32 = jnp.float32
BF16 = jnp.bfloat16

EPS = 1e-6
HY_WIDTH = 512
HY_GROUPS = 8
HY_ORDER = 2
HY_BANDS = 16
HY_EMB = 2 * HY_BANDS + 1
HY_EMB_PAD = 40
HY_FFN = 64
MLA_HEADS = 8
QK_NOPE = 64
QK_ROPE = 32
V_HEAD = 64
Q_RANK = 256
KV_RANK = 128
ROPE_BASE = 10000.0
HEAD_PAD = 128
V_PAD = 96
ATTN_WIDTH = MLA_HEADS * V_HEAD

LANES = 128
SUBLANES = 8
HY_CB = HY_WIDTH // LANES
DFT_N2 = 128
DFT_G = DFT_N2 // SUBLANES
MID_CB = 2
OUTER_GB = 4
HALO = 16
VMEM_LIMIT = 56 * 1024 * 1024
ATTN_UNROLL = 32
STAT_ROWS = 24

FAST_S_MAX = 64.0
FAST_V_MAX = 2.0 ** 30


def _cparams(sem):
    return pltpu.CompilerParams(dimension_semantics=sem, vmem_limit_bytes=VMEM_LIMIT)


def _dot(a, b):
    return jnp.dot(a, b, preferred_element_type=F32)


def _dot_nt(a, b):
    return lax.dot_general(a, b, (((1,), (1,)), ((), ())), preferred_element_type=F32)


def _rms(x, g):
    return x * lax.rsqrt(jnp.mean(x * x, axis=-1, keepdims=True) + EPS) * g


def _store_grouped(out_ref, val, lane0, ncb):
    for j in range(ncb):
        for nl in range(val.shape[0] // DFT_N2):
            for g in range(DFT_G):
                r0 = nl * DFT_N2 + g * SUBLANES
                out_ref[j, g, nl * SUBLANES:(nl + 1) * SUBLANES, :] = (
                    val[r0:r0 + SUBLANES, lane0 + j * LANES:lane0 + (j + 1) * LANES])


def _fold_lanes(row):
    parts = [row[:, j * LANES:(j + 1) * LANES] for j in range(row.shape[1] // LANES)]
    return functools.reduce(jnp.maximum, parts)


def _inproj_kernel(xp_ref, x_ref, xn_ref, gpre_ref, win_ref, cw_ref, cb_ref, qg_ref, wqt_ref,
                   kvg_ref, wkp_ref, wvt_ref, qcos_ref, qsin_ref, kcs_ref,
                   v_out, x1_out, x2_out, qt_out, k_out, vt_out, st_out, h_scr, pe_scr,
                   *, tm, tiles_per_seq, scale):
    i = pl.program_id(0)
    t_idx = i % tiles_per_seq
    g = gpre_ref[...]
    h_scr[0:HALO, :] = _rms(xp_ref[...], g).astype(BF16)
    h_scr[HALO:HALO + tm, :] = _rms(x_ref[...], g).astype(BF16)
    h_scr[HALO + tm:2 * HALO + tm, :] = _rms(xn_ref[...], g).astype(BF16)
    hw3 = 3 * HY_WIDTH
    core = _dot(h_scr[HALO:HALO + tm, :], win_ref[:, hw3:])
    pe_scr[...] = _dot(h_scr[...], win_ref[:, 0:hw3])
    row = lax.broadcasted_iota(jnp.int32, (tm, 1), 0)
    prev = pe_scr[HALO - 1:HALO - 1 + tm, :]
    cur = pe_scr[HALO:HALO + tm, :]
    nxt = pe_scr[HALO + 1:HALO + 1 + tm, :]
    prev = jnp.where(jnp.logical_and(row == 0, t_idx == 0), 0.0, prev)
    nxt = jnp.where(jnp.logical_and(row == tm - 1, t_idx == tiles_per_seq - 1), 0.0, nxt)
    u = prev * cw_ref[0:1, :] + cur * cw_ref[1:2, :] + nxt * cw_ref[2:3, :] + cb_ref[...]
    for o, out in enumerate((v_out, x1_out, x2_out)):
        _store_grouped(out, u, o * HY_WIDTH, HY_CB)

    cq = core[:, 0:Q_RANK]
    cqn =_rms(cq, qg_ref[...]).astype(BF16)
    dq = QK_NOPE + QK_ROPE
    qall = _dot_nt(wqt_ref[...], cqn)
    rcos = qcos_ref[...]
    rsin = qsin_ref[...]
    zpad = jnp.zeros((HEAD_PAD - dq, tm), F32)
    for h in range(MLA_HEADS):
        rot = qall[MLA_HEADS * dq + h * QK_ROPE:MLA_HEADS * dq + (h + 1) * QK_ROPE]
        rope = qall[h * dq + QK_NOPE:(h + 1) * dq] * rcos + rot * rsin
        qh = (jnp.concatenate([qall[h * dq:h * dq + QK_NOPE], rope, zpad], axis=0) * scale).astype(BF16)
        qt_out[h] = qh
        qf = qh.astype(F32)
        st_out[h:h + 1, :] = _fold_lanes(jnp.sum(qf * qf, axis=0, keepdims=True))

    ckv = core[:, Q_RANK:Q_RANK + KV_RANK]
    ckvn = _rms(ckv, kvg_ref[...]).astype(BF16)
    kp = core[:, Q_RANK + KV_RANK:] * kcs_ref[...]
    kp = kp + pltpu.roll(kp, HEAD_PAD - QK_ROPE, axis=1)
    kk = _dot(jnp.concatenate([ckvn, kp.astype(BF16)], axis=1), wkp_ref[...])
    for h in range(MLA_HEADS):
        kh = kk[:, h * HEAD_PAD:(h + 1) * HEAD_PAD].astype(BF16)
        k_out[h] = kh
        kf = kh.astype(F32)
        kn = jnp.max(jnp.sum(kf * kf, axis=1, keepdims=True), axis=0, keepdims=True)
        st_out[MLA_HEADS + h:MLA_HEADS + h + 1, :] = jnp.broadcast_to(kn, (1, LANES))
    vt = _dot_nt(wvt_ref[...], ckvn)
    st_out[2 * MLA_HEADS:2 * MLA_HEADS + 1, :] = _fold_lanes(jnp.max(jnp.abs(vt), axis=0, keepdims=True))
    st_out[2 * MLA_HEADS + 1:, :] = jnp.zeros((STAT_ROWS - 2 * MLA_HEADS - 1, LANES), F32)
    frow = lax.broadcasted_iota(jnp.int32, (MLA_HEADS * V_PAD, 1), 0)
    vt = vt + jnp.where(frow % V_PAD == V_HEAD, 1.0, 0.0)
    for h in range(MLA_HEADS):
        vt_out[h] = vt[h * V_PAD:(h + 1) * V_PAD, :].astype(BF16)


def _inproj(x, lw, tabs, nb, L):
    M, D = x.shape
    tm = min(512, L)
    tps = L // tm
    nt = M // tm
    hb = tm // HALO
    nhb = M // HALO
    H = MLA_HEADS
    wcols = lw["win"].shape[1]
    const = lambda *shape: pl.BlockSpec(shape, lambda i: (0,) * len(shape))
    tab_t = pl.BlockSpec((QK_ROPE, tm), lambda i: (0, i % tps))
    in_specs = [
        pl.BlockSpec((HALO, D), lambda i: (jnp.maximum(i * hb - 1, 0), 0)),
        pl.BlockSpec((tm, D), lambda i: (i, 0)),
        pl.BlockSpec((HALO, D), lambda i: (jnp.minimum((i + 1) * hb, nhb - 1), 0)),
        const(1, D), const(D, wcols), const(3, 3 * HY_WIDTH), const(1, 3 * HY_WIDTH),
        const(1, Q_RANK), const(H * HEAD_PAD, Q_RANK),
        const(1, KV_RANK), const(KV_RANK + HEAD_PAD, H * HEAD_PAD),
        const(H * V_PAD, KV_RANK),
        tab_t, tab_t, pl.BlockSpec((tm, HEAD_PAD), lambda i: (i % tps, 0)),
    ]
    hy_spec = pl.BlockSpec((None, HY_CB, DFT_G, tm // DFT_G, LANES), lambda i: (i // tps, 0, 0, i % tps, 0))
    out_specs = [
        hy_spec, hy_spec, hy_spec,
        pl.BlockSpec((None, H, HEAD_PAD, tm), lambda i: (i // tps, 0, 0, i % tps)),
        pl.BlockSpec((None, H, tm, HEAD_PAD), lambda i: (i // tps, 0, i % tps, 0)),
        pl.BlockSpec((None, H, None, V_PAD, tm), lambda i: (i // tps, 0, i % tps, 0, 0)),
        pl.BlockSpec((None, None, STAT_ROWS, LANES), lambda i: (i // tps, i % tps, 0, 0)),
    ]
    hy_shape = jax.ShapeDtypeStruct((nb, HY_CB, DFT_G, L // DFT_G, LANES), F32)
    out_shape = [
        hy_shape, hy_shape, hy_shape,
        jax.ShapeDtypeStruct((nb, H, HEAD_PAD, L), BF16),
        jax.ShapeDtypeStruct((nb, H, L, HEAD_PAD), BF16),
        jax.ShapeDtypeStruct((nb, H, tps, V_PAD, tm), BF16),
        jax.ShapeDtypeStruct((nb, tps, STAT_ROWS, LANES), F32),
    ]
    scale = float((QK_NOPE + QK_ROPE) ** -0.5 * math.log2(math.e))
    return pl.pallas_call(
        functools.partial(_inproj_kernel, tm=tm, tiles_per_seq=tps, scale=scale),
        grid=(nt,), in_specs=in_specs, out_specs=out_specs, out_shape=out_shape,
        scratch_shapes=[pltpu.VMEM((tm + 2 * HALO, D), BF16),
                        pltpu.VMEM((tm + 2 * HALO, 3 * HY_WIDTH), F32)],
        compiler_params=_cparams(("parallel",)), name="inproj",
    )(x, x, x, lw["g_pre"], lw["win"], lw["conv_w"], lw["conv_b"], lw["q_g"], lw["wqt"],
      lw["kv_g"], lw["wkp"], lw["wvt"], tabs["rcos_t"], tabs["rsin_t"], tabs["kcs"])


def _filter_kernel(t_ref, zt_ref, w1_ref, b1_ref, sf_ref, w2_ref, b2_ref, w3_ref, dec_ref, out_ref, *, rb, L):
    hi = lax.Precision.HIGHEST
    r = pl.program_id(1)
    h = jnp.sin(sf_ref[:, 0:1] * (jnp.dot(w1_ref[...], zt_ref[...], precision=hi, preferred_element_type=F32)
                                  + b1_ref[...]))
    h = jnp.sin(sf_ref[:, 1:2] * (jnp.dot(w2_ref[...], h, precision=hi, preferred_element_type=F32) + b2_ref[...]))
    tn = lambda a, b: lax.dot_general(a, b, (((0,), (0,)), ((), ())), preferred_element_type=F32)
    h_hi = h.astype(BF16)
    h_lo = (h - h_hi.astype(F32)).astype(BF16)
    k = tn(h_hi, w3_ref[0]) + (tn(h_hi, w3_ref[1]) + tn(h_lo, w3_ref[0]))
    k = k * jnp.exp(-t_ref[...] * jnp.abs(dec_ref[...]))
    row = r * rb + lax.broadcasted_iota(jnp.int32, (rb, 1), 0)
    k = jnp.where(row == L, 0.0, k)
    npair = out_ref.shape[0]
    w = jnp.concatenate([_pack_pair(k[:, 2 * j * LANES:(2 * j + 1) * LANES], k[:, (2 * j + 1) * LANES:(2 * j + 2) * LANES])
                         for j in range(npair)], axis=1)
    _store_grouped(out_ref, w, 0, npair)


def _filters(ztab, fw, L):
    depth = fw["w1"].shape[0]
    rows = 2 * L
    rb = min(512, L)
    nblk = rows // rb
    half = nblk // 2
    oc = HY_ORDER * HY_WIDTH
    lay = lambda *shape: pl.BlockSpec((None,) + shape, lambda l, r: (l,) + (0,) * len(shape))
    in_specs = [
        pl.BlockSpec((rb, 1), lambda l, r: (r, 0)), pl.BlockSpec((HY_EMB_PAD, rb), lambda l, r: (0, r)),
        lay(HY_FFN, HY_EMB_PAD), lay(HY_FFN, 1), lay(HY_FFN, 2), lay(HY_FFN, HY_FFN), lay(HY_FFN, 1),
        pl.BlockSpec((None, None, 2, HY_FFN, oc), lambda l, r: (l, r // half, 0, 0, 0)),
        pl.BlockSpec((None, None, 1, oc), lambda l, r: (l, r // half, 0, 0)),
    ]
    return pl.pallas_call(
        functools.partial(_filter_kernel, rb=rb, L=L),
        grid=(depth, nblk), in_specs=in_specs,
        out_specs=pl.BlockSpec((None, oc // LANES // 2, DFT_G, rb // DFT_G, LANES), lambda l, r: (l, 0, 0, r, 0)),
        out_shape=jax.ShapeDtypeStruct((depth, oc // LANES // 2, DFT_G, rows // DFT_G, LANES), jnp.uint32),
        compiler_params=_cparams(("parallel", "parallel")), name="hyena_filter",
    )(ztab[:, 0:1], ztab.T, fw["w1"], fw["b1"], fw["sf"], fw["w2"], fw["b2"], fw["w3"], fw["dec"])


def _rows_of(ref, p, n, lead=()):
    return ref[lead + (pl.ds(p, n, stride=SUBLANES), slice(None))]


def _pack_pair(re, im):
    hi = lax.bitcast_convert_type(re.astype(BF16).astype(F32), jnp.uint32)
    lo = lax.bitcast_convert_type(im.astype(BF16).astype(F32), jnp.uint32)
    return hi | (lo >> 16)


def _unpack_pair(w, dtype=BF16):
    re = lax.bitcast_convert_type(w & jnp.uint32(0xFFFF0000), F32)
    im = lax.bitcast_convert_type(w << 16, F32)
    return re.astype(dtype), im.astype(dtype)


def _dft1_block(f_ref, xs, out_ref, gg):
    kp = f_ref.shape[0] // 2
    res = _dot(f_ref[...], jnp.concatenate([x.astype(BF16) for x in xs], axis=1))
    w = _pack_pair(res[:kp], res[kp:])
    for kg in range(kp // SUBLANES):
        for p in range(SUBLANES):
            r0 = (gg * SUBLANES + p) * SUBLANES
            out_ref[kg, r0:r0 + SUBLANES, :] = w[kg * SUBLANES:(kg + 1) * SUBLANES, p * LANES:(p + 1) * LANES]


def _dft1_kernel(f_ref, x_ref, out_ref):
    R = f_ref.shape[1]
    for gg in range(OUTER_GB):
        _dft1_block(f_ref, [_rows_of(x_ref, p, R, (gg,)) for p in range(SUBLANES)], out_ref, gg)


def _dft1_spec(kg):
    return pl.BlockSpec((None, None, kg, OUTER_GB * SUBLANES * SUBLANES, LANES), lambda b, c, g: (b, c, 0, g, 0))


def _dft1(x, f1):
    nb, ncb, _, rows, _ = x.shape
    kp = f1.shape[0] // 2
    kg = kp // SUBLANES
    return pl.pallas_call(
        _dft1_kernel, grid=(nb, ncb, DFT_G // OUTER_GB),
        in_specs=[pl.BlockSpec(f1.shape, lambda b, c, g: (0, 0)),
                  pl.BlockSpec((None, None, OUTER_GB, rows, LANES), lambda b, c, g: (b, c, g, 0, 0))],
        out_specs=_dft1_spec(kg),
        out_shape=jax.ShapeDtypeStruct((nb, ncb, kg, DFT_N2 * SUBLANES, LANES), jnp.uint32),
        compiler_params=_cparams(("parallel", "parallel", "parallel")), name="dft_outer",
    )(f1, x)


def _dft1_pairs_kernel(f_ref, x_ref, out_ref):
    R = f_ref.shape[1]
    for gg in range(OUTER_GB):
        pairs = [_unpack_pair(_rows_of(x_ref, p, R, (gg,))) for p in range(SUBLANES)]
        for c in range(2):
            _dft1_block(f_ref, [pr[c] for pr in pairs], out_ref.at[c], gg)


def _dft1_pairs(x, f1):
    nb, npair, _, rows, _ = x.shape
    kg = f1.shape[0] // 2 // SUBLANES
    return pl.pallas_call(
        _dft1_pairs_kernel, grid=(nb, npair, DFT_G // OUTER_GB),
        in_specs=[pl.BlockSpec(f1.shape, lambda b, c, g: (0, 0)),
                  pl.BlockSpec((None, None, OUTER_GB, rows, LANES), lambda b, c, g: (b, c, g, 0, 0))],
        out_specs=pl.BlockSpec((None, 2, kg, OUTER_GB * SUBLANES * SUBLANES, LANES), lambda b, c, g: (b, c, 0, g, 0)),
        out_shape=jax.ShapeDtypeStruct((nb, 2 * npair, kg, DFT_N2 * SUBLANES, LANES), jnp.uint32),
        compiler_params=_cparams(("parallel", "parallel", "parallel")), name="dft_outer_filter",
    )(f1, x)


def _inner_fwd(m1_ref, a_ref, kk):
    w = jnp.concatenate([_rows_of(a_ref, kk, DFT_N2, (c,)) for c in range(a_ref.shape[0])], axis=1)
    ar, ai = _unpack_pair(w)
    return _dot(m1_ref[kk], jnp.concatenate([ar, ai], axis=0))


def _spec_kernel(m1_ref, a_ref, out_ref):
    for kk in range(SUBLANES):
        x = _inner_fwd(m1_ref, a_ref, kk)
        for c in range(a_ref.shape[0]):
            out_ref[c, kk] = _pack_pair(x[:DFT_N2, c * LANES:(c + 1) * LANES], x[DFT_N2:, c * LANES:(c + 1) * LANES])


def _filter_spectrum(a, m1):
    depth, ncb, kg, rows, _ = a.shape
    n2 = DFT_N2
    return pl.pallas_call(
        _spec_kernel, grid=(kg, depth, ncb // MID_CB),
        in_specs=[pl.BlockSpec((SUBLANES, 2 * n2, 2 * n2), lambda k, l, c: (k, 0, 0)),
                  pl.BlockSpec((None, MID_CB, None, rows, LANES), lambda k, l, c: (l, c, k, 0, 0))],
        out_specs=pl.BlockSpec((None, MID_CB, SUBLANES, n2, LANES), lambda k, l, c: (l, c, k, 0, 0)),
        out_shape=jax.ShapeDtypeStruct((depth, ncb, kg * SUBLANES, n2, LANES), jnp.uint32),
        compiler_params=_cparams(("parallel", "parallel", "parallel")), name="filter_spectrum",
    )(m1, a)


def _mid_kernel(m1_ref, m2_ref, kf_ref, a_ref, out_ref):
    ncb = a_ref.shape[0]
    lanes = lambda parts: jnp.concatenate(parts, axis=1)
    for kk in range(SUBLANES):
        x = _inner_fwd(m1_ref, a_ref, kk)
        xr, xi = x[:DFT_N2], x[DFT_N2:]
        kr, ki = _unpack_pair(lanes([kf_ref[c, kk] for c in range(ncb)]), F32)
        yr = (xr * kr - xi * ki).astype(BF16)
        yi = (xr * ki + xi * kr).astype(BF16)
        y = _dot(m2_ref[kk], jnp.concatenate([yr, yi], axis=0))
        w = _pack_pair(y[:DFT_N2], y[DFT_N2:])
        for c in range(ncb):
            for g in range(DFT_G):
                out_ref[c, g, kk * SUBLANES:(kk + 1) * SUBLANES, :] = (
                    w[g * SUBLANES:(g + 1) * SUBLANES, c * LANES:(c + 1) * LANES])


def _conv_mid(a, kf, m1, m2, layer, order):
    nb, ncb, kg, rows, _ = a.shape
    n2 = DFT_N2
    mspec = pl.BlockSpec((SUBLANES, 2 * n2, 2 * n2), lambda k, c, b: (k, 0, 0))
    return pl.pallas_call(
        _mid_kernel, grid=(kg, ncb // MID_CB, nb),
        in_specs=[mspec, mspec,
                  pl.BlockSpec((None, MID_CB, SUBLANES, n2, LANES),
                               lambda k, c, b: (layer, order * (ncb // MID_CB) + c, k, 0, 0)),
                  pl.BlockSpec((None, MID_CB, None, rows, LANES), lambda k, c, b: (b, c, k, 0, 0))],
        out_specs=pl.BlockSpec((None, MID_CB, DFT_G, SUBLANES * SUBLANES, LANES), lambda k, c, b: (b, c, 0, k, 0)),
        out_shape=jax.ShapeDtypeStruct((nb, ncb, DFT_G, kg * SUBLANES * SUBLANES, LANES), jnp.uint32),
        compiler_params=_cparams(("parallel", "parallel", "arbitrary")), name="conv_mid",
    )(m1, m2, kf, a)


def _gate_kernel(gre_ref, gim_ref, f_ref, b_ref, z_ref, gate_ref, bias_ref, out_ref, *next_ref):
    R, kp = gre_ref.shape
    bias = bias_ref[...]
    for gg in range(OUTER_GB):
        bre, bim = _unpack_pair(jnp.concatenate([_rows_of(b_ref, p, kp, (gg,)) for p in range(SUBLANES)], axis=1))
        y = _dot(gre_ref[...], bre) + _dot(gim_ref[...], bim)
        vals = []
        for p in range(SUBLANES):
            yp = y[:, p * LANES:(p + 1) * LANES]
            vals.append(_rows_of(gate_ref, p, R, (gg,)) * (yp + _rows_of(z_ref, p, R, (gg,)) * bias))
            out_ref[gg, pl.ds(p, R, stride=SUBLANES), :] = vals[p]
        if next_ref:
            _dft1_block(f_ref, vals, next_ref[0], gg)


def _conv_out(bsp, tabs, z, gate, bias, with_next):
    nb, ncb, _, krows, _ = bsp.shape
    rows = z.shape[3]
    gre, gim, f1 = tabs["gre"], tabs["gim"], tabs["f1d"]
    kg = f1.shape[0] // 2 // SUBLANES
    full = lambda t: pl.BlockSpec(t.shape, lambda b, c, g: (0, 0))
    tile = pl.BlockSpec((None, None, OUTER_GB, rows, LANES), lambda b, c, g: (b, c, g, 0, 0))
    out_specs, out_shape = [tile], [jax.ShapeDtypeStruct(z.shape, F32)]
    if with_next:
        out_specs.append(_dft1_spec(kg))
        out_shape.append(jax.ShapeDtypeStruct((nb, ncb, kg, DFT_N2 * SUBLANES, LANES), jnp.uint32))
    return pl.pallas_call(
        _gate_kernel, grid=(nb, ncb, DFT_G // OUTER_GB),
        in_specs=[full(gre), full(gim), full(f1),
                  pl.BlockSpec((None, None, OUTER_GB, krows, LANES), lambda b, c, g: (b, c, g, 0, 0)),
                  tile, tile, pl.BlockSpec((None, 1, LANES), lambda b, c, g: (c, 0, 0))],
        out_specs=out_specs, out_shape=out_shape,
        compiler_params=_cparams(("parallel", "parallel", "parallel")), name="conv_gate",
    )(gre, gim, f1, bsp, z, gate, bias)


def _hyena_convs(v, x1, x2, kf, bias, tabs, layer):
    b0, b1 = bias[0].reshape(HY_CB, 1, LANES), bias[1].reshape(HY_CB, 1, LANES)
    bsp = _conv_mid(_dft1(v, tabs["f1d"]), kf, tabs["m1"], tabs["m2"], layer, 0)
    z1, a2 = _conv_out(bsp, tabs, v, x1, b0, True)
    bsp = _conv_mid(a2, kf, tabs["m1"], tabs["m2"], layer, 1)
    return _conv_out(bsp, tabs, z1, x2, b1, False)[0]


def _attn_kernel(flag_ref, qt_ref, k_ref, vt_ref, g_ref, o_ref, acc_scr, m_scr, s_scr, *, nk, tk, unroll):
    pair = pl.program_id(0) * pl.num_programs(1) + pl.program_id(1)
    fast = jnp.logical_and(flag_ref[2 * pair] == 1, flag_ref[2 * pair + 1] == 1)
    acc_scr[...] = jnp.zeros(acc_scr.shape, F32)

    @pl.when(fast)
    def _():
        last = 2 * nk - 1

        def scores(cc):
            hd = cc // nk
            c = cc - hd * nk
            return _dot(k_ref[hd, pl.ds(pl.multiple_of(c * tk, tk), tk), :], qt_ref[hd])

        s_scr[0] = scores(0)

        def body(j, carry):
            hd = (j * unroll) // nk
            c0 = j * unroll - hd * nk
            pv = None
            for u in range(unroll):
                s_scr[(u + 1) % 2] = scores(jnp.minimum(j * unroll + u + 1, last))
                pt = jnp.exp2(s_scr[u % 2]).astype(BF16)
                d = _dot(vt_ref[hd, c0 + u], pt)
                pv = d if pv is None else pv + d
            acc_scr[hd] += pv
            return carry
        lax.fori_loop(0, 2 * nk // unroll, body, 0)

    @pl.when(jnp.logical_not(fast))
    def _():
        for hh in range(2):
            qt = qt_ref[hh]
            m_scr[...] = jnp.full(m_scr.shape, -jnp.inf, F32)

            def body(j, carry, hh=hh, qt=qt):
                kk = k_ref[hh, pl.ds(pl.multiple_of(j * tk, tk), tk), :]
                s = _dot(kk, qt)
                m_prev = m_scr[...]
                m_new = jnp.maximum(m_prev, jnp.max(s, axis=0, keepdims=True))
                pt = jnp.exp2(s - m_new).astype(BF16)
                acc_scr[hh] = jnp.exp2(m_prev - m_new) * acc_scr[hh] + _dot(vt_ref[hh, j], pt)
                m_scr[...] = m_new
                return carry
            lax.fori_loop(0, nk, body, 0)

    outs = []
    for hh in range(2):
        acc = acc_scr[hh]
        o = acc[:V_HEAD] / acc[V_HEAD:V_HEAD + 1]
        ms = jnp.mean(o * o, axis=0, keepdims=True)
        outs.append(o * lax.rsqrt(ms + EPS))
    ot = jnp.concatenate(outs, axis=0)
    o_ref[...] = (ot.T * g_ref[...]).astype(o_ref.dtype)


def _attention(flags, qt, k, vt, g_attn, nb, L):
    H = MLA_HEADS
    tq = min(512, L)
    nk, tk = vt.shape[2], vt.shape[4]
    grid_spec = pltpu.PrefetchScalarGridSpec(
        num_scalar_prefetch=1, grid=(nb, H // 2, L // tq),
        in_specs=[pl.BlockSpec((None, 2, HEAD_PAD, tq), lambda b, h, i, f: (b, h, 0, i)),
                  pl.BlockSpec((None, 2, L, HEAD_PAD), lambda b, h, i, f: (b, h, 0, 0)),
                  pl.BlockSpec((None, 2, nk, V_PAD, tk), lambda b, h, i, f: (b, h, 0, 0, 0)),
                  pl.BlockSpec((1, 2 * V_HEAD), lambda b, h, i, f: (0, h))],
        out_specs=pl.BlockSpec((None, tq, 2 * V_HEAD), lambda b, h, i, f: (b, i, h)),
        scratch_shapes=[pltpu.VMEM((2, V_PAD, tq), F32), pltpu.VMEM((1, tq), F32), pltpu.VMEM((2, tk, tq), F32)])
    unroll = math.gcd(nk, ATTN_UNROLL)
    assert unroll % 2 == 0
    return pl.pallas_call(
        functools.partial(_attn_kernel, nk=nk, tk=tk, unroll=unroll), grid_spec=grid_spec,
        out_shape=jax.ShapeDtypeStruct((nb, L, ATTN_WIDTH), BF16),
        compiler_params=_cparams(("parallel", "parallel", "arbitrary")), name="attention",
    )(flags, qt, k, vt, g_attn)


def _fast_flags(stats):
    H = MLA_HEADS
    qn = jnp.max(stats[:, :, 0:H, :], axis=(1, 3))
    kn = jnp.max(stats[:, :, H:2 * H, :], axis=(1, 3))
    vm = jnp.max(stats[:, :, 2 * H, :], axis=(1, 2))
    ok = jnp.logical_and(qn * kn <= FAST_S_MAX * FAST_S_MAX, (vm <= FAST_V_MAX)[:, None])
    return ok.astype(jnp.int32).reshape(-1)


def _mix_mlp_kernel(x_ref, zh_ref, an_ref, ghy_ref, gsum_ref, wo_ref, gpost_ref, gmpre_ref, wup_ref,
                    wdn_ref, gmpost_ref, out_ref):
    zh = jnp.concatenate(
        [jnp.concatenate([zh_ref[j, g, nl * SUBLANES:(nl + 1) * SUBLANES, :] for j in range(HY_CB)], axis=1)
         for nl in range(zh_ref.shape[2] // SUBLANES) for g in range(DFT_G)], axis=0)
    ms = _dot((zh * zh).astype(BF16), gsum_ref[...]) * (HY_GROUPS / HY_WIDTH)
    hn = (zh * lax.rsqrt(ms + EPS) * ghy_ref[...]).astype(BF16)
    mix = _dot(hn, wo_ref[0:HY_WIDTH, :]) + _dot(an_ref[...], wo_ref[HY_WIDTH:, :])
    x = x_ref[...] + _rms(mix, gpost_ref[...])
    h = _rms(x, gmpre_ref[...]).astype(BF16)
    up = jnp.maximum(_dot(h, wup_ref[...]), 0.0)
    m = _dot((up * up).astype(BF16), wdn_ref[...])
    out_ref[...] = x + _rms(m, gmpost_ref[...])


def _mix_mlp(x, zh, an, lw, tabs, L):
    M, D = x.shape
    tm = min(512, L)
    tps = L // tm
    dff = lw["wup"].shape[1]
    mw = HY_WIDTH + ATTN_WIDTH
    const = lambda *shape: pl.BlockSpec(shape, lambda i: (0,) * len(shape), pipeline_mode=pl.Buffered(1))
    rows = lambda w: pl.BlockSpec((tm, w), lambda i: (i, 0))
    return pl.pallas_call(
        _mix_mlp_kernel, grid=(M // tm,),
        in_specs=[rows(D), pl.BlockSpec((None, HY_CB, DFT_G, tm // DFT_G, LANES),
                                        lambda i: (i // tps, 0, 0, i % tps, 0)),
                  rows(ATTN_WIDTH), const(1, HY_WIDTH), const(HY_WIDTH, HY_WIDTH),
                  const(mw, D), const(1, D), const(1, D), const(D, dff), const(dff, D), const(1, D)],
        out_specs=rows(D), out_shape=jax.ShapeDtypeStruct((M, D), F32),
        compiler_params=_cparams(("parallel",)), name="mix_mlp",
    )(x, zh, an, lw["g_hy"], tabs["gsum"], lw["wo"], lw["g_post"], lw["g_mpre"], lw["wup"], lw["wdn"],
      lw["g_mpost"])


def _tables(L):
    n = 2 * L
    n2 = DFT_N2
    n1 = n // n2
    nh = n1 // 2
    kp = -(-(nh + 1) // SUBLANES) * SUBLANES
    two_pi = 2.0 * math.pi

    k1 = jnp.arange(kp, dtype=jnp.int32)
    valid = (k1 <= nh)
    def outer(ncols):
        nn = jnp.arange(ncols, dtype=jnp.int32)
        ang = ((k1[:, None] * nn[None, :]) % n1).astype(F32) * (two_pi / n1)
        c = jnp.where(valid[:, None], jnp.cos(ang), 0.0)
        s = jnp.where(valid[:, None], -jnp.sin(ang), 0.0)
        return jnp.concatenate([c, s], axis=0).astype(BF16)
    f1d = outer(nh)
    f1f = outer(n1)
    nn = jnp.arange(nh, dtype=jnp.int32)
    ang = ((nn[:, None] * k1[None, :]) % n1).astype(F32) * (two_pi / n1)
    wgt = jnp.where(valid, jnp.where((k1 == 0) | (k1 == nh), 1.0, 2.0), 0.0) / n
    gre = (jnp.cos(ang) * wgt[None, :]).astype(BF16)
    gim = (-jnp.sin(ang) * wgt[None, :]).astype(BF16)
    a2 = jnp.arange(n2, dtype=jnp.int32)
    idx = (a2[None, :, None] * a2[None, None, :] * n1 + a2[None, None, :] * k1[:, None, None]) % n
    ph = idx.astype(F32) * (two_pi / n)
    gr, gi = jnp.cos(ph), -jnp.sin(ph)
    m1 = jnp.concatenate([jnp.concatenate([gr, -gi], axis=2), jnp.concatenate([gi, gr], axis=2)], axis=1)
    m2 = jnp.swapaxes(m1, 1, 2)
    inv = 1.0 / (ROPE_BASE ** (jnp.arange(0, QK_ROPE, 2, dtype=F32) / QK_ROPE))
    ang = jnp.arange(L, dtype=F32)[:, None] * inv[None, :]
    cos, sin = jnp.cos(ang), jnp.sin(ang)
    rcos = jnp.concatenate([cos, cos], axis=1)
    rsin = jnp.concatenate([sin, sin], axis=1)
    kcs =jnp.concatenate([cos, cos, sin, sin, jnp.zeros((L, HEAD_PAD - 2 * QK_ROPE), F32)], axis=1)
    cc = jnp.arange(MLA_HEADS * HEAD_PAD)
    src = jnp.arange(HEAD_PAD)
    pk = ((cc[None, :] % HEAD_PAD) - QK_NOPE == src[:, None]) & (src[:, None] < QK_ROPE)
    grp = jnp.arange(HY_WIDTH) // (HY_WIDTH // HY_GROUPS)
    gsum = (grp[:, None] == grp[None, :]).astype(BF16)
    t = jnp.linspace(0.0, 1.0, L, dtype=F32)[:, None]
    omega = (two_pi / L) * jnp.arange(L, dtype=F32)
    bands = jnp.linspace(1e-4, HY_BANDS - 1, HY_BANDS, dtype=F32)
    phase = omega[:, None] * bands[None, :]
    z = jnp.concatenate([t, jnp.cos(phase), -jnp.sin(phase), jnp.zeros((L, HY_EMB_PAD - HY_EMB), F32)], axis=-1)
    ztab = jnp.concatenate([z, z[:1], z[:0:-1]], axis=0)
    return dict(f1d=f1d, f1f=f1f, gre=gre, gim=gim, m1=m1.astype(BF16), m2=m2.astype(BF16), rcos_t=rcos.T,
                rsin_t=rsin.T, kcs=kcs, pk=pk.astype(BF16), gsum=gsum, ztab=ztab)


def _rot_half_cols(w):
    half = QK_ROPE // 2
    return jnp.concatenate([-w[..., half:], w[..., :half]], axis=-1)


def _layer_weights(i, p, tabs):
    D = p["w_in"].shape[1]
    H = MLA_HEADS
    hw3 = 3 * HY_WIDTH
    w_in = p["w_in"][i]
    kpe = w_in[:, hw3 + Q_RANK + KV_RANK:]
    win = jnp.concatenate([w_in, _rot_half_cols(kpe),
                           jnp.zeros((D, HEAD_PAD - 2 * QK_ROPE), F32)], axis=1).astype(BF16)
    dq = QK_NOPE + QK_ROPE
    wq = p["mla_w_uq"][i].reshape(Q_RANK, H, dq)
    wqt = jnp.concatenate([wq.reshape(Q_RANK, H * dq),
                           _rot_half_cols(wq[..., QK_NOPE:]).reshape(Q_RANK, H * QK_ROPE)], axis=1).T
    wkv =p["mla_w_ukv"][i].reshape(KV_RANK, H, QK_NOPE + V_HEAD)
    wk = jnp.concatenate([wkv[..., :QK_NOPE], jnp.zeros((KV_RANK, H, HEAD_PAD - QK_NOPE), F32)], axis=2)
    wv = jnp.concatenate([wkv[..., QK_NOPE:], jnp.zeros((KV_RANK, H, V_PAD - V_HEAD), F32)], axis=2)
    row = lambda a: a.reshape(1, -1)
    return dict(
        win=win, g_pre=row(p["norm_mix_pre"][i]), conv_w=p["hy_conv_w"][i], conv_b=row(p["hy_conv_b"][i]),
        q_g=row(p["mla_q_norm"][i]), wqt=wqt.astype(BF16), kv_g=row(p["mla_kv_norm"][i]),
        wkp=jnp.concatenate([wk.reshape(KV_RANK, H * HEAD_PAD).astype(BF16), tabs["pk"]], axis=0),
        wvt=wv.reshape(KV_RANK, H * V_PAD).T.astype(BF16),
        g_hy=row(p["grp_norm_hy"][i]), g_attn=row(p["grp_norm_attn"][i]), wo=p["w_out"][i].astype(BF16),
        g_post=row(p["norm_mix_post"][i]), g_mpre=row(p["norm_mlp_pre"][i]), g_mpost=row(p["norm_mlp_post"][i]),
        wup=p["w_mlp_up"][i].astype(BF16), wdn=p["w_mlp_down"][i].astype(BF16),
    )


def _filter_weights(p):
    depth = p["hy_ffn_w1"].shape[0]
    oc = HY_ORDER * HY_WIDTH
    w1 = jnp.pad(p["hy_ffn_w1"], ((0, 0), (0, HY_EMB_PAD - HY_EMB), (0, 0)))
    w3 = p["hy_ffn_w3"].reshape(depth, HY_FFN, HY_ORDER, 2, HY_WIDTH).transpose(0, 3, 1, 2, 4)
    dec = p["hy_decay"].transpose(0, 2, 1, 3).reshape(depth, 2, 1, oc)
    tr = lambda a: jnp.swapaxes(a, 1, 2)
    w3 = w3.reshape(depth, 2, HY_FFN, oc)
    w3_hi = w3.astype(BF16)
    w3_lo = (w3 - w3_hi.astype(F32)).astype(BF16)
    return dict(w1=tr(w1), b1=p["hy_ffn_b1"][:, :, None], sf=tr(p["hy_sin_freq"]), w2=tr(p["hy_ffn_w2"]),
                b2=p["hy_ffn_b2"][:, :, None], w3=jnp.stack([w3_hi, w3_lo], axis=2), dec=dec)


def kernel(x_prompt, x_sample, w_in, hy_conv_w, hy_conv_b, hy_ffn_w1, hy_ffn_b1, hy_ffn_w2, hy_ffn_b2,
           hy_ffn_w3, hy_sin_freq, hy_decay, hy_bias, mla_q_norm, mla_w_uq, mla_kv_norm, mla_w_ukv,
           grp_norm_hy, grp_norm_attn, w_out, norm_mix_pre, norm_mix_post, norm_mlp_pre, norm_mlp_post,
           w_mlp_up, w_mlp_down):
    p = dict(w_in=w_in, hy_conv_w=hy_conv_w, hy_conv_b=hy_conv_b, hy_ffn_w1=hy_ffn_w1, hy_ffn_b1=hy_ffn_b1,
             hy_ffn_w2=hy_ffn_w2, hy_ffn_b2=hy_ffn_b2, hy_ffn_w3=hy_ffn_w3, hy_sin_freq=hy_sin_freq,
             hy_decay=hy_decay, hy_bias=hy_bias, mla_q_norm=mla_q_norm, mla_w_uq=mla_w_uq,
             mla_kv_norm=mla_kv_norm, mla_w_ukv=mla_w_ukv, grp_norm_hy=grp_norm_hy,
             grp_norm_attn=grp_norm_attn, w_out=w_out, norm_mix_pre=norm_mix_pre, norm_mix_post=norm_mix_post,
             norm_mlp_pre=norm_mlp_pre, norm_mlp_post=norm_mlp_post, w_mlp_up=w_mlp_up, w_mlp_down=w_mlp_down)
    bp, L, D = x_prompt.shape
    bs, Ls, _ = x_sample.shape
    assert L == Ls and L % (DFT_N2 * SUBLANES) == 0
    nb = bp + bs
    depth = w_in.shape[0]

    tabs = _tables(L)
    kc = _filters(tabs["ztab"], _filter_weights(p), L)
    kf = _filter_spectrum(_dft1_pairs(kc, tabs["f1f"]), tabs["m1"])

    x = jnp.concatenate([x_prompt.reshape(bp * L, D), x_sample.reshape(bs * L, D)], axis=0)
    for i in range(depth):
        lw = _layer_weights(i, p, tabs)
        v, x1, x2, qt, k, vt, stats = _inproj(x, lw, tabs, nb, L)
        z = _hyena_convs(v, x1, x2, kf, hy_bias[i], tabs, i)
        an = _attention(_fast_flags(stats), qt, k, vt, lw["g_attn"], nb, L).reshape(nb * L, ATTN_WIDTH)
        x = _mix_mlp(x, z, an, lw, tabs, L)
    return (x[:bp * L].reshape(bp, L, D), x[bp * L:].reshape(bs, L, D))
```

```python
import functools
import math

import jax
import jax.numpy as jnp
from jax import lax
from jax.experimental import pallas as pl
from jax.experimental.pallas import tpu as pltpu

F32 = jnp.float32
BF16 = jnp.bfloat16

EPS = 1e-6
HY_WIDTH = 512
HY_GROUPS = 8
HY_ORDER = 2
HY_BANDS = 16
HY_EMB = 2 * HY_BANDS + 1
HY_EMB_PAD = 40
HY_FFN = 64
MLA_HEADS = 8
QK_NOPE = 64
QK_ROPE = 32
V_HEAD = 64
Q_RANK = 256
KV_RANK = 128
ROPE_BASE = 10000.0
HEAD_PAD = 128
V_PAD = 128
ATTN_WIDTH = MLA_HEADS * V_HEAD

LANES = 128
SUBLANES = 8
HY_CB = HY_WIDTH // LANES
DFT_N2 = 128
DFT_G = DFT_N2 // SUBLANES
MID_CB = 2
OUTER_GB = 4
HALO = 16
VMEM_LIMIT = 56 * 1024 * 1024
ATTN_UNROLL = 32
STAT_ROWS = 24

FAST_S_MAX = 64.0
FAST_V_MAX = 2.0 ** 30


def _cparams(sem):
    return pltpu.CompilerParams(dimension_semantics=sem, vmem_limit_bytes=VMEM_LIMIT)


def _dot(a, b):
    return jnp.dot(a, b, preferred_element_type=F32)


def _dot_nt(a, b):
    return lax.dot_general(a, b, (((1,), (1,)), ((), ())), preferred_element_type=F32)


def _rms(x, g):
    return x * lax.rsqrt(jnp.mean(x * x, axis=-1, keepdims=True) + EPS) * g


def _store_grouped(out_ref, val, lane0, ncb):
    for j in range(ncb):
        for nl in range(val.shape[0] // DFT_N2):
            for g in range(DFT_G):
                r0 = nl * DFT_N2 + g * SUBLANES
                out_ref[j, g, nl * SUBLANES:(nl + 1) * SUBLANES, :] = (
                    val[r0:r0 + SUBLANES, lane0 + j * LANES:lane0 + (j + 1) * LANES])


def _fold_lanes(row):
    parts = [row[:, j * LANES:(j + 1) * LANES] for j in range(row.shape[1] // LANES)]
    return functools.reduce(jnp.maximum, parts)


def _inproj_kernel(xp_ref, x_ref, xn_ref, gpre_ref, win_ref, cw_ref, cb_ref, qg_ref, wqt_ref,
                   kvg_ref, wkp_ref, wvt_ref, qcos_ref, qsin_ref, kcs_ref,
                   v_out, x1_out, x2_out, qt_out, k_out, vt_out, st_out, h_scr, pe_scr,
                   *, tm, tiles_per_seq, scale):
    i = pl.program_id(0)
    t_idx = i % tiles_per_seq
    g = gpre_ref[...]
    h_scr[0:HALO, :] = _rms(xp_ref[...], g).astype(BF16)
    h_scr[HALO:HALO + tm, :] = _rms(x_ref[...], g).astype(BF16)
    h_scr[HALO + tm:2 * HALO + tm, :] = _rms(xn_ref[...], g).astype(BF16)
    hw3 = 3 * HY_WIDTH
    core = _dot(h_scr[HALO:HALO + tm, :], win_ref[:, hw3:])
    pe_scr[...] = _dot(h_scr[...], win_ref[:, 0:hw3])
    row = lax.broadcasted_iota(jnp.int32, (tm, 1), 0)
    prev = pe_scr[HALO - 1:HALO - 1 + tm, :]
    cur = pe_scr[HALO:HALO + tm, :]
    nxt = pe_scr[HALO + 1:HALO + 1 + tm, :]
    prev = jnp.where(jnp.logical_and(row == 0, t_idx == 0), 0.0, prev)
    nxt = jnp.where(jnp.logical_and(row == tm - 1, t_idx == tiles_per_seq - 1), 0.0, nxt)
    u = prev * cw_ref[0:1, :] + cur * cw_ref[1:2, :] + nxt * cw_ref[2:3, :] + cb_ref[...]
    for o, out in enumerate((v_out, x1_out, x2_out)):
        _store_grouped(out, u, o * HY_WIDTH, HY_CB)

    cq = core[:, 0:Q_RANK]
    cqn =_rms(cq, qg_ref[...]).astype(BF16)
    dq = QK_NOPE + QK_ROPE
    qall = _dot_nt(wqt_ref[...], cqn)
    rcos = qcos_ref[...]
    rsin = qsin_ref[...]
    zpad = jnp.zeros((HEAD_PAD - dq, tm), F32)
    for h in range(MLA_HEADS):
        rot = qall[MLA_HEADS * dq + h * QK_ROPE:MLA_HEADS * dq + (h + 1) * QK_ROPE]
        rope = qall[h * dq + QK_NOPE:(h + 1) * dq] * rcos + rot * rsin
        qh = (jnp.concatenate([qall[h * dq:h * dq + QK_NOPE], rope, zpad], axis=0) * scale).astype(BF16)
        qt_out[h] = qh
        qf = qh.astype(F32)
        st_out[h:h + 1, :] = _fold_lanes(jnp.sum(qf * qf, axis=0, keepdims=True))

    ckv = core[:, Q_RANK:Q_RANK + KV_RANK]
    ckvn = _rms(ckv, kvg_ref[...]).astype(BF16)
    kp = core[:, Q_RANK + KV_RANK:] * kcs_ref[...]
    kp = kp + pltpu.roll(kp, HEAD_PAD - QK_ROPE, axis=1)
    kk = _dot(jnp.concatenate([ckvn, kp.astype(BF16)], axis=1), wkp_ref[...])
    for h in range(MLA_HEADS):
        kh = kk[:, h * HEAD_PAD:(h + 1) * HEAD_PAD].astype(BF16)
        k_out[h] = kh
        kf = kh.astype(F32)
        kn = jnp.max(jnp.sum(kf * kf, axis=1, keepdims=True), axis=0, keepdims=True)
        st_out[MLA_HEADS + h:MLA_HEADS + h + 1, :] = jnp.broadcast_to(kn, (1, LANES))
    vt = _dot_nt(wvt_ref[...], ckvn)
    st_out[2 * MLA_HEADS:2 * MLA_HEADS + 1, :] = _fold_lanes(jnp.max(jnp.abs(vt), axis=0, keepdims=True))
    st_out[2 * MLA_HEADS + 1:, :] = jnp.zeros((STAT_ROWS - 2 * MLA_HEADS - 1, LANES), F32)
    frow = lax.broadcasted_iota(jnp.int32, (MLA_HEADS * V_PAD, 1), 0)
    vt = vt + jnp.where(frow % V_PAD == V_HEAD, 1.0, 0.0)
    for h in range(MLA_HEADS):
        vt_out[h] = vt[h * V_PAD:(h + 1) * V_PAD, :].astype(BF16)


def _inproj(x, lw, tabs, nb, L):
    M, D = x.shape
    tm = min(512, L)
    tps = L // tm
    nt = M // tm
    hb = tm // HALO
    nhb = M // HALO
    H = MLA_HEADS
    wcols = lw["win"].shape[1]
    const = lambda *shape: pl.BlockSpec(shape, lambda i: (0,) * len(shape))
    tab_t = pl.BlockSpec((QK_ROPE, tm), lambda i: (0, i % tps))
    in_specs = [
        pl.BlockSpec((HALO, D), lambda i: (jnp.maximum(i * hb - 1, 0), 0)),
        pl.BlockSpec((tm, D), lambda i: (i, 0)),
        pl.BlockSpec((HALO, D), lambda i: (jnp.minimum((i + 1) * hb, nhb - 1), 0)),
        const(1, D), const(D, wcols), const(3, 3 * HY_WIDTH), const(1, 3 * HY_WIDTH),
        const(1, Q_RANK), const(H * HEAD_PAD, Q_RANK),
        const(1, KV_RANK), const(KV_RANK + HEAD_PAD, H * HEAD_PAD),
        const(H * V_PAD, KV_RANK),
        tab_t, tab_t, pl.BlockSpec((tm, HEAD_PAD), lambda i: (i % tps, 0)),
    ]
    hy_spec = pl.BlockSpec((None, HY_CB, DFT_G, tm // DFT_G, LANES), lambda i: (i // tps, 0, 0, i % tps, 0))
    out_specs = [
        hy_spec, hy_spec, hy_spec,
        pl.BlockSpec((None, H, HEAD_PAD, tm), lambda i: (i // tps, 0, 0, i % tps)),
        pl.BlockSpec((None, H, tm, HEAD_PAD), lambda i: (i // tps, 0, i % tps, 0)),
        pl.BlockSpec((None, H, None, V_PAD, tm), lambda i: (i // tps, 0, i % tps, 0, 0)),
        pl.BlockSpec((None, None, STAT_ROWS, LANES), lambda i: (i // tps, i % tps, 0, 0)),
    ]
    hy_shape = jax.ShapeDtypeStruct((nb, HY_CB, DFT_G, L // DFT_G, LANES), F32)
    out_shape = [
        hy_shape, hy_shape, hy_shape,
        jax.ShapeDtypeStruct((nb, H, HEAD_PAD, L), BF16),
        jax.ShapeDtypeStruct((nb, H, L, HEAD_PAD), BF16),
        jax.ShapeDtypeStruct((nb, H, tps, V_PAD, tm), BF16),
        jax.ShapeDtypeStruct((nb, tps, STAT_ROWS, LANES), F32),
    ]
    scale = float((QK_NOPE + QK_ROPE) ** -0.5 * math.log2(math.e))
    return pl.pallas_call(
        functools.partial(_inproj_kernel, tm=tm, tiles_per_seq=tps, scale=scale),
        grid=(nt,), in_specs=in_specs, out_specs=out_specs, out_shape=out_shape,
        scratch_shapes=[pltpu.VMEM((tm + 2 * HALO, D), BF16),
                        pltpu.VMEM((tm + 2 * HALO, 3 * HY_WIDTH), F32)],
        compiler_params=_cparams(("parallel",)), name="inproj",
    )(x, x, x, lw["g_pre"], lw["win"], lw["conv_w"], lw["conv_b"], lw["q_g"], lw["wqt"],
      lw["kv_g"], lw["wkp"], lw["wvt"], tabs["rcos_t"], tabs["rsin_t"], tabs["kcs"])


def _filter_kernel(t_ref, zt_ref, w1_ref, b1_ref, sf_ref, w2_ref, b2_ref, w3_ref, dec_ref, out_ref, *, rb, L):
    hi = lax.Precision.HIGHEST
    r = pl.program_id(1)
    h = jnp.sin(sf_ref[:, 0:1] * (jnp.dot(w1_ref[...], zt_ref[...], precision=hi, preferred_element_type=F32)
                                  + b1_ref[...]))
    h = jnp.sin(sf_ref[:, 1:2] * (jnp.dot(w2_ref[...], h, precision=hi, preferred_element_type=F32) + b2_ref[...]))
    tn = lambda a, b: lax.dot_general(a, b, (((0,), (0,)), ((), ())), preferred_element_type=F32)
    h_hi = h.astype(BF16)
    h_lo = (h - h_hi.astype(F32)).astype(BF16)
    k = tn(h_hi, w3_ref[0]) + (tn(h_hi, w3_ref[1]) + tn(h_lo, w3_ref[0]))
    k = k * jnp.exp(-t_ref[...] * jnp.abs(dec_ref[...]))
    row = r * rb + lax.broadcasted_iota(jnp.int32, (rb, 1), 0)
    k = jnp.where(row == L, 0.0, k)
    npair = out_ref.shape[0]
    w = jnp.concatenate([_pack_pair(k[:, 2 * j * LANES:(2 * j + 1) * LANES], k[:, (2 * j + 1) * LANES:(2 * j + 2) * LANES])
                         for j in range(npair)], axis=1)
    _store_grouped(out_ref, w, 0, npair)


def _filters(ztab, fw, L):
    depth = fw["w1"].shape[0]
    rows = 2 * L
    rb = min(512, L)
    nblk = rows // rb
    half = nblk // 2
    oc = HY_ORDER * HY_WIDTH
    lay = lambda *shape: pl.BlockSpec((None,) + shape, lambda l, r: (l,) + (0,) * len(shape))
    in_specs = [
        pl.BlockSpec((rb, 1), lambda l, r: (r, 0)), pl.BlockSpec((HY_EMB_PAD, rb), lambda l, r: (0, r)),
        lay(HY_FFN, HY_EMB_PAD), lay(HY_FFN, 1), lay(HY_FFN, 2), lay(HY_FFN, HY_FFN), lay(HY_FFN, 1),
        pl.BlockSpec((None, None, 2, HY_FFN, oc), lambda l, r: (l, r // half, 0, 0, 0)),
        pl.BlockSpec((None, None, 1, oc), lambda l, r: (l, r // half, 0, 0)),
    ]
    return pl.pallas_call(
        functools.partial(_filter_kernel, rb=rb, L=L),
        grid=(depth, nblk), in_specs=in_specs,
        out_specs=pl.BlockSpec((None, oc // LANES // 2, DFT_G, rb // DFT_G, LANES), lambda l, r: (l, 0, 0, r, 0)),
        out_shape=jax.ShapeDtypeStruct((depth, oc // LANES // 2, DFT_G, rows // DFT_G, LANES), jnp.uint32),
        compiler_params=_cparams(("parallel", "parallel")), name="hyena_filter",
    )(ztab[:, 0:1], ztab.T, fw["w1"], fw["b1"], fw["sf"], fw["w2"], fw["b2"], fw["w3"], fw["dec"])


def _rows_of(ref, p, n, lead=()):
    return ref[lead + (pl.ds(p, n, stride=SUBLANES), slice(None))]


def _pack_pair(re, im):
    hi = lax.bitcast_convert_type(re.astype(BF16).astype(F32), jnp.uint32)
    lo = lax.bitcast_convert_type(im.astype(BF16).astype(F32), jnp.uint32)
    return hi | (lo >> 16)


def _unpack_pair(w, dtype=BF16):
    re = lax.bitcast_convert_type(w & jnp.uint32(0xFFFF0000), F32)
    im = lax.bitcast_convert_type(w << 16, F32)
    return re.astype(dtype), im.astype(dtype)


def _dft1_block(f_ref, xs, out_ref, gg):
    kp = f_ref.shape[0] // 2
    res = _dot(f_ref[...], jnp.concatenate([x.astype(BF16) for x in xs], axis=1))
    w = _pack_pair(res[:kp], res[kp:])
    for kg in range(kp // SUBLANES):
        for p in range(SUBLANES):
            r0 = (gg * SUBLANES + p) * SUBLANES
            out_ref[kg, r0:r0 + SUBLANES, :] = w[kg * SUBLANES:(kg + 1) * SUBLANES, p * LANES:(p + 1) * LANES]


def _dft1_kernel(f_ref, x_ref, out_ref):
    R = f_ref.shape[1]
    for gg in range(OUTER_GB):
        _dft1_block(f_ref, [_rows_of(x_ref, p, R, (gg,)) for p in range(SUBLANES)], out_ref, gg)


def _dft1_spec(kg):
    return pl.BlockSpec((None, None, kg, OUTER_GB * SUBLANES * SUBLANES, LANES), lambda b, c, g: (b, c, 0, g, 0))


def _dft1(x, f1):
    nb, ncb, _, rows, _ = x.shape
    kp = f1.shape[0] // 2
    kg = kp // SUBLANES
    return pl.pallas_call(
        _dft1_kernel, grid=(nb, ncb, DFT_G // OUTER_GB),
        in_specs=[pl.BlockSpec(f1.shape, lambda b, c, g: (0, 0)),
                  pl.BlockSpec((None, None, OUTER_GB, rows, LANES), lambda b, c, g: (b, c, g, 0, 0))],
        out_specs=_dft1_spec(kg),
        out_shape=jax.ShapeDtypeStruct((nb, ncb, kg, DFT_N2 * SUBLANES, LANES), jnp.uint32),
        compiler_params=_cparams(("parallel", "parallel", "parallel")), name="dft_outer",
    )(f1, x)


def _dft1_pairs_kernel(f_ref, x_ref, out_ref):
    R = f_ref.shape[1]
    for gg in range(OUTER_GB):
        pairs = [_unpack_pair(_rows_of(x_ref, p, R, (gg,))) for p in range(SUBLANES)]
        for c in range(2):
            _dft1_block(f_ref, [pr[c] for pr in pairs], out_ref.at[c], gg)


def _dft1_pairs(x, f1):
    nb, npair, _, rows, _ = x.shape
    kg = f1.shape[0] // 2 // SUBLANES
    return pl.pallas_call(
        _dft1_pairs_kernel, grid=(nb, npair, DFT_G // OUTER_GB),
        in_specs=[pl.BlockSpec(f1.shape, lambda b, c, g: (0, 0)),
                  pl.BlockSpec((None, None, OUTER_GB, rows, LANES), lambda b, c, g: (b, c, g, 0, 0))],
        out_specs=pl.BlockSpec((None, 2, kg, OUTER_GB * SUBLANES * SUBLANES, LANES), lambda b, c, g: (b, c, 0, g, 0)),
        out_shape=jax.ShapeDtypeStruct((nb, 2 * npair, kg, DFT_N2 * SUBLANES, LANES), jnp.uint32),
        compiler_params=_cparams(("parallel", "parallel", "parallel")), name="dft_outer_filter",
    )(f1, x)


def _inner_fwd(m1_ref, a_ref, kk):
    w = jnp.concatenate([_rows_of(a_ref, kk, DFT_N2, (c,)) for c in range(a_ref.shape[0])], axis=1)
    ar, ai = _unpack_pair(w)
    return _dot(m1_ref[kk], jnp.concatenate([ar, ai], axis=0))


def _spec_kernel(m1_ref, a_ref, out_ref):
    for kk in range(SUBLANES):
        x = _inner_fwd(m1_ref, a_ref, kk)
        for c in range(a_ref.shape[0]):
            out_ref[c, kk] = _pack_pair(x[:DFT_N2, c * LANES:(c + 1) * LANES], x[DFT_N2:, c * LANES:(c + 1) * LANES])


def _filter_spectrum(a, m1):
    depth, ncb, kg, rows, _ = a.shape
    n2 = DFT_N2
    return pl.pallas_call(
        _spec_kernel, grid=(kg, depth, ncb // MID_CB),
        in_specs=[pl.BlockSpec((SUBLANES, 2 * n2, 2 * n2), lambda k, l, c: (k, 0, 0)),
                  pl.BlockSpec((None, MID_CB, None, rows, LANES), lambda k, l, c: (l, c, k, 0, 0))],
        out_specs=pl.BlockSpec((None, MID_CB, SUBLANES, n2, LANES), lambda k, l, c: (l, c, k, 0, 0)),
        out_shape=jax.ShapeDtypeStruct((depth, ncb, kg * SUBLANES, n2, LANES), jnp.uint32),
        compiler_params=_cparams(("parallel", "parallel", "parallel")), name="filter_spectrum",
    )(m1, a)


def _mid_kernel(m1_ref, m2_ref, kf_ref, a_ref, out_ref):
    ncb = a_ref.shape[0]
    lanes = lambda parts: jnp.concatenate(parts, axis=1)
    for kk in range(SUBLANES):
        x = _inner_fwd(m1_ref, a_ref, kk)
        xr, xi = x[:DFT_N2], x[DFT_N2:]
        kr, ki = _unpack_pair(lanes([kf_ref[c, kk] for c in range(ncb)]), F32)
        yr = (xr * kr - xi * ki).astype(BF16)
        yi = (xr * ki + xi * kr).astype(BF16)
        y = _dot(m2_ref[kk], jnp.concatenate([yr, yi], axis=0))
        w = _pack_pair(y[:DFT_N2], y[DFT_N2:])
        for c in range(ncb):
            for g in range(DFT_G):
                out_ref[c, g, kk * SUBLANES:(kk + 1) * SUBLANES, :] = (
                    w[g * SUBLANES:(g + 1) * SUBLANES, c * LANES:(c + 1) * LANES])


def _conv_mid(a, kf, m1, m2, layer, order):
    nb, ncb, kg, rows, _ = a.shape
    n2 = DFT_N2
    mspec = pl.BlockSpec((SUBLANES, 2 * n2, 2 * n2), lambda k, c, b: (k, 0, 0))
    return pl.pallas_call(
        _mid_kernel, grid=(kg, ncb // MID_CB, nb),
        in_specs=[mspec, mspec,
                  pl.BlockSpec((None, MID_CB, SUBLANES, n2, LANES),
                               lambda k, c, b: (layer, order * (ncb // MID_CB) + c, k, 0, 0)),
                  pl.BlockSpec((None, MID_CB, None, rows, LANES), lambda k, c, b: (b, c, k, 0, 0))],
        out_specs=pl.BlockSpec((None, MID_CB, DFT_G, SUBLANES * SUBLANES, LANES), lambda k, c, b: (b, c, 0, k, 0)),
        out_shape=jax.ShapeDtypeStruct((nb, ncb, DFT_G, kg * SUBLANES * SUBLANES, LANES), jnp.uint32),
        compiler_params=_cparams(("parallel", "parallel", "arbitrary")), name="conv_mid",
    )(m1, m2, kf, a)


def _gate_kernel(gre_ref, gim_ref, f_ref, b_ref, z_ref, gate_ref, bias_ref, out_ref, *next_ref):
    R, kp = gre_ref.shape
    bias = bias_ref[...]
    for gg in range(OUTER_GB):
        bre, bim = _unpack_pair(jnp.concatenate([_rows_of(b_ref, p, kp, (gg,)) for p in range(SUBLANES)], axis=1))
        y = _dot(gre_ref[...], bre) + _dot(gim_ref[...], bim)
        vals = []
        for p in range(SUBLANES):
            yp = y[:, p * LANES:(p + 1) * LANES]
            vals.append(_rows_of(gate_ref, p, R, (gg,)) * (yp + _rows_of(z_ref, p, R, (gg,)) * bias))
            out_ref[gg, pl.ds(p, R, stride=SUBLANES), :] = vals[p]
        if next_ref:
            _dft1_block(f_ref, vals, next_ref[0], gg)


def _conv_out(bsp, tabs, z, gate, bias, with_next):
    nb, ncb, _, krows, _ = bsp.shape
    rows = z.shape[3]
    gre, gim, f1 = tabs["gre"], tabs["gim"], tabs["f1d"]
    kg = f1.shape[0] // 2 // SUBLANES
    full = lambda t: pl.BlockSpec(t.shape, lambda b, c, g: (0, 0))
    tile = pl.BlockSpec((None, None, OUTER_GB, rows, LANES), lambda b, c, g: (b, c, g, 0, 0))
    out_specs, out_shape = [tile], [jax.ShapeDtypeStruct(z.shape, F32)]
    if with_next:
        out_specs.append(_dft1_spec(kg))
        out_shape.append(jax.ShapeDtypeStruct((nb, ncb, kg, DFT_N2 * SUBLANES, LANES), jnp.uint32))
    return pl.pallas_call(
        _gate_kernel, grid=(nb, ncb, DFT_G // OUTER_GB),
        in_specs=[full(gre), full(gim), full(f1),
                  pl.BlockSpec((None, None, OUTER_GB, krows, LANES), lambda b, c, g: (b, c, g, 0, 0)),
                  tile, tile, pl.BlockSpec((None, 1, LANES), lambda b, c, g: (c, 0, 0))],
        out_specs=out_specs, out_shape=out_shape,
        compiler_params=_cparams(("parallel", "parallel", "parallel")), name="conv_gate",
    )(gre, gim, f1, bsp, z, gate, bias)


def _hyena_convs(v, x1, x2, kf, bias, tabs, layer):
    b0, b1 = bias[0].reshape(HY_CB, 1, LANES), bias[1].reshape(HY_CB, 1, LANES)
    bsp = _conv_mid(_dft1(v, tabs["f1d"]), kf, tabs["m1"], tabs["m2"], layer, 0)
    z1, a2 = _conv_out(bsp, tabs, v, x1, b0, True)
    bsp = _conv_mid(a2, kf, tabs["m1"], tabs["m2"], layer, 1)
    return _conv_out(bsp, tabs, z1, x2, b1, False)[0]


def _attn_kernel(flag_ref, qt_ref, k_ref, vt_ref, g_ref, o_ref, acc_scr, m_scr, s_scr, *, nk, tk, unroll):
    pair = pl.program_id(0) * pl.num_programs(1) + pl.program_id(1)
    fast = jnp.logical_and(flag_ref[2 * pair] == 1, flag_ref[2 * pair + 1] == 1)
    acc_scr[...] = jnp.zeros(acc_scr.shape, F32)

    @pl.when(fast)
    def _():
        last = 2 * nk - 1

        def scores(cc):
            hd = cc // nk
            c = cc - hd * nk
            return _dot(k_ref[hd, pl.ds(pl.multiple_of(c * tk, tk), tk), :], qt_ref[hd])

        s_scr[0] = scores(0)

        def body(j, carry):
            hd = (j * unroll) // nk
            c0 = j * unroll - hd * nk
            pv = None
            for u in range(unroll):
                s_scr[(u + 1) % 2] = scores(jnp.minimum(j * unroll + u + 1, last))
                pt = jnp.exp2(s_scr[u % 2]).astype(BF16)
                d = _dot(vt_ref[hd, c0 + u], pt)
                pv = d if pv is None else pv + d
            acc_scr[hd] += pv
            return carry
        lax.fori_loop(0, 2 * nk // unroll, body, 0)

    @pl.when(jnp.logical_not(fast))
    def _():
        for hh in range(2):
            qt = qt_ref[hh]
            m_scr[...] = jnp.full(m_scr.shape, -jnp.inf, F32)

            def body(j, carry, hh=hh, qt=qt):
                kk = k_ref[hh, pl.ds(pl.multiple_of(j * tk, tk), tk), :]
                s = _dot(kk, qt)
                m_prev = m_scr[...]
                m_new = jnp.maximum(m_prev, jnp.max(s, axis=0, keepdims=True))
                pt = jnp.exp2(s - m_new).astype(BF16)
                acc_scr[hh] = jnp.exp2(m_prev - m_new) * acc_scr[hh] + _dot(vt_ref[hh, j], pt)
                m_scr[...] = m_new
                return carry
            lax.fori_loop(0, nk, body, 0)

    outs = []
    for hh in range(2):
        acc = acc_scr[hh]
        o = acc[:V_HEAD] / acc[V_HEAD:V_HEAD + 1]
        ms = jnp.mean(o * o, axis=0, keepdims=True)
        outs.append(o * lax.rsqrt(ms + EPS))
    ot = jnp.concatenate(outs, axis=0)
    o_ref[...] = (ot.T * g_ref[...]).astype(o_ref.dtype)


def _attention(flags, qt, k, vt, g_attn, nb, L):
    H = MLA_HEADS
    tq = min(512, L)
    nk, tk = vt.shape[2], vt.shape[4]
    grid_spec = pltpu.PrefetchScalarGridSpec(
        num_scalar_prefetch=1, grid=(nb, H // 2, L // tq),
        in_specs=[pl.BlockSpec((None, 2, HEAD_PAD, tq), lambda b, h, i, f: (b, h, 0, i)),
                  pl.BlockSpec((None, 2, L, HEAD_PAD), lambda b, h, i, f: (b, h, 0, 0)),
                  pl.BlockSpec((None, 2, nk, V_PAD, tk), lambda b, h, i, f: (b, h, 0, 0, 0)),
                  pl.BlockSpec((1, 2 * V_HEAD), lambda b, h, i, f: (0, h))],
        out_specs=pl.BlockSpec((None, tq, 2 * V_HEAD), lambda b, h, i, f: (b, i, h)),
        scratch_shapes=[pltpu.VMEM((2, V_PAD, tq), F32), pltpu.VMEM((1, tq), F32), pltpu.VMEM((2, tk, tq), F32)])
    unroll = math.gcd(nk, ATTN_UNROLL)
    assert unroll % 2 == 0
    return pl.pallas_call(
        functools.partial(_attn_kernel, nk=nk, tk=tk, unroll=unroll), grid_spec=grid_spec,
        out_shape=jax.ShapeDtypeStruct((nb, L, ATTN_WIDTH), BF16),
        compiler_params=_cparams(("parallel", "parallel", "arbitrary")), name="attention",
    )(flags, qt, k, vt, g_attn)


def _fast_flags(stats):
    H = MLA_HEADS
    qn = jnp.max(stats[:, :, 0:H, :], axis=(1, 3))
    kn = jnp.max(stats[:, :, H:2 * H, :], axis=(1, 3))
    vm = jnp.max(stats[:, :, 2 * H, :], axis=(1, 2))
    ok = jnp.logical_and(qn * kn <= FAST_S_MAX * FAST_S_MAX, (vm <= FAST_V_MAX)[:, None])
    return ok.astype(jnp.int32).reshape(-1)


def _mix_mlp_kernel(x_ref, zh_ref, an_ref, ghy_ref, gsum_ref, wo_ref, gpost_ref, gmpre_ref, wup_ref,
                    wdn_ref, gmpost_ref, out_ref):
    zh = jnp.concatenate(
        [jnp.concatenate([zh_ref[j, g, nl * SUBLANES:(nl + 1) * SUBLANES, :] for j in range(HY_CB)], axis=1)
         for nl in range(zh_ref.shape[2] // SUBLANES) for g in range(DFT_G)], axis=0)
    ms = _dot((zh * zh).astype(BF16), gsum_ref[...]) * (HY_GROUPS / HY_WIDTH)
    hn = (zh * lax.rsqrt(ms + EPS) * ghy_ref[...]).astype(BF16)
    mix = _dot(hn, wo_ref[0:HY_WIDTH, :]) + _dot(an_ref[...], wo_ref[HY_WIDTH:, :])
    x = x_ref[...] + _rms(mix, gpost_ref[...])
    h = _rms(x, gmpre_ref[...]).astype(BF16)
    up = jnp.maximum(_dot(h, wup_ref[...]), 0.0)
    m = _dot((up * up).astype(BF16), wdn_ref[...])
    out_ref[...] = x + _rms(m, gmpost_ref[...])


def _mix_mlp(x, zh, an, lw, tabs, L):
    M, D = x.shape
    tm = min(512, L)
    tps = L // tm
    dff = lw["wup"].shape[1]
    mw = HY_WIDTH + ATTN_WIDTH
    const = lambda *shape: pl.BlockSpec(shape, lambda i: (0,) * len(shape), pipeline_mode=pl.Buffered(1))
    rows = lambda w: pl.BlockSpec((tm, w), lambda i: (i, 0))
    return pl.pallas_call(
        _mix_mlp_kernel, grid=(M // tm,),
        in_specs=[rows(D), pl.BlockSpec((None, HY_CB, DFT_G, tm // DFT_G, LANES),
                                        lambda i: (i // tps, 0, 0, i % tps, 0)),
                  rows(ATTN_WIDTH), const(1, HY_WIDTH), const(HY_WIDTH, HY_WIDTH),
                  const(mw, D), const(1, D), const(1, D), const(D, dff), const(dff, D), const(1, D)],
        out_specs=rows(D), out_shape=jax.ShapeDtypeStruct((M, D), F32),
        compiler_params=_cparams(("parallel",)), name="mix_mlp",
    )(x, zh, an, lw["g_hy"], tabs["gsum"], lw["wo"], lw["g_post"], lw["g_mpre"], lw["wup"], lw["wdn"],
      lw["g_mpost"])


def _tables(L):
    n = 2 * L
    n2 = DFT_N2
    n1 = n // n2
    nh = n1 // 2
    kp = -(-(nh + 1) // SUBLANES) * SUBLANES
    two_pi = 2.0 * math.pi

    k1 = jnp.arange(kp, dtype=jnp.int32)
    valid = (k1 <= nh)
    def outer(ncols):
        nn = jnp.arange(ncols, dtype=jnp.int32)
        ang = ((k1[:, None] * nn[None, :]) % n1).astype(F32) * (two_pi / n1)
        c = jnp.where(valid[:, None], jnp.cos(ang), 0.0)
        s = jnp.where(valid[:, None], -jnp.sin(ang), 0.0)
        return jnp.concatenate([c, s], axis=0).astype(BF16)
    f1d = outer(nh)
    f1f = outer(n1)
    nn = jnp.arange(nh, dtype=jnp.int32)
    ang = ((nn[:, None] * k1[None, :]) % n1).astype(F32) * (two_pi / n1)
    wgt = jnp.where(valid, jnp.where((k1 == 0) | (k1 == nh), 1.0, 2.0), 0.0) / n
    gre = (jnp.cos(ang) * wgt[None, :]).astype(BF16)
    gim = (-jnp.sin(ang) * wgt[None, :]).astype(BF16)
    a2 = jnp.arange(n2, dtype=jnp.int32)
    idx = (a2[None, :, None] * a2[None, None, :] * n1 + a2[None, None, :] * k1[:, None, None]) % n
    ph = idx.astype(F32) * (two_pi / n)
    gr, gi = jnp.cos(ph), -jnp.sin(ph)
    m1 = jnp.concatenate([jnp.concatenate([gr, -gi], axis=2), jnp.concatenate([gi, gr], axis=2)], axis=1)
    m2 = jnp.swapaxes(m1, 1, 2)
    inv = 1.0 / (ROPE_BASE ** (jnp.arange(0, QK_ROPE, 2, dtype=F32) / QK_ROPE))
    ang = jnp.arange(L, dtype=F32)[:, None] * inv[None, :]
    cos, sin = jnp.cos(ang), jnp.sin(ang)
    rcos = jnp.concatenate([cos, cos], axis=1)
    rsin = jnp.concatenate([sin, sin], axis=1)
    kcs =jnp.concatenate([cos, cos, sin, sin, jnp.zeros((L, HEAD_PAD - 2 * QK_ROPE), F32)], axis=1)
    cc = jnp.arange(MLA_HEADS * HEAD_PAD)
    src = jnp.arange(HEAD_PAD)
    pk = ((cc[None, :] % HEAD_PAD) - QK_NOPE == src[:, None]) & (src[:, None] < QK_ROPE)
    grp = jnp.arange(HY_WIDTH) // (HY_WIDTH // HY_GROUPS)
    gsum = (grp[:, None] == grp[None, :]).astype(BF16)
    t = jnp.linspace(0.0, 1.0, L, dtype=F32)[:, None]
    omega = (two_pi / L) * jnp.arange(L, dtype=F32)
    bands = jnp.linspace(1e-4, HY_BANDS - 1, HY_BANDS, dtype=F32)
    phase = omega[:, None] * bands[None, :]
    z = jnp.concatenate([t, jnp.cos(phase), -jnp.sin(phase), jnp.zeros((L, HY_EMB_PAD - HY_EMB), F32)], axis=-1)
    ztab = jnp.concatenate([z, z[:1], z[:0:-1]], axis=0)
    return dict(f1d=f1d, f1f=f1f, gre=gre, gim=gim, m1=m1.astype(BF16), m2=m2.astype(BF16), rcos_t=rcos.T,
                rsin_t=rsin.T, kcs=kcs, pk=pk.astype(BF16), gsum=gsum, ztab=ztab)


def _rot_half_cols(w):
    half = QK_ROPE // 2
    return jnp.concatenate([-w[..., half:], w[..., :half]], axis=-1)


def _layer_weights(i, p, tabs):
    D = p["w_in"].shape[1]
    H = MLA_HEADS
    hw3 = 3 * HY_WIDTH
    w_in = p["w_in"][i]
    kpe = w_in[:, hw3 + Q_RANK + KV_RANK:]
    win = jnp.concatenate([w_in, _rot_half_cols(kpe),
                           jnp.zeros((D, HEAD_PAD - 2 * QK_ROPE), F32)], axis=1).astype(BF16)
    dq = QK_NOPE + QK_ROPE
    wq = p["mla_w_uq"][i].reshape(Q_RANK, H, dq)
    wqt = jnp.concatenate([wq.reshape(Q_RANK, H * dq),
                           _rot_half_cols(wq[..., QK_NOPE:]).reshape(Q_RANK, H * QK_ROPE)], axis=1).T
    wkv =p["mla_w_ukv"][i].reshape(KV_RANK, H, QK_NOPE + V_HEAD)
    wk = jnp.concatenate([wkv[..., :QK_NOPE], jnp.zeros((KV_RANK, H, HEAD_PAD - QK_NOPE), F32)], axis=2)
    wv = jnp.concatenate([wkv[..., QK_NOPE:], jnp.zeros((KV_RANK, H, V_PAD - V_HEAD), F32)], axis=2)
    row = lambda a: a.reshape(1, -1)
    return dict(
        win=win, g_pre=row(p["norm_mix_pre"][i]), conv_w=p["hy_conv_w"][i], conv_b=row(p["hy_conv_b"][i]),
        q_g=row(p["mla_q_norm"][i]), wqt=wqt.astype(BF16), kv_g=row(p["mla_kv_norm"][i]),
        wkp=jnp.concatenate([wk.reshape(KV_RANK, H * HEAD_PAD).astype(BF16), tabs["pk"]], axis=0),
        wvt=wv.reshape(KV_RANK, H * V_PAD).T.astype(BF16),
        g_hy=row(p["grp_norm_hy"][i]), g_attn=row(p["grp_norm_attn"][i]), wo=p["w_out"][i].astype(BF16),
        g_post=row(p["norm_mix_post"][i]), g_mpre=row(p["norm_mlp_pre"][i]), g_mpost=row(p["norm_mlp_post"][i]),
        wup=p["w_mlp_up"][i].astype(BF16), wdn=p["w_mlp_down"][i].astype(BF16),
    )


def _filter_weights(p):
    depth = p["hy_ffn_w1"].shape[0]
    oc = HY_ORDER * HY_WIDTH
    w1 = jnp.pad(p["hy_ffn_w1"], ((0, 0), (0, HY_EMB_PAD - HY_EMB), (0, 0)))
    w3 = p["hy_ffn_w3"].reshape(depth, HY_FFN, HY_ORDER, 2, HY_WIDTH).transpose(0, 3, 1, 2, 4)
    dec = p["hy_decay"].transpose(0, 2, 1, 3).reshape(depth, 2, 1, oc)
    tr = lambda a: jnp.swapaxes(a, 1, 2)
    w3 = w3.reshape(depth, 2, HY_FFN, oc)
    w3_hi = w3.astype(BF16)
    w3_lo = (w3 - w3_hi.astype(F32)).astype(BF16)
    return dict(w1=tr(w1), b1=p["hy_ffn_b1"][:, :, None], sf=tr(p["hy_sin_freq"]), w2=tr(p["hy_ffn_w2"]),
                b2=p["hy_ffn_b2"][:, :, None], w3=jnp.stack([w3_hi, w3_lo], axis=2), dec=dec)


def kernel(x_prompt, x_sample, w_in, hy_conv_w, hy_conv_b, hy_ffn_w1, hy_ffn_b1, hy_ffn_w2, hy_ffn_b2,
           hy_ffn_w3, hy_sin_freq, hy_decay, hy_bias, mla_q_norm, mla_w_uq, mla_kv_norm, mla_w_ukv,
           grp_norm_hy, grp_norm_attn, w_out, norm_mix_pre, norm_mix_post, norm_mlp_pre, norm_mlp_post,
           w_mlp_up, w_mlp_down):
    p = dict(w_in=w_in, hy_conv_w=hy_conv_w, hy_conv_b=hy_conv_b, hy_ffn_w1=hy_ffn_w1, hy_ffn_b1=hy_ffn_b1,
             hy_ffn_w2=hy_ffn_w2, hy_ffn_b2=hy_ffn_b2, hy_ffn_w3=hy_ffn_w3, hy_sin_freq=hy_sin_freq,
             hy_decay=hy_decay, hy_bias=hy_bias, mla_q_norm=mla_q_norm, mla_w_uq=mla_w_uq,
             mla_kv_norm=mla_kv_norm, mla_w_ukv=mla_w_ukv, grp_norm_hy=grp_norm_hy,
             grp_norm_attn=grp_norm_attn, w_out=w_out, norm_mix_pre=norm_mix_pre, norm_mix_post=norm_mix_post,
             norm_mlp_pre=norm_mlp_pre, norm_mlp_post=norm_mlp_post, w_mlp_up=w_mlp_up, w_mlp_down=w_mlp_down)
    bp, L, D = x_prompt.shape
    bs, Ls, _ = x_sample.shape
    assert L == Ls and L % (DFT_N2 * SUBLANES) == 0
    nb = bp + bs
    depth = w_in.shape[0]

    tabs = _tables(L)
    kc = _filters(tabs["ztab"], _filter_weights(p), L)
    kf = _filter_spectrum(_dft1_pairs(kc, tabs["f1f"]), tabs["m1"])

    x = jnp.concatenate([x_prompt.reshape(bp * L, D), x_sample.reshape(bs * L, D)], axis=0)
    for i in range(depth):
        lw = _layer_weights(i, p, tabs)
        v, x1, x2, qt, k, vt, stats = _inproj(x, lw, tabs, nb, L)
        z = _hyena_convs(v, x1, x2, kf, hy_bias[i], tabs, i)
        an = _attention(_fast_flags(stats), qt, k, vt, lw["g_attn"], nb, L).reshape(nb * L, ATTN_WIDTH)
        x = _mix_mlp(x, z, an, lw, tabs, L)
    return (x[:bp * L].reshape(bp, L, D), x[bp * L:].reshape(bs, L, D))
```

```python
import functools
import math

import jax
import jax.numpy as jnp
from jax import lax
from jax.experimental import pallas as pl
from jax.experimental.pallas import tpu as pltpu

F32 = jnp.float32
BF16 = jnp.bfloat16

EPS = 1e-6
HY_WIDTH = 512
HY_GROUPS = 8
HY_ORDER = 2
HY_BANDS = 16
HY_EMB = 2 * HY_BANDS + 1
HY_EMB_PAD = 40
HY_FFN = 64
MLA_HEADS = 8
QK_NOPE = 64
QK_ROPE = 32
V_HEAD = 64
Q_RANK = 256
KV_RANK = 128
ROPE_BASE = 10000.0
HEAD_PAD = 128
V_PAD = 128
ATTN_WIDTH = MLA_HEADS * V_HEAD

LANES = 128
SUBLANES = 8
HY_CB = HY_WIDTH // LANES
DFT_N2 = 128
DFT_G = DFT_N2 // SUBLANES
MID_CB = 2
OUTER_GB = 4
HALO = 16
VMEM_LIMIT = 56 * 1024 * 1024
ATTN_UNROLL = 32
STAT_ROWS = 24

FAST_S_MAX = 64.0
FAST_V_MAX = 2.0 ** 30


def _cparams(sem):
    return pltpu.CompilerParams(dimension_semantics=sem, vmem_limit_bytes=VMEM_LIMIT)


def _dot(a, b):
    return jnp.dot(a, b, preferred_element_type=F32)


def _dot_nt(a, b):
    return lax.dot_general(a, b, (((1,), (1,)), ((), ())), preferred_element_type=F32)


def _rms(x, g):
    return x * lax.rsqrt(jnp.mean(x * x, axis=-1, keepdims=True) + EPS) * g


def _store_grouped(out_ref, val, lane0, ncb):
    for j in range(ncb):
        for nl in range(val.shape[0] // DFT_N2):
            for g in range(DFT_G):
                r0 = nl * DFT_N2 + g * SUBLANES
                out_ref[j, g, nl * SUBLANES:(nl + 1) * SUBLANES, :] = (
                    val[r0:r0 + SUBLANES, lane0 + j * LANES:lane0 + (j + 1) * LANES])


def _fold_lanes(row):
    parts = [row[:, j * LANES:(j + 1) * LANES] for j in range(row.shape[1] // LANES)]
    return functools.reduce(jnp.maximum, parts)


def _inproj_kernel(xp_ref, x_ref, xn_ref, gpre_ref, win_ref, cw_ref, cb_ref, qg_ref, wqt_ref,
                   kvg_ref, wkp_ref, wvt_ref, qcos_ref, qsin_ref, kcs_ref,
                   v_out, x1_out, x2_out, qt_out, k_out, vt_out, st_out, h_scr, pe_scr,
                   *, tm, tiles_per_seq, scale):
    i = pl.program_id(0)
    t_idx = i % tiles_per_seq
    g = gpre_ref[...]
    h_scr[0:HALO, :] = _rms(xp_ref[...], g).astype(BF16)
    h_scr[HALO:HALO + tm, :] = _rms(x_ref[...], g).astype(BF16)
    h_scr[HALO + tm:2 * HALO + tm, :] = _rms(xn_ref[...], g).astype(BF16)
    hw3 = 3 * HY_WIDTH
    core = _dot(h_scr[HALO:HALO + tm, :], win_ref[:, hw3:])
    pe_scr[...] = _dot(h_scr[...], win_ref[:, 0:hw3])
    row = lax.broadcasted_iota(jnp.int32, (tm, 1), 0)
    prev = pe_scr[HALO - 1:HALO - 1 + tm, :]
    cur = pe_scr[HALO:HALO + tm, :]
    nxt = pe_scr[HALO + 1:HALO + 1 + tm, :]
    prev = jnp.where(jnp.logical_and(row == 0, t_idx == 0), 0.0, prev)
    nxt = jnp.where(jnp.logical_and(row == tm - 1, t_idx == tiles_per_seq - 1), 0.0, nxt)
    u = prev * cw_ref[0:1, :] + cur * cw_ref[1:2, :] + nxt * cw_ref[2:3, :] + cb_ref[...]
    for o, out in enumerate((v_out, x1_out, x2_out)):
        _store_grouped(out, u, o * HY_WIDTH, HY_CB)

    cq = core[:, 0:Q_RANK]
    cqn =_rms(cq, qg_ref[...]).astype(BF16)
    dq = QK_NOPE + QK_ROPE
    qall = _dot_nt(wqt_ref[...], cqn)
    rcos = qcos_ref[...]
    rsin = qsin_ref[...]
    zpad = jnp.zeros((HEAD_PAD - dq, tm), F32)
    for h in range(MLA_HEADS):
        rot = qall[MLA_HEADS * dq + h * QK_ROPE:MLA_HEADS * dq + (h + 1) * QK_ROPE]
        rope = qall[h * dq + QK_NOPE:(h + 1) * dq] * rcos + rot * rsin
        qh = (jnp.concatenate([qall[h * dq:h * dq + QK_NOPE], rope, zpad], axis=0) * scale).astype(BF16)
        qt_out[h] = qh
        qf = qh.astype(F32)
        st_out[h:h + 1, :] = _fold_lanes(jnp.sum(qf * qf, axis=0, keepdims=True))

    ckv = core[:, Q_RANK:Q_RANK + KV_RANK]
    ckvn = _rms(ckv, kvg_ref[...]).astype(BF16)
    kp = core[:, Q_RANK + KV_RANK:] * kcs_ref[...]
    kp = kp + pltpu.roll(kp, HEAD_PAD - QK_ROPE, axis=1)
    kk = _dot(jnp.concatenate([ckvn, kp.astype(BF16)], axis=1), wkp_ref[...])
    for h in range(MLA_HEADS):
        kh = kk[:, h * HEAD_PAD:(h + 1) * HEAD_PAD].astype(BF16)
        k_out[h] = kh
        kf = kh.astype(F32)
        kn = jnp.max(jnp.sum(kf * kf, axis=1, keepdims=True), axis=0, keepdims=True)
        st_out[MLA_HEADS + h:MLA_HEADS + h + 1, :] = jnp.broadcast_to(kn, (1, LANES))
    vt = _dot_nt(wvt_ref[...], ckvn)
    st_out[2 * MLA_HEADS:2 * MLA_HEADS + 1, :] = _fold_lanes(jnp.max(jnp.abs(vt), axis=0, keepdims=True))
    st_out[2 * MLA_HEADS + 1:, :] = jnp.zeros((STAT_ROWS - 2 * MLA_HEADS - 1, LANES), F32)
    frow = lax.broadcasted_iota(jnp.int32, (MLA_HEADS * V_PAD, 1), 0)
    vt = vt + jnp.where(frow % V_PAD == V_HEAD, 1.0, 0.0)
    for h in range(MLA_HEADS):
        vt_out[h] = vt[h * V_PAD:(h + 1) * V_PAD, :].astype(BF16)


def _inproj(x, lw, tabs, nb, L):
    M, D = x.shape
    tm = min(512, L)
    tps = L // tm
    nt = M // tm
    hb = tm // HALO
    nhb = M // HALO
    H = MLA_HEADS
    wcols = lw["win"].shape[1]
    const = lambda *shape: pl.BlockSpec(shape, lambda i: (0,) * len(shape))
    tab_t = pl.BlockSpec((QK_ROPE, tm), lambda i: (0, i % tps))
    in_specs = [
        pl.BlockSpec((HALO, D), lambda i: (jnp.maximum(i * hb - 1, 0), 0)),
        pl.BlockSpec((tm, D), lambda i: (i, 0)),
        pl.BlockSpec((HALO, D), lambda i: (jnp.minimum((i + 1) * hb, nhb - 1), 0)),
        const(1, D), const(D, wcols), const(3, 3 * HY_WIDTH), const(1, 3 * HY_WIDTH),
        const(1, Q_RANK), const(H * HEAD_PAD, Q_RANK),
        const(1, KV_RANK), const(KV_RANK + HEAD_PAD, H * HEAD_PAD),
        const(H * V_PAD, KV_RANK),
        tab_t, tab_t, pl.BlockSpec((tm, HEAD_PAD), lambda i: (i % tps, 0)),
    ]
    hy_spec = pl.BlockSpec((None, HY_CB, DFT_G, tm // DFT_G, LANES), lambda i: (i // tps, 0, 0, i % tps, 0))
    out_specs = [
        hy_spec, hy_spec, hy_spec,
        pl.BlockSpec((None, H, HEAD_PAD, tm), lambda i: (i // tps, 0, 0, i % tps)),
        pl.BlockSpec((None, H, tm, HEAD_PAD), lambda i: (i // tps, 0, i % tps, 0)),
        pl.BlockSpec((None, H, None, V_PAD, tm), lambda i: (i // tps, 0, i % tps, 0, 0)),
        pl.BlockSpec((None, None, STAT_ROWS, LANES), lambda i: (i // tps, i % tps, 0, 0)),
    ]
    hy_shape = jax.ShapeDtypeStruct((nb, HY_CB, DFT_G, L // DFT_G, LANES), F32)
    out_shape = [
        hy_shape, hy_shape, hy_shape,
        jax.ShapeDtypeStruct((nb, H, HEAD_PAD, L), BF16),
        jax.ShapeDtypeStruct((nb, H, L, HEAD_PAD), BF16),
        jax.ShapeDtypeStruct((nb, H, tps, V_PAD, tm), BF16),
        jax.ShapeDtypeStruct((nb, tps, STAT_ROWS, LANES), F32),
    ]
    scale = float((QK_NOPE + QK_ROPE) ** -0.5 * math.log2(math.e))
    return pl.pallas_call(
        functools.partial(_inproj_kernel, tm=tm, tiles_per_seq=tps, scale=scale),
        grid=(nt,), in_specs=in_specs, out_specs=out_specs, out_shape=out_shape,
        scratch_shapes=[pltpu.VMEM((tm + 2 * HALO, D), BF16),
                        pltpu.VMEM((tm + 2 * HALO, 3 * HY_WIDTH), F32)],
        compiler_params=_cparams(("parallel",)), name="inproj",
    )(x, x, x, lw["g_pre"], lw["win"], lw["conv_w"], lw["conv_b"], lw["q_g"], lw["wqt"],
      lw["kv_g"], lw["wkp"], lw["wvt"], tabs["rcos_t"], tabs["rsin_t"], tabs["kcs"])


def _filter_kernel(t_ref, zt_ref, w1_ref, b1_ref, sf_ref, w2_ref, b2_ref, w3_ref, dec_ref, out_ref, *, rb, L):
    hi = lax.Precision.HIGHEST
    r = pl.program_id(1)
    h = jnp.sin(sf_ref[:, 0:1] * (jnp.dot(w1_ref[...], zt_ref[...], precision=hi, preferred_element_type=F32)
                                  + b1_ref[...]))
    h = jnp.sin(sf_ref[:, 1:2] * (jnp.dot(w2_ref[...], h, precision=hi, preferred_element_type=F32) + b2_ref[...]))
    tn = lambda a, b: lax.dot_general(a, b, (((0,), (0,)), ((), ())), preferred_element_type=F32)
    h_hi = h.astype(BF16)
    h_lo = (h - h_hi.astype(F32)).astype(BF16)
    k = tn(h_hi, w3_ref[0]) + (tn(h_hi, w3_ref[1]) + tn(h_lo, w3_ref[0]))
    k = k * jnp.exp(-t_ref[...] * jnp.abs(dec_ref[...]))
    row = r * rb + lax.broadcasted_iota(jnp.int32, (rb, 1), 0)
    k = jnp.where(row == L, 0.0, k)
    npair = out_ref.shape[0]
    w = jnp.concatenate([_pack_pair(k[:, 2 * j * LANES:(2 * j + 1) * LANES], k[:, (2 * j + 1) * LANES:(2 * j + 2) * LANES])
                         for j in range(npair)], axis=1)
    _store_grouped(out_ref, w, 0, npair)


def _filters(ztab, fw, L):
    depth = fw["w1"].shape[0]
    rows = 2 * L
    rb = min(512, L)
    nblk = rows // rb
    half = nblk // 2
    oc = HY_ORDER * HY_WIDTH
    lay = lambda *shape: pl.BlockSpec((None,) + shape, lambda l, r: (l,) + (0,) * len(shape))
    in_specs = [
        pl.BlockSpec((rb, 1), lambda l, r: (r, 0)), pl.BlockSpec((HY_EMB_PAD, rb), lambda l, r: (0, r)),
        lay(HY_FFN, HY_EMB_PAD), lay(HY_FFN, 1), lay(HY_FFN, 2), lay(HY_FFN, HY_FFN), lay(HY_FFN, 1),
        pl.BlockSpec((None, None, 2, HY_FFN, oc), lambda l, r: (l, r // half, 0, 0, 0)),
        pl.BlockSpec((None, None, 1, oc), lambda l, r: (l, r // half, 0, 0)),
    ]
    return pl.pallas_call(
        functools.partial(_filter_kernel, rb=rb, L=L),
        grid=(depth, nblk), in_specs=in_specs,
        out_specs=pl.BlockSpec((None, oc // LANES // 2, DFT_G, rb // DFT_G, LANES), lambda l, r: (l, 0, 0, r, 0)),
        out_shape=jax.ShapeDtypeStruct((depth, oc // LANES // 2, DFT_G, rows // DFT_G, LANES), jnp.uint32),
        compiler_params=_cparams(("parallel", "parallel")), name="hyena_filter",
    )(ztab[:, 0:1], ztab.T, fw["w1"], fw["b1"], fw["sf"], fw["w2"], fw["b2"], fw["w3"], fw["dec"])


def _rows_of(ref, p, n, lead=()):
    return ref[lead + (pl.ds(p, n, stride=SUBLANES), slice(None))]


def _pack_pair(re, im):
    hi = lax.bitcast_convert_type(re.astype(BF16).astype(F32), jnp.uint32)
    lo = lax.bitcast_convert_type(im.astype(BF16).astype(F32), jnp.uint32)
    return hi | (lo >> 16)


def _unpack_pair(w, dtype=BF16):
    re = lax.bitcast_convert_type(w & jnp.uint32(0xFFFF0000), F32)
    im = lax.bitcast_convert_type(w << 16, F32)
    return re.astype(dtype), im.astype(dtype)


def _dft1_block(f_ref, xs, out_ref, gg):
    kp = f_ref.shape[0] // 2
    res = _dot(f_ref[...], jnp.concatenate([x.astype(BF16) for x in xs], axis=1))
    w = _pack_pair(res[:kp], res[kp:])
    for kg in range(kp // SUBLANES):
        for p in range(SUBLANES):
            r0 = (gg * SUBLANES + p) * SUBLANES
            out_ref[kg, r0:r0 + SUBLANES, :] = w[kg * SUBLANES:(kg + 1) * SUBLANES, p * LANES:(p + 1) * LANES]


def _dft1_kernel(f_ref, x_ref, out_ref):
    R = f_ref.shape[1]
    for gg in range(OUTER_GB):
        _dft1_block(f_ref, [_rows_of(x_ref, p, R, (gg,)) for p in range(SUBLANES)], out_ref, gg)


def _dft1_spec(kg):
    return pl.BlockSpec((None, None, kg, OUTER_GB * SUBLANES * SUBLANES, LANES), lambda b, c, g: (b, c, 0, g, 0))


def _dft1(x, f1):
    nb, ncb, _, rows, _ = x.shape
    kp = f1.shape[0] // 2
    kg = kp // SUBLANES
    return pl.pallas_call(
        _dft1_kernel, grid=(nb, ncb, DFT_G // OUTER_GB),
        in_specs=[pl.BlockSpec(f1.shape, lambda b, c, g: (0, 0)),
                  pl.BlockSpec((None, None, OUTER_GB, rows, LANES), lambda b, c, g: (b, c, g, 0, 0))],
        out_specs=_dft1_spec(kg),
        out_shape=jax.ShapeDtypeStruct((nb, ncb, kg, DFT_N2 * SUBLANES, LANES), jnp.uint32),
        compiler_params=_cparams(("parallel", "parallel", "parallel")), name="dft_outer",
    )(f1, x)


def _dft1_pairs_kernel(f_ref, x_ref, out_ref):
    R = f_ref.shape[1]
    for gg in range(OUTER_GB):
        pairs = [_unpack_pair(_rows_of(x_ref, p, R, (gg,))) for p in range(SUBLANES)]
        for c in range(2):
            _dft1_block(f_ref, [pr[c] for pr in pairs], out_ref.at[c], gg)


def _dft1_pairs(x, f1):
    nb, npair, _, rows, _ = x.shape
    kg = f1.shape[0] // 2 // SUBLANES
    return pl.pallas_call(
        _dft1_pairs_kernel, grid=(nb, npair, DFT_G // OUTER_GB),
        in_specs=[pl.BlockSpec(f1.shape, lambda b, c, g: (0, 0)),
                  pl.BlockSpec((None, None, OUTER_GB, rows, LANES), lambda b, c, g: (b, c, g, 0, 0))],
        out_specs=pl.BlockSpec((None, 2, kg, OUTER_GB * SUBLANES * SUBLANES, LANES), lambda b, c, g: (b, c, 0, g, 0)),
        out_shape=jax.ShapeDtypeStruct((nb, 2 * npair, kg, DFT_N2 * SUBLANES, LANES), jnp.uint32),
        compiler_params=_cparams(("parallel", "parallel", "parallel")), name="dft_outer_filter",
    )(f1, x)


def _inner_fwd(m1_ref, a_ref, kk):
    w = jnp.concatenate([_rows_of(a_ref, kk, DFT_N2, (c,)) for c in range(a_ref.shape[0])], axis=1)
    ar, ai = _unpack_pair(w)
    return _dot(m1_ref[kk], jnp.concatenate([ar, ai], axis=0))


def _spec_kernel(m1_ref, a_ref, out_ref):
    for kk in range(SUBLANES):
        x = _inner_fwd(m1_ref, a_ref, kk)
        for c in range(a_ref.shape[0]):
            out_ref[c, kk] = _pack_pair(x[:DFT_N2, c * LANES:(c + 1) * LANES], x[DFT_N2:, c * LANES:(c + 1) * LANES])


def _filter_spectrum(a, m1):
    depth, ncb, kg, rows, _ = a.shape
    n2 = DFT_N2
    return pl.pallas_call(
        _spec_kernel, grid=(kg, depth, ncb // MID_CB),
        in_specs=[pl.BlockSpec((SUBLANES, 2 * n2, 2 * n2), lambda k, l, c: (k, 0, 0)),
                  pl.BlockSpec((None, MID_CB, None, rows, LANES), lambda k, l, c: (l, c, k, 0, 0))],
        out_specs=pl.BlockSpec((None, MID_CB, SUBLANES, n2, LANES), lambda k, l, c: (l, c, k, 0, 0)),
        out_shape=jax.ShapeDtypeStruct((depth, ncb, kg * SUBLANES, n2, LANES), jnp.uint32),
        compiler_params=_cparams(("parallel", "parallel", "parallel")), name="filter_spectrum",
    )(m1, a)


def _mid_kernel(m1_ref, m2_ref, kf_ref, a_ref, out_ref):
    ncb = a_ref.shape[0]
    lanes = lambda parts: jnp.concatenate(parts, axis=1)
    for kk in range(SUBLANES):
        x = _inner_fwd(m1_ref, a_ref, kk)
        xr, xi = x[:DFT_N2], x[DFT_N2:]
        kr, ki = _unpack_pair(lanes([kf_ref[c, kk] for c in range(ncb)]), F32)
        yr = (xr * kr - xi * ki).astype(BF16)
        yi = (xr * ki + xi * kr).astype(BF16)
        y = _dot(m2_ref[kk], jnp.concatenate([yr, yi], axis=0))
        w = _pack_pair(y[:DFT_N2], y[DFT_N2:])
        for c in range(ncb):
            for g in range(DFT_G):
                out_ref[c, g, kk * SUBLANES:(kk + 1) * SUBLANES, :] = (
                    w[g * SUBLANES:(g + 1) * SUBLANES, c * LANES:(c + 1) * LANES])


def _conv_mid(a, kf, m1, m2, layer, order):
    nb, ncb, kg, rows, _ = a.shape
    n2 = DFT_N2
    mspec = pl.BlockSpec((SUBLANES, 2 * n2, 2 * n2), lambda k, c, b: (k, 0, 0))
    return pl.pallas_call(
        _mid_kernel, grid=(kg, ncb // MID_CB, nb),
        in_specs=[mspec, mspec,
                  pl.BlockSpec((None, MID_CB, SUBLANES, n2, LANES),
                               lambda k, c, b: (layer, order * (ncb // MID_CB) + c, k, 0, 0)),
                  pl.BlockSpec((None, MID_CB, None, rows, LANES), lambda k, c, b: (b, c, k, 0, 0))],
        out_specs=pl.BlockSpec((None, MID_CB, DFT_G, SUBLANES * SUBLANES, LANES), lambda k, c, b: (b, c, 0, k, 0)),
        out_shape=jax.ShapeDtypeStruct((nb, ncb, DFT_G, kg * SUBLANES * SUBLANES, LANES), jnp.uint32),
        compiler_params=_cparams(("parallel", "parallel", "arbitrary")), name="conv_mid",
    )(m1, m2, kf, a)


def _gate_kernel(gre_ref, gim_ref, f_ref, b_ref, z_ref, gate_ref, bias_ref, out_ref, *next_ref):
    R, kp = gre_ref.shape
    bias = bias_ref[...]
    for gg in range(OUTER_GB):
        bre, bim = _unpack_pair(jnp.concatenate([_rows_of(b_ref, p, kp, (gg,)) for p in range(SUBLANES)], axis=1))
        y = _dot(gre_ref[...], bre) + _dot(gim_ref[...], bim)
        vals = []
        for p in range(SUBLANES):
            yp = y[:, p * LANES:(p + 1) * LANES]
            vals.append(_rows_of(gate_ref, p, R, (gg,)) * (yp + _rows_of(z_ref, p, R, (gg,)) * bias))
            out_ref[gg, pl.ds(p, R, stride=SUBLANES), :] = vals[p]
        if next_ref:
            _dft1_block(f_ref, vals, next_ref[0], gg)


def _conv_out(bsp, tabs, z, gate, bias, with_next):
    nb, ncb, _, krows, _ = bsp.shape
    rows = z.shape[3]
    gre, gim, f1 = tabs["gre"], tabs["gim"], tabs["f1d"]
    kg = f1.shape[0] // 2 // SUBLANES
    full = lambda t: pl.BlockSpec(t.shape, lambda b, c, g: (0, 0))
    tile = pl.BlockSpec((None, None, OUTER_GB, rows, LANES), lambda b, c, g: (b, c, g, 0, 0))
    out_specs, out_shape = [tile], [jax.ShapeDtypeStruct(z.shape, F32)]
    if with_next:
        out_specs.append(_dft1_spec(kg))
        out_shape.append(jax.ShapeDtypeStruct((nb, ncb, kg, DFT_N2 * SUBLANES, LANES), jnp.uint32))
    return pl.pallas_call(
        _gate_kernel, grid=(nb, ncb, DFT_G // OUTER_GB),
        in_specs=[full(gre), full(gim), full(f1),
                  pl.BlockSpec((None, None, OUTER_GB, krows, LANES), lambda b, c, g: (b, c, g, 0, 0)),
                  tile, tile, pl.BlockSpec((None, 1, LANES), lambda b, c, g: (c, 0, 0))],
        out_specs=out_specs, out_shape=out_shape,
        compiler_params=_cparams(("parallel", "parallel", "parallel")), name="conv_gate",
    )(gre, gim, f1, bsp, z, gate, bias)


def _hyena_convs(v, x1, x2, kf, bias, tabs, layer):
    b0, b1 = bias[0].reshape(HY_CB, 1, LANES), bias[1].reshape(HY_CB, 1, LANES)
    bsp = _conv_mid(_dft1(v, tabs["f1d"]), kf, tabs["m1"], tabs["m2"], layer, 0)
    z1, a2 = _conv_out(bsp, tabs, v, x1, b0, True)
    bsp = _conv_mid(a2, kf, tabs["m1"], tabs["m2"], layer, 1)
    return _conv_out(bsp, tabs, z1, x2, b1, False)[0]


def _attn_kernel(flag_ref, qt_ref, qtn_ref, k_ref, vt_ref, g_ref, o_ref, acc_scr, m_scr, s_scr, q_scr,
                 *, nk, tk, unroll):
    pair = pl.program_id(0) * pl.num_programs(1) + pl.program_id(1)
    fast = jnp.logical_and(flag_ref[2 * pair] == 1, flag_ref[2 * pair + 1] == 1)
    acc_scr[...] = jnp.zeros(acc_scr.shape, F32)

    @pl.when(fast)
    def _():
        q_scr[0] = qt_ref[0]
        q_scr[1] = qt_ref[1]
        q_scr[2] = qtn_ref[...]

        def scores(cc):
            hq = cc // nk
            c = cc - hq * nk
            hk = jnp.where(hq == 2, 0, hq)
            return _dot(k_ref[hk, pl.ds(pl.multiple_of(c * tk, tk), tk), :], q_scr[hq])

        @pl.when(pl.program_id(2) == 0)
        def _():
            s_scr[0] = scores(jnp.int32(0))

        def body(j, carry):
            hd = (j * unroll) // nk
            c0 = j * unroll - hd * nk
            pv = None
            for u in range(unroll):
                s_scr[(u + 1) % 2] = scores(j * unroll + u + 1)
                pt = jnp.exp2(s_scr[u % 2]).astype(BF16)
                d = _dot(vt_ref[hd, c0 + u], pt)
                pv = d if pv is None else pv + d
            acc_scr[hd] += pv
            return carry
        lax.fori_loop(0, 2 * nk // unroll, body, 0)

    @pl.when(jnp.logical_not(fast))
    def _():
        for hh in range(2):
            qt = qt_ref[hh]
            m_scr[...] = jnp.full(m_scr.shape, -jnp.inf, F32)

            def body(j, carry, hh=hh, qt=qt):
                kk = k_ref[hh, pl.ds(pl.multiple_of(j * tk, tk), tk), :]
                s = _dot(kk, qt)
                m_prev = m_scr[...]
                m_new = jnp.maximum(m_prev, jnp.max(s, axis=0, keepdims=True))
                pt = jnp.exp2(s - m_new).astype(BF16)
                acc_scr[hh] = jnp.exp2(m_prev - m_new) * acc_scr[hh] + _dot(vt_ref[hh, j], pt)
                m_scr[...] = m_new
                return carry
            lax.fori_loop(0, nk, body, 0)

    outs = []
    for hh in range(2):
        acc = acc_scr[hh]
        o = acc[:V_HEAD] / acc[V_HEAD:V_HEAD + 1]
        ms = jnp.mean(o * o, axis=0, keepdims=True)
        outs.append(o * lax.rsqrt(ms + EPS))
    ot = jnp.concatenate(outs, axis=0)
    o_ref[...] = (ot.T * g_ref[...]).astype(o_ref.dtype)


def _attention(flags, qt, k, vt, g_attn, nb, L):
    H = MLA_HEADS
    tq = min(512, L)
    nq = L // tq
    nk, tk = vt.shape[2], vt.shape[4]
    grid_spec = pltpu.PrefetchScalarGridSpec(
        num_scalar_prefetch=1, grid=(nb, H // 2, nq),
        in_specs=[pl.BlockSpec((None, 2, HEAD_PAD, tq), lambda b, h, i, f: (b, h, 0, i)),
                  pl.BlockSpec((None, None, HEAD_PAD, tq), lambda b, h, i, f: (b, 2 * h, 0, jnp.minimum(i + 1, nq - 1))),
                  pl.BlockSpec((None, 2, L, HEAD_PAD), lambda b, h, i, f: (b, h, 0, 0)),
                  pl.BlockSpec((None, 2, nk, V_PAD, tk), lambda b, h, i, f: (b, h, 0, 0, 0)),
                  pl.BlockSpec((1, 2 * V_HEAD), lambda b, h, i, f: (0, h))],
        out_specs=pl.BlockSpec((None, tq, 2 * V_HEAD), lambda b, h, i, f: (b, i, h)),
        scratch_shapes=[pltpu.VMEM((2, V_PAD, tq), F32), pltpu.VMEM((1, tq), F32), pltpu.VMEM((2, tk, tq), F32),
                        pltpu.VMEM((3, HEAD_PAD, tq), BF16)])
    unroll = math.gcd(nk, ATTN_UNROLL)
    assert unroll % 2 == 0
    return pl.pallas_call(
        functools.partial(_attn_kernel, nk=nk, tk=tk, unroll=unroll), grid_spec=grid_spec,
        out_shape=jax.ShapeDtypeStruct((nb, L, ATTN_WIDTH), BF16),
        compiler_params=_cparams(("parallel", "parallel", "arbitrary")), name="attention",
    )(flags, qt, qt, k, vt, g_attn)


def _fast_flags(stats):
    H = MLA_HEADS
    qn = jnp.max(stats[:, :, 0:H, :], axis=(1, 3))
    kn = jnp.max(stats[:, :, H:2 * H, :], axis=(1, 3))
    vm = jnp.max(stats[:, :, 2 * H, :], axis=(1, 2))
    ok = jnp.logical_and(qn * kn <= FAST_S_MAX * FAST_S_MAX, (vm <= FAST_V_MAX)[:, None])
    return ok.astype(jnp.int32).reshape(-1)


def _mix_mlp_kernel(x_ref, zh_ref, an_ref, ghy_ref, gsum_ref, wo_ref, gpost_ref, gmpre_ref, wup_ref,
                    wdn_ref, gmpost_ref, out_ref):
    zh = jnp.concatenate(
        [jnp.concatenate([zh_ref[j, g, nl * SUBLANES:(nl + 1) * SUBLANES, :] for j in range(HY_CB)], axis=1)
         for nl in range(zh_ref.shape[2] // SUBLANES) for g in range(DFT_G)], axis=0)
    ms = _dot((zh * zh).astype(BF16), gsum_ref[...]) * (HY_GROUPS / HY_WIDTH)
    hn = (zh * lax.rsqrt(ms + EPS) * ghy_ref[...]).astype(BF16)
    mix = _dot(hn, wo_ref[0:HY_WIDTH, :]) + _dot(an_ref[...], wo_ref[HY_WIDTH:, :])
    x = x_ref[...] + _rms(mix, gpost_ref[...])
    h = _rms(x, gmpre_ref[...]).astype(BF16)
    up = jnp.maximum(_dot(h, wup_ref[...]), 0.0)
    m = _dot((up * up).astype(BF16), wdn_ref[...])
    out_ref[...] = x + _rms(m, gmpost_ref[...])


def _mix_mlp(x, zh, an, lw, tabs, L):
    M, D = x.shape
    tm = min(512, L)
    tps = L // tm
    dff = lw["wup"].shape[1]
    mw = HY_WIDTH + ATTN_WIDTH
    const = lambda *shape: pl.BlockSpec(shape, lambda i: (0,) * len(shape), pipeline_mode=pl.Buffered(1))
    rows = lambda w: pl.BlockSpec((tm, w), lambda i: (i, 0))
    return pl.pallas_call(
        _mix_mlp_kernel, grid=(M // tm,),
        in_specs=[rows(D), pl.BlockSpec((None, HY_CB, DFT_G, tm // DFT_G, LANES),
                                        lambda i: (i // tps, 0, 0, i % tps, 0)),
                  rows(ATTN_WIDTH), const(1, HY_WIDTH), const(HY_WIDTH, HY_WIDTH),
                  const(mw, D), const(1, D), const(1, D), const(D, dff), const(dff, D), const(1, D)],
        out_specs=rows(D), out_shape=jax.ShapeDtypeStruct((M, D), F32),
        compiler_params=_cparams(("parallel",)), name="mix_mlp",
    )(x, zh, an, lw["g_hy"], tabs["gsum"], lw["wo"], lw["g_post"], lw["g_mpre"], lw["wup"], lw["wdn"],
      lw["g_mpost"])


def _tables(L):
    n = 2 * L
    n2 = DFT_N2
    n1 = n // n2
    nh = n1 // 2
    kp = -(-(nh + 1) // SUBLANES) * SUBLANES
    two_pi = 2.0 * math.pi

    k1 = jnp.arange(kp, dtype=jnp.int32)
    valid = (k1 <= nh)
    def outer(ncols):
        nn = jnp.arange(ncols, dtype=jnp.int32)
        ang = ((k1[:, None] * nn[None, :]) % n1).astype(F32) * (two_pi / n1)
        c = jnp.where(valid[:, None], jnp.cos(ang), 0.0)
        s = jnp.where(valid[:, None], -jnp.sin(ang), 0.0)
        return jnp.concatenate([c, s], axis=0).astype(BF16)
    f1d = outer(nh)
    f1f = outer(n1)
    nn = jnp.arange(nh, dtype=jnp.int32)
    ang = ((nn[:, None] * k1[None, :]) % n1).astype(F32) * (two_pi / n1)
    wgt = jnp.where(valid, jnp.where((k1 == 0) | (k1 == nh), 1.0, 2.0), 0.0) / n
    gre = (jnp.cos(ang) * wgt[None, :]).astype(BF16)
    gim = (-jnp.sin(ang) * wgt[None, :]).astype(BF16)
    a2 = jnp.arange(n2, dtype=jnp.int32)
    idx = (a2[None, :, None] * a2[None, None, :] * n1 + a2[None, None, :] * k1[:, None, None]) % n
    ph = idx.astype(F32) * (two_pi / n)
    gr, gi = jnp.cos(ph), -jnp.sin(ph)
    m1 = jnp.concatenate([jnp.concatenate([gr, -gi], axis=2), jnp.concatenate([gi, gr], axis=2)], axis=1)
    m2 = jnp.swapaxes(m1, 1, 2)
    inv = 1.0 / (ROPE_BASE ** (jnp.arange(0, QK_ROPE, 2, dtype=F32) / QK_ROPE))
    ang = jnp.arange(L, dtype=F32)[:, None] * inv[None, :]
    cos, sin = jnp.cos(ang), jnp.sin(ang)
    rcos = jnp.concatenate([cos, cos], axis=1)
    rsin = jnp.concatenate([sin, sin], axis=1)
    kcs =jnp.concatenate([cos, cos, sin, sin, jnp.zeros((L, HEAD_PAD - 2 * QK_ROPE), F32)], axis=1)
    cc = jnp.arange(MLA_HEADS * HEAD_PAD)
    src = jnp.arange(HEAD_PAD)
    pk = ((cc[None, :] % HEAD_PAD) - QK_NOPE == src[:, None]) & (src[:, None] < QK_ROPE)
    grp = jnp.arange(HY_WIDTH) // (HY_WIDTH // HY_GROUPS)
    gsum = (grp[:, None] == grp[None, :]).astype(BF16)
    t = jnp.linspace(0.0, 1.0, L, dtype=F32)[:, None]
    omega = (two_pi / L) * jnp.arange(L, dtype=F32)
    bands = jnp.linspace(1e-4, HY_BANDS - 1, HY_BANDS, dtype=F32)
    phase = omega[:, None] * bands[None, :]
    z = jnp.concatenate([t, jnp.cos(phase), -jnp.sin(phase), jnp.zeros((L, HY_EMB_PAD - HY_EMB), F32)], axis=-1)
    ztab = jnp.concatenate([z, z[:1], z[:0:-1]], axis=0)
    return dict(f1d=f1d, f1f=f1f, gre=gre, gim=gim, m1=m1.astype(BF16), m2=m2.astype(BF16), rcos_t=rcos.T,
                rsin_t=rsin.T, kcs=kcs, pk=pk.astype(BF16), gsum=gsum, ztab=ztab)


def _rot_half_cols(w):
    half = QK_ROPE // 2
    return jnp.concatenate([-w[..., half:], w[..., :half]], axis=-1)


def _layer_weights(i, p, tabs):
    D = p["w_in"].shape[1]
    H = MLA_HEADS
    hw3 = 3 * HY_WIDTH
    w_in = p["w_in"][i]
    kpe = w_in[:, hw3 + Q_RANK + KV_RANK:]
    win = jnp.concatenate([w_in, _rot_half_cols(kpe),
                           jnp.zeros((D, HEAD_PAD - 2 * QK_ROPE), F32)], axis=1).astype(BF16)
    dq = QK_NOPE + QK_ROPE
    wq = p["mla_w_uq"][i].reshape(Q_RANK, H, dq)
    wqt = jnp.concatenate([wq.reshape(Q_RANK, H * dq),
                           _rot_half_cols(wq[..., QK_NOPE:]).reshape(Q_RANK, H * QK_ROPE)], axis=1).T
    wkv =p["mla_w_ukv"][i].reshape(KV_RANK, H, QK_NOPE + V_HEAD)
    wk = jnp.concatenate([wkv[..., :QK_NOPE], jnp.zeros((KV_RANK, H, HEAD_PAD - QK_NOPE), F32)], axis=2)
    wv = jnp.concatenate([wkv[..., QK_NOPE:], jnp.zeros((KV_RANK, H, V_PAD - V_HEAD), F32)], axis=2)
    row = lambda a: a.reshape(1, -1)
    return dict(
        win=win, g_pre=row(p["norm_mix_pre"][i]), conv_w=p["hy_conv_w"][i], conv_b=row(p["hy_conv_b"][i]),
        q_g=row(p["mla_q_norm"][i]), wqt=wqt.astype(BF16), kv_g=row(p["mla_kv_norm"][i]),
        wkp=jnp.concatenate([wk.reshape(KV_RANK, H * HEAD_PAD).astype(BF16), tabs["pk"]], axis=0),
        wvt=wv.reshape(KV_RANK, H * V_PAD).T.astype(BF16),
        g_hy=row(p["grp_norm_hy"][i]), g_attn=row(p["grp_norm_attn"][i]), wo=p["w_out"][i].astype(BF16),
        g_post=row(p["norm_mix_post"][i]), g_mpre=row(p["norm_mlp_pre"][i]), g_mpost=row(p["norm_mlp_post"][i]),
        wup=p["w_mlp_up"][i].astype(BF16), wdn=p["w_mlp_down"][i].astype(BF16),
    )


def _filter_weights(p):
    depth = p["hy_ffn_w1"].shape[0]
    oc = HY_ORDER * HY_WIDTH
    w1 = jnp.pad(p["hy_ffn_w1"], ((0, 0), (0, HY_EMB_PAD - HY_EMB), (0, 0)))
    w3 = p["hy_ffn_w3"].reshape(depth, HY_FFN, HY_ORDER, 2, HY_WIDTH).transpose(0, 3, 1, 2, 4)
    dec = p["hy_decay"].transpose(0, 2, 1, 3).reshape(depth, 2, 1, oc)
    tr = lambda a: jnp.swapaxes(a, 1, 2)
    w3 = w3.reshape(depth, 2, HY_FFN, oc)
    w3_hi = w3.astype(BF16)
    w3_lo = (w3 - w3_hi.astype(F32)).astype(BF16)
    return dict(w1=tr(w1), b1=p["hy_ffn_b1"][:, :, None], sf=tr(p["hy_sin_freq"]), w2=tr(p["hy_ffn_w2"]),
                b2=p["hy_ffn_b2"][:, :, None], w3=jnp.stack([w3_hi, w3_lo], axis=2), dec=dec)


def kernel(x_prompt, x_sample, w_in, hy_conv_w, hy_conv_b, hy_ffn_w1, hy_ffn_b1, hy_ffn_w2, hy_ffn_b2,
           hy_ffn_w3, hy_sin_freq, hy_decay, hy_bias, mla_q_norm, mla_w_uq, mla_kv_norm, mla_w_ukv,
           grp_norm_hy, grp_norm_attn, w_out, norm_mix_pre, norm_mix_post, norm_mlp_pre, norm_mlp_post,
           w_mlp_up, w_mlp_down):
    p = dict(w_in=w_in, hy_conv_w=hy_conv_w, hy_conv_b=hy_conv_b, hy_ffn_w1=hy_ffn_w1, hy_ffn_b1=hy_ffn_b1,
             hy_ffn_w2=hy_ffn_w2, hy_ffn_b2=hy_ffn_b2, hy_ffn_w3=hy_ffn_w3, hy_sin_freq=hy_sin_freq,
             hy_decay=hy_decay, hy_bias=hy_bias, mla_q_norm=mla_q_norm, mla_w_uq=mla_w_uq,
             mla_kv_norm=mla_kv_norm, mla_w_ukv=mla_w_ukv, grp_norm_hy=grp_norm_hy,
             grp_norm_attn=grp_norm_attn, w_out=w_out, norm_mix_pre=norm_mix_pre, norm_mix_post=norm_mix_post,
             norm_mlp_pre=norm_mlp_pre, norm_mlp_post=norm_mlp_post, w_mlp_up=w_mlp_up, w_mlp_down=w_mlp_down)
    bp, L, D = x_prompt.shape
    bs, Ls, _ = x_sample.shape
    assert L == Ls and L % (DFT_N2 * SUBLANES) == 0
    nb = bp + bs
    depth = w_in.shape[0]

    tabs = _tables(L)
    kc = _filters(tabs["ztab"], _filter_weights(p), L)
    kf = _filter_spectrum(_dft1_pairs(kc, tabs["f1f"]), tabs["m1"])

    x = jnp.concatenate([x_prompt.reshape(bp * L, D), x_sample.reshape(bs * L, D)], axis=0)
    for i in range(depth):
        lw = _layer_weights(i, p, tabs)
        v, x1, x2, qt, k, vt, stats = _inproj(x, lw, tabs, nb, L)
        z = _hyena_convs(v, x1, x2, kf, hy_bias[i], tabs, i)
        an = _attention(_fast_flags(stats), qt, k, vt, lw["g_attn"], nb, L).reshape(nb * L, ATTN_WIDTH)
        x = _mix_mlp(x, z, an, lw, tabs, L)
    return (x[:bp * L].reshape(bp, L, D), x[bp * L:].reshape(bs, L, D))
```

```python
import functools
import math

import jax
import jax.numpy as jnp
from jax import lax
from jax.experimental import pallas as pl
from jax.experimental.pallas import tpu as pltpu

F32 = jnp.float32
BF16 = jnp.bfloat16

EPS = 1e-6
HY_WIDTH = 512
HY_GROUPS = 8
HY_ORDER = 2
HY_BANDS = 16
HY_EMB = 2 * HY_BANDS + 1
HY_EMB_PAD = 40
HY_FFN = 64
MLA_HEADS = 8
QK_NOPE = 64
QK_ROPE = 32
V_HEAD = 64
Q_RANK = 256
KV_RANK = 128
ROPE_BASE = 10000.0
HEAD_PAD = 128
V_PAD = 128
ATTN_WIDTH = MLA_HEADS * V_HEAD

LANES = 128
SUBLANES = 8
HY_CB = HY_WIDTH // LANES
DFT_N2 = 128
DFT_G = DFT_N2 // SUBLANES
MID_CB = 2
OUTER_GB = 4
HALO = 16
VMEM_LIMIT = 56 * 1024 * 1024
ATTN_UNROLL = 32
STAT_ROWS = 24

FAST_S_MAX = 64.0
FAST_V_MAX = 2.0 ** 30


def _cparams(sem):
    return pltpu.CompilerParams(dimension_semantics=sem, vmem_limit_bytes=VMEM_LIMIT)


def _dot(a, b):
    return jnp.dot(a, b, preferred_element_type=F32)


def _dot_nt(a, b):
    return lax.dot_general(a, b, (((1,), (1,)), ((), ())), preferred_element_type=F32)


def _rms(x, g):
    return x * lax.rsqrt(jnp.mean(x * x, axis=-1, keepdims=True) + EPS) * g


def _store_grouped(out_ref, val, lane0, ncb):
    for j in range(ncb):
        for nl in range(val.shape[0] // DFT_N2):
            for g in range(DFT_G):
                r0 = nl * DFT_N2 + g * SUBLANES
                out_ref[j, g, nl * SUBLANES:(nl + 1) * SUBLANES, :] = (
                    val[r0:r0 + SUBLANES, lane0 + j * LANES:lane0 + (j + 1) * LANES])


def _fold_lanes(row):
    parts = [row[:, j * LANES:(j + 1) * LANES] for j in range(row.shape[1] // LANES)]
    return functools.reduce(jnp.maximum, parts)


def _inproj_kernel(xp_ref, x_ref, xn_ref, gpre_ref, win_ref, cw_ref, cb_ref, qg_ref, wqt_ref,
                   kvg_ref, wkp_ref, wvt_ref, qcos_ref, qsin_ref, kcs_ref,
                   vx_out, x2_out, qt_out, k_out, vt_out, st_out, h_scr, pe_scr,
                   *, tm, tiles_per_seq, scale):
    i = pl.program_id(0)
    t_idx = i % tiles_per_seq
    g = gpre_ref[...]
    h_scr[0:HALO, :] = _rms(xp_ref[...], g).astype(BF16)
    h_scr[HALO:HALO + tm, :] = _rms(x_ref[...], g).astype(BF16)
    h_scr[HALO + tm:2 * HALO + tm, :] = _rms(xn_ref[...], g).astype(BF16)
    hw3 = 3 * HY_WIDTH
    core = _dot(h_scr[HALO:HALO + tm, :], win_ref[:, hw3:])
    pe_scr[...] = _dot(h_scr[...], win_ref[:, 0:hw3])
    row = lax.broadcasted_iota(jnp.int32, (tm, 1), 0)
    prev = pe_scr[HALO - 1:HALO - 1 + tm, :]
    cur = pe_scr[HALO:HALO + tm, :]
    nxt = pe_scr[HALO + 1:HALO + 1 + tm, :]
    prev = jnp.where(jnp.logical_and(row == 0, t_idx == 0), 0.0, prev)
    nxt = jnp.where(jnp.logical_and(row == tm - 1, t_idx == tiles_per_seq - 1), 0.0, nxt)
    u = prev * cw_ref[0:1, :] + cur * cw_ref[1:2, :] + nxt * cw_ref[2:3, :] + cb_ref[...]
    _store_grouped(vx_out, _pack_pair(u[:, :HY_WIDTH], u[:, HY_WIDTH:2 * HY_WIDTH]), 0, HY_CB)
    _store_grouped(x2_out, u, 2 * HY_WIDTH, HY_CB)

    cq = core[:, 0:Q_RANK]
    cqn =_rms(cq, qg_ref[...]).astype(BF16)
    dq = QK_NOPE + QK_ROPE
    qall = _dot_nt(wqt_ref[...], cqn)
    rcos = qcos_ref[...]
    rsin = qsin_ref[...]
    zpad = jnp.zeros((HEAD_PAD - dq, tm), F32)
    for h in range(MLA_HEADS):
        rot = qall[MLA_HEADS * dq + h * QK_ROPE:MLA_HEADS * dq + (h + 1) * QK_ROPE]
        rope = qall[h * dq + QK_NOPE:(h + 1) * dq] * rcos + rot * rsin
        qh = (jnp.concatenate([qall[h * dq:h * dq + QK_NOPE], rope, zpad], axis=0) * scale).astype(BF16)
        qt_out[h] = qh
        qf = qh.astype(F32)
        st_out[h:h + 1, :] = _fold_lanes(jnp.sum(qf * qf, axis=0, keepdims=True))

    ckv = core[:, Q_RANK:Q_RANK + KV_RANK]
    ckvn = _rms(ckv, kvg_ref[...]).astype(BF16)
    kp = core[:, Q_RANK + KV_RANK:] * kcs_ref[...]
    kp = kp + pltpu.roll(kp, HEAD_PAD - QK_ROPE, axis=1)
    kk = _dot(jnp.concatenate([ckvn, kp.astype(BF16)], axis=1), wkp_ref[...])
    for h in range(MLA_HEADS):
        kh = kk[:, h * HEAD_PAD:(h + 1) * HEAD_PAD].astype(BF16)
        k_out[h] = kh
        kf = kh.astype(F32)
        kn = jnp.max(jnp.sum(kf * kf, axis=1, keepdims=True), axis=0, keepdims=True)
        st_out[MLA_HEADS + h:MLA_HEADS + h + 1, :] = jnp.broadcast_to(kn, (1, LANES))
    vt = _dot_nt(wvt_ref[...], ckvn)
    st_out[2 * MLA_HEADS:2 * MLA_HEADS + 1, :] = _fold_lanes(jnp.max(jnp.abs(vt), axis=0, keepdims=True))
    st_out[2 * MLA_HEADS + 1:, :] = jnp.zeros((STAT_ROWS - 2 * MLA_HEADS - 1, LANES), F32)
    frow = lax.broadcasted_iota(jnp.int32, (MLA_HEADS * V_PAD, 1), 0)
    vt = vt + jnp.where(frow % V_PAD == V_HEAD, 1.0, 0.0)
    for h in range(MLA_HEADS):
        vt_out[h] = vt[h * V_PAD:(h + 1) * V_PAD, :].astype(BF16)


def _inproj(x, lw, tabs, nb, L):
    M, D = x.shape
    tm = min(512, L)
    tps = L // tm
    nt = M // tm
    hb = tm // HALO
    nhb = M // HALO
    H = MLA_HEADS
    wcols = lw["win"].shape[1]
    const = lambda *shape: pl.BlockSpec(shape, lambda i: (0,) * len(shape))
    tab_t = pl.BlockSpec((QK_ROPE, tm), lambda i: (0, i % tps))
    in_specs = [
        pl.BlockSpec((HALO, D), lambda i: (jnp.maximum(i * hb - 1, 0), 0)),
        pl.BlockSpec((tm, D), lambda i: (i, 0)),
        pl.BlockSpec((HALO, D), lambda i: (jnp.minimum((i + 1) * hb, nhb - 1), 0)),
        const(1, D), const(D, wcols), const(3, 3 * HY_WIDTH), const(1, 3 * HY_WIDTH),
        const(1, Q_RANK), const(H * HEAD_PAD, Q_RANK),
        const(1, KV_RANK), const(KV_RANK + HEAD_PAD, H * HEAD_PAD),
        const(H * V_PAD, KV_RANK),
        tab_t, tab_t, pl.BlockSpec((tm, HEAD_PAD), lambda i: (i % tps, 0)),
    ]
    hy_spec = pl.BlockSpec((None, HY_CB, DFT_G, tm // DFT_G, LANES), lambda i: (i // tps, 0, 0, i % tps, 0))
    out_specs = [
        hy_spec, hy_spec,
        pl.BlockSpec((None, H, HEAD_PAD, tm), lambda i: (i // tps, 0, 0, i % tps)),
        pl.BlockSpec((None, H, tm, HEAD_PAD), lambda i: (i // tps, 0, i % tps, 0)),
        pl.BlockSpec((None, H, None, V_PAD, tm), lambda i: (i // tps, 0, i % tps, 0, 0)),
        pl.BlockSpec((None, None, STAT_ROWS, LANES), lambda i: (i // tps, i % tps, 0, 0)),
    ]
    hy_shape = lambda dt: jax.ShapeDtypeStruct((nb, HY_CB, DFT_G, L // DFT_G, LANES), dt)
    out_shape = [
        hy_shape(jnp.uint32), hy_shape(F32),
        jax.ShapeDtypeStruct((nb, H, HEAD_PAD, L), BF16),
        jax.ShapeDtypeStruct((nb, H, L, HEAD_PAD), BF16),
        jax.ShapeDtypeStruct((nb, H, tps, V_PAD, tm), BF16),
        jax.ShapeDtypeStruct((nb, tps, STAT_ROWS, LANES), F32),
    ]
    scale = float((QK_NOPE + QK_ROPE) ** -0.5 * math.log2(math.e))
    return pl.pallas_call(
        functools.partial(_inproj_kernel, tm=tm, tiles_per_seq=tps, scale=scale),
        grid=(nt,), in_specs=in_specs, out_specs=out_specs, out_shape=out_shape,
        scratch_shapes=[pltpu.VMEM((tm + 2 * HALO, D), BF16),
                        pltpu.VMEM((tm + 2 * HALO, 3 * HY_WIDTH), F32)],
        compiler_params=_cparams(("parallel",)), name="inproj",
    )(x, x, x, lw["g_pre"], lw["win"], lw["conv_w"], lw["conv_b"], lw["q_g"], lw["wqt"],
      lw["kv_g"], lw["wkp"], lw["wvt"], tabs["rcos_t"], tabs["rsin_t"], tabs["kcs"])


def _filter_kernel(t_ref, zt_ref, w1_ref, b1_ref, sf_ref, w2_ref, b2_ref, w3_ref, dec_ref, out_ref, *, rb, L):
    hi = lax.Precision.HIGHEST
    r = pl.program_id(1)
    h = jnp.sin(sf_ref[:, 0:1] * (jnp.dot(w1_ref[...], zt_ref[...], precision=hi, preferred_element_type=F32)
                                  + b1_ref[...]))
    h = jnp.sin(sf_ref[:, 1:2] * (jnp.dot(w2_ref[...], h, precision=hi, preferred_element_type=F32) + b2_ref[...]))
    tn = lambda a, b: lax.dot_general(a, b, (((0,), (0,)), ((), ())), preferred_element_type=F32)
    h_hi = h.astype(BF16)
    h_lo = (h - h_hi.astype(F32)).astype(BF16)
    k = tn(h_hi, w3_ref[0]) + (tn(h_hi, w3_ref[1]) + tn(h_lo, w3_ref[0]))
    k = k * jnp.exp(-t_ref[...] * jnp.abs(dec_ref[...]))
    row = r * rb + lax.broadcasted_iota(jnp.int32, (rb, 1), 0)
    k = jnp.where(row == L, 0.0, k)
    npair = out_ref.shape[0]
    w = jnp.concatenate([_pack_pair(k[:, 2 * j * LANES:(2 * j + 1) * LANES], k[:, (2 * j + 1) * LANES:(2 * j + 2) * LANES])
                         for j in range(npair)], axis=1)
    _store_grouped(out_ref, w, 0, npair)


def _filters(ztab, fw, L):
    depth = fw["w1"].shape[0]
    rows = 2 * L
    rb = min(512, L)
    nblk = rows // rb
    half = nblk // 2
    oc = HY_ORDER * HY_WIDTH
    lay = lambda *shape: pl.BlockSpec((None,) + shape, lambda l, r: (l,) + (0,) * len(shape))
    in_specs = [
        pl.BlockSpec((rb, 1), lambda l, r: (r, 0)), pl.BlockSpec((HY_EMB_PAD, rb), lambda l, r: (0, r)),
        lay(HY_FFN, HY_EMB_PAD), lay(HY_FFN, 1), lay(HY_FFN, 2), lay(HY_FFN, HY_FFN), lay(HY_FFN, 1),
        pl.BlockSpec((None, None, 2, HY_FFN, oc), lambda l, r: (l, r // half, 0, 0, 0)),
        pl.BlockSpec((None, None, 1, oc), lambda l, r: (l, r // half, 0, 0)),
    ]
    return pl.pallas_call(
        functools.partial(_filter_kernel, rb=rb, L=L),
        grid=(depth, nblk), in_specs=in_specs,
        out_specs=pl.BlockSpec((None, oc // LANES // 2, DFT_G, rb // DFT_G, LANES), lambda l, r: (l, 0, 0, r, 0)),
        out_shape=jax.ShapeDtypeStruct((depth, oc // LANES // 2, DFT_G, rows // DFT_G, LANES), jnp.uint32),
        compiler_params=_cparams(("parallel", "parallel")), name="hyena_filter",
    )(ztab[:, 0:1], ztab.T, fw["w1"], fw["b1"], fw["sf"], fw["w2"], fw["b2"], fw["w3"], fw["dec"])


def _rows_of(ref, p, n, lead=()):
    return ref[lead + (pl.ds(p, n, stride=SUBLANES), slice(None))]


def _pack_pair(re, im):
    hi = lax.bitcast_convert_type(re.astype(BF16).astype(F32), jnp.uint32)
    lo = lax.bitcast_convert_type(im.astype(BF16).astype(F32), jnp.uint32)
    return hi | (lo >> 16)


def _unpack_pair(w, dtype=BF16):
    re = lax.bitcast_convert_type(w & jnp.uint32(0xFFFF0000), F32)
    im = lax.bitcast_convert_type(w << 16, F32)
    return re.astype(dtype), im.astype(dtype)


def _dft1_block(f_ref, xs, out_ref, gg):
    kp = f_ref.shape[0] // 2
    res = _dot(f_ref[...], jnp.concatenate([x.astype(BF16) for x in xs], axis=1))
    w = _pack_pair(res[:kp], res[kp:])
    for kg in range(kp // SUBLANES):
        for p in range(SUBLANES):
            r0 = (gg * SUBLANES + p) * SUBLANES
            out_ref[kg, r0:r0 + SUBLANES, :] = w[kg * SUBLANES:(kg + 1) * SUBLANES, p * LANES:(p + 1) * LANES]


def _dft1_kernel(f_ref, x_ref, out_ref):
    R = f_ref.shape[1]
    for gg in range(OUTER_GB):
        _dft1_block(f_ref, [_unpack_pair(_rows_of(x_ref, p, R, (gg,)))[0] for p in range(SUBLANES)], out_ref, gg)


def _dft1_spec(kg):
    return pl.BlockSpec((None, None, kg, OUTER_GB * SUBLANES * SUBLANES, LANES), lambda b, c, g: (b, c, 0, g, 0))


def _dft1(x, f1):
    nb, ncb, _, rows, _ = x.shape
    kp = f1.shape[0] // 2
    kg = kp // SUBLANES
    return pl.pallas_call(
        _dft1_kernel, grid=(nb, ncb, DFT_G // OUTER_GB),
        in_specs=[pl.BlockSpec(f1.shape, lambda b, c, g: (0, 0)),
                  pl.BlockSpec((None, None, OUTER_GB, rows, LANES), lambda b, c, g: (b, c, g, 0, 0))],
        out_specs=_dft1_spec(kg),
        out_shape=jax.ShapeDtypeStruct((nb, ncb, kg, DFT_N2 * SUBLANES, LANES), jnp.uint32),
        compiler_params=_cparams(("parallel", "parallel", "parallel")), name="dft_outer",
    )(f1, x)


def _dft1_pairs_kernel(f_ref, x_ref, out_ref):
    R = f_ref.shape[1]
    for gg in range(OUTER_GB):
        pairs = [_unpack_pair(_rows_of(x_ref, p, R, (gg,))) for p in range(SUBLANES)]
        for c in range(2):
            _dft1_block(f_ref, [pr[c] for pr in pairs], out_ref.at[c], gg)


def _dft1_pairs(x, f1):
    nb, npair, _, rows, _ = x.shape
    kg = f1.shape[0] // 2 // SUBLANES
    return pl.pallas_call(
        _dft1_pairs_kernel, grid=(nb, npair, DFT_G // OUTER_GB),
        in_specs=[pl.BlockSpec(f1.shape, lambda b, c, g: (0, 0)),
                  pl.BlockSpec((None, None, OUTER_GB, rows, LANES), lambda b, c, g: (b, c, g, 0, 0))],
        out_specs=pl.BlockSpec((None, 2, kg, OUTER_GB * SUBLANES * SUBLANES, LANES), lambda b, c, g: (b, c, 0, g, 0)),
        out_shape=jax.ShapeDtypeStruct((nb, 2 * npair, kg, DFT_N2 * SUBLANES, LANES), jnp.uint32),
        compiler_params=_cparams(("parallel", "parallel", "parallel")), name="dft_outer_filter",
    )(f1, x)


def _inner_fwd(m1_ref, a_ref, kk):
    w = jnp.concatenate([_rows_of(a_ref, kk, DFT_N2, (c,)) for c in range(a_ref.shape[0])], axis=1)
    ar, ai = _unpack_pair(w)
    return _dot(m1_ref[kk], jnp.concatenate([ar, ai], axis=0))


def _spec_kernel(m1_ref, a_ref, out_ref):
    for kk in range(SUBLANES):
        x = _inner_fwd(m1_ref, a_ref, kk)
        for c in range(a_ref.shape[0]):
            out_ref[c, kk] = _pack_pair(x[:DFT_N2, c * LANES:(c + 1) * LANES], x[DFT_N2:, c * LANES:(c + 1) * LANES])


def _filter_spectrum(a, m1):
    depth, ncb, kg, rows, _ = a.shape
    n2 = DFT_N2
    return pl.pallas_call(
        _spec_kernel, grid=(kg, depth, ncb // MID_CB),
        in_specs=[pl.BlockSpec((SUBLANES, 2 * n2, 2 * n2), lambda k, l, c: (k, 0, 0)),
                  pl.BlockSpec((None, MID_CB, None, rows, LANES), lambda k, l, c: (l, c, k, 0, 0))],
        out_specs=pl.BlockSpec((None, MID_CB, SUBLANES, n2, LANES), lambda k, l, c: (l, c, k, 0, 0)),
        out_shape=jax.ShapeDtypeStruct((depth, ncb, kg * SUBLANES, n2, LANES), jnp.uint32),
        compiler_params=_cparams(("parallel", "parallel", "parallel")), name="filter_spectrum",
    )(m1, a)


def _mid_kernel(m1_ref, m2_ref, kf_ref, a_ref, out_ref):
    ncb = a_ref.shape[0]
    lanes = lambda parts: jnp.concatenate(parts, axis=1)
    for kk in range(SUBLANES):
        x = _inner_fwd(m1_ref, a_ref, kk)
        xr, xi = x[:DFT_N2], x[DFT_N2:]
        kr, ki = _unpack_pair(lanes([kf_ref[c, kk] for c in range(ncb)]), F32)
        yr = (xr * kr - xi * ki).astype(BF16)
        yi = (xr * ki + xi * kr).astype(BF16)
        y = _dot(m2_ref[kk], jnp.concatenate([yr, yi], axis=0))
        w = _pack_pair(y[:DFT_N2], y[DFT_N2:])
        for c in range(ncb):
            for g in range(DFT_G):
                out_ref[c, g, kk * SUBLANES:(kk + 1) * SUBLANES, :] = (
                    w[g * SUBLANES:(g + 1) * SUBLANES, c * LANES:(c + 1) * LANES])


def _conv_mid(a, kf, m1, m2, layer, order):
    nb, ncb, kg, rows, _ = a.shape
    n2 = DFT_N2
    mspec = pl.BlockSpec((SUBLANES, 2 * n2, 2 * n2), lambda k, c, b: (k, 0, 0))
    return pl.pallas_call(
        _mid_kernel, grid=(kg, ncb // MID_CB, nb),
        in_specs=[mspec, mspec,
                  pl.BlockSpec((None, MID_CB, SUBLANES, n2, LANES),
                               lambda k, c, b: (layer, order * (ncb // MID_CB) + c, k, 0, 0)),
                  pl.BlockSpec((None, MID_CB, None, rows, LANES), lambda k, c, b: (b, c, k, 0, 0))],
        out_specs=pl.BlockSpec((None, MID_CB, DFT_G, SUBLANES * SUBLANES, LANES), lambda k, c, b: (b, c, 0, k, 0)),
        out_shape=jax.ShapeDtypeStruct((nb, ncb, DFT_G, kg * SUBLANES * SUBLANES, LANES), jnp.uint32),
        compiler_params=_cparams(("parallel", "parallel", "arbitrary")), name="conv_mid",
    )(m1, m2, kf, a)


def _gate_kernel(gre_ref, gim_ref, f_ref, b_ref, *refs, first):
    if first:
        zg_ref, bias_ref, out_ref, next_ref = refs
    else:
        z_ref, gate_ref, bias_ref, out_ref = refs
    R, kp = gre_ref.shape
    bias = bias_ref[...]
    for gg in range(OUTER_GB):
        bre, bim = _unpack_pair(jnp.concatenate([_rows_of(b_ref, p, kp, (gg,)) for p in range(SUBLANES)], axis=1))
        y = _dot(gre_ref[...], bre) + _dot(gim_ref[...], bim)
        vals = []
        for p in range(SUBLANES):
            yp = y[:, p * LANES:(p + 1) * LANES]
            if first:
                z, gate = _unpack_pair(_rows_of(zg_ref, p, R, (gg,)), F32)
            else:
                z, gate = _rows_of(z_ref, p, R, (gg,)), _rows_of(gate_ref, p, R, (gg,))
            vals.append(gate * (yp + z * bias))
            out_ref[gg, pl.ds(p, R, stride=SUBLANES), :] = vals[p]
        if first:
            _dft1_block(f_ref, vals, next_ref, gg)


def _conv_out(bsp, tabs, zg, bias, first):
    nb, ncb, _, krows, _ = bsp.shape
    zshape = zg.shape if first else zg[0].shape
    rows = zshape[3]
    gre, gim, f1 = tabs["gre"], tabs["gim"], tabs["f1d"]
    kg = f1.shape[0] // 2 // SUBLANES
    full = lambda t: pl.BlockSpec(t.shape, lambda b, c, g: (0, 0))
    tile = pl.BlockSpec((None, None, OUTER_GB, rows, LANES), lambda b, c, g: (b, c, g, 0, 0))
    out_specs, out_shape = [tile], [jax.ShapeDtypeStruct(zshape, F32)]
    if first:
        out_specs.append(_dft1_spec(kg))
        out_shape.append(jax.ShapeDtypeStruct((nb, ncb, kg, DFT_N2 * SUBLANES, LANES), jnp.uint32))
    zg_ops = (zg,) if first else tuple(zg)
    return pl.pallas_call(
        functools.partial(_gate_kernel, first=first), grid=(nb, ncb, DFT_G // OUTER_GB),
        in_specs=[full(gre), full(gim), full(f1),
                  pl.BlockSpec((None, None, OUTER_GB, krows, LANES), lambda b, c, g: (b, c, g, 0, 0))]
                 + [tile] * len(zg_ops) + [pl.BlockSpec((None, 1, LANES), lambda b, c, g: (c, 0, 0))],
        out_specs=out_specs, out_shape=out_shape,
        compiler_params=_cparams(("parallel", "parallel", "parallel")), name="conv_gate",
    )(gre, gim, f1, bsp, *zg_ops, bias)


def _hyena_convs(vx1, x2, kf, bias, tabs, layer):
    b0, b1 = bias[0].reshape(HY_CB, 1, LANES), bias[1].reshape(HY_CB, 1, LANES)
    bsp = _conv_mid(_dft1(vx1, tabs["f1d"]), kf, tabs["m1"], tabs["m2"], layer, 0)
    z1, a2 = _conv_out(bsp, tabs, vx1, b0, True)
    bsp = _conv_mid(a2, kf, tabs["m1"], tabs["m2"], layer, 1)
    return _conv_out(bsp, tabs, (z1, x2), b1, False)[0]


def _attn_kernel(flag_ref, qt_ref, qtn_ref, k_ref, vt_ref, g_ref, o_ref, acc_scr, m_scr, s_scr, q_scr,
                 *, nk, tk, unroll):
    pair = pl.program_id(0) * pl.num_programs(1) + pl.program_id(1)
    fast = jnp.logical_and(flag_ref[2 * pair] == 1, flag_ref[2 * pair + 1] == 1)
    acc_scr[...] = jnp.zeros(acc_scr.shape, F32)

    @pl.when(fast)
    def _():
        q_scr[0] = qt_ref[0]
        q_scr[1] = qt_ref[1]
        q_scr[2] = qtn_ref[...]

        def scores(cc):
            hq = cc // nk
            c = cc - hq * nk
            hk = jnp.where(hq == 2, 0, hq)
            return _dot(k_ref[hk, pl.ds(pl.multiple_of(c * tk, tk), tk), :], q_scr[hq])

        @pl.when(pl.program_id(2) == 0)
        def _():
            s_scr[0] = scores(jnp.int32(0))

        def body(j, carry):
            hd = (j * unroll) // nk
            c0 = j * unroll - hd * nk
            pv = None
            for u in range(unroll):
                s_scr[(u + 1) % 2] = scores(j * unroll + u + 1)
                pt = jnp.exp2(s_scr[u % 2]).astype(BF16)
                d = _dot(vt_ref[hd, c0 + u], pt)
                pv = d if pv is None else pv + d
            acc_scr[hd] += pv
            return carry
        lax.fori_loop(0, 2 * nk // unroll, body, 0)

    @pl.when(jnp.logical_not(fast))
    def _():
        for hh in range(2):
            qt = qt_ref[hh]
            m_scr[...] = jnp.full(m_scr.shape, -jnp.inf, F32)

            def body(j, carry, hh=hh, qt=qt):
                kk = k_ref[hh, pl.ds(pl.multiple_of(j * tk, tk), tk), :]
                s = _dot(kk, qt)
                m_prev = m_scr[...]
                m_new = jnp.maximum(m_prev, jnp.max(s, axis=0, keepdims=True))
                pt = jnp.exp2(s - m_new).astype(BF16)
                acc_scr[hh] = jnp.exp2(m_prev - m_new) * acc_scr[hh] + _dot(vt_ref[hh, j], pt)
                m_scr[...] = m_new
                return carry
            lax.fori_loop(0, nk, body, 0)

    outs = []
    for hh in range(2):
        acc = acc_scr[hh]
        o = acc[:V_HEAD] / acc[V_HEAD:V_HEAD + 1]
        ms = jnp.mean(o * o, axis=0, keepdims=True)
        outs.append(o * lax.rsqrt(ms + EPS))
    ot = jnp.concatenate(outs, axis=0)
    o_ref[...] = (ot.T * g_ref[...]).astype(o_ref.dtype)


def _attention(flags, qt, k, vt, g_attn, nb, L):
    H = MLA_HEADS
    tq = min(512, L)
    nq = L // tq
    nk, tk = vt.shape[2], vt.shape[4]
    grid_spec = pltpu.PrefetchScalarGridSpec(
        num_scalar_prefetch=1, grid=(nb, H // 2, nq),
        in_specs=[pl.BlockSpec((None, 2, HEAD_PAD, tq), lambda b, h, i, f: (b, h, 0, i)),
                  pl.BlockSpec((None, None, HEAD_PAD, tq), lambda b, h, i, f: (b, 2 * h, 0, jnp.minimum(i + 1, nq - 1))),
                  pl.BlockSpec((None, 2, L, HEAD_PAD), lambda b, h, i, f: (b, h, 0, 0)),
                  pl.BlockSpec((None, 2, nk, V_PAD, tk), lambda b, h, i, f: (b, h, 0, 0, 0)),
                  pl.BlockSpec((1, 2 * V_HEAD), lambda b, h, i, f: (0, h))],
        out_specs=pl.BlockSpec((None, tq, 2 * V_HEAD), lambda b, h, i, f: (b, i, h)),
        scratch_shapes=[pltpu.VMEM((2, V_PAD, tq), F32), pltpu.VMEM((1, tq), F32), pltpu.VMEM((2, tk, tq), F32),
                        pltpu.VMEM((3, HEAD_PAD, tq), BF16)])
    unroll = math.gcd(nk, ATTN_UNROLL)
    assert unroll % 2 == 0
    return pl.pallas_call(
        functools.partial(_attn_kernel, nk=nk, tk=tk, unroll=unroll), grid_spec=grid_spec,
        out_shape=jax.ShapeDtypeStruct((nb, L, ATTN_WIDTH), BF16),
        compiler_params=_cparams(("parallel", "parallel", "arbitrary")), name="attention",
    )(flags, qt, qt, k, vt, g_attn)


def _fast_flags(stats):
    H = MLA_HEADS
    qn = jnp.max(stats[:, :, 0:H, :], axis=(1, 3))
    kn = jnp.max(stats[:, :, H:2 * H, :], axis=(1, 3))
    vm = jnp.max(stats[:, :, 2 * H, :], axis=(1, 2))
    ok = jnp.logical_and(qn * kn <= FAST_S_MAX * FAST_S_MAX, (vm <= FAST_V_MAX)[:, None])
    return ok.astype(jnp.int32).reshape(-1)


def _mix_mlp_kernel(x_ref, zh_ref, an_ref, ghy_ref, gsum_ref, wo_ref, gpost_ref, gmpre_ref, wup_ref,
                    wdn_ref, gmpost_ref, out_ref):
    zh = jnp.concatenate(
        [jnp.concatenate([zh_ref[j, g, nl * SUBLANES:(nl + 1) * SUBLANES, :] for j in range(HY_CB)], axis=1)
         for nl in range(zh_ref.shape[2] // SUBLANES) for g in range(DFT_G)], axis=0)
    ms = _dot((zh * zh).astype(BF16), gsum_ref[...]) * (HY_GROUPS / HY_WIDTH)
    hn = (zh * lax.rsqrt(ms + EPS) * ghy_ref[...]).astype(BF16)
    mix = _dot(hn, wo_ref[0:HY_WIDTH, :]) + _dot(an_ref[...], wo_ref[HY_WIDTH:, :])
    x = x_ref[...] + _rms(mix, gpost_ref[...])
    h = _rms(x, gmpre_ref[...]).astype(BF16)
    up = jnp.maximum(_dot(h, wup_ref[...]), 0.0)
    m = _dot((up * up).astype(BF16), wdn_ref[...])
    out_ref[...] = x + _rms(m, gmpost_ref[...])


def _mix_mlp(x, zh, an, lw, tabs, L):
    M, D = x.shape
    tm = min(512, L)
    tps = L // tm
    dff = lw["wup"].shape[1]
    mw = HY_WIDTH + ATTN_WIDTH
    const = lambda *shape: pl.BlockSpec(shape, lambda i: (0,) * len(shape), pipeline_mode=pl.Buffered(1))
    rows = lambda w: pl.BlockSpec((tm, w), lambda i: (i, 0))
    return pl.pallas_call(
        _mix_mlp_kernel, grid=(M // tm,),
        in_specs=[rows(D), pl.BlockSpec((None, HY_CB, DFT_G, tm // DFT_G, LANES),
                                        lambda i: (i // tps, 0, 0, i % tps, 0)),
                  rows(ATTN_WIDTH), const(1, HY_WIDTH), const(HY_WIDTH, HY_WIDTH),
                  const(mw, D), const(1, D), const(1, D), const(D, dff), const(dff, D), const(1, D)],
        out_specs=rows(D), out_shape=jax.ShapeDtypeStruct((M, D), F32),
        compiler_params=_cparams(("parallel",)), name="mix_mlp",
    )(x, zh, an, lw["g_hy"], tabs["gsum"], lw["wo"], lw["g_post"], lw["g_mpre"], lw["wup"], lw["wdn"],
      lw["g_mpost"])


def _tables(L):
    n = 2 * L
    n2 = DFT_N2
    n1 = n // n2
    nh = n1 // 2
    kp = -(-(nh + 1) // SUBLANES) * SUBLANES
    two_pi = 2.0 * math.pi

    k1 = jnp.arange(kp, dtype=jnp.int32)
    valid = (k1 <= nh)
    def outer(ncols):
        nn = jnp.arange(ncols, dtype=jnp.int32)
        ang = ((k1[:, None] * nn[None, :]) % n1).astype(F32) * (two_pi / n1)
        c = jnp.where(valid[:, None], jnp.cos(ang), 0.0)
        s = jnp.where(valid[:, None], -jnp.sin(ang), 0.0)
        return jnp.concatenate([c, s], axis=0).astype(BF16)
    f1d = outer(nh)
    f1f = outer(n1)
    nn = jnp.arange(nh, dtype=jnp.int32)
    ang = ((nn[:, None] * k1[None, :]) % n1).astype(F32) * (two_pi / n1)
    wgt = jnp.where(valid, jnp.where((k1 == 0) | (k1 == nh), 1.0, 2.0), 0.0) / n
    gre = (jnp.cos(ang) * wgt[None, :]).astype(BF16)
    gim = (-jnp.sin(ang) * wgt[None, :]).astype(BF16)
    a2 = jnp.arange(n2, dtype=jnp.int32)
    idx = (a2[None, :, None] * a2[None, None, :] * n1 + a2[None, None, :] * k1[:, None, None]) % n
    ph = idx.astype(F32) * (two_pi / n)
    gr, gi = jnp.cos(ph), -jnp.sin(ph)
    m1 = jnp.concatenate([jnp.concatenate([gr, -gi], axis=2), jnp.concatenate([gi, gr], axis=2)], axis=1)
    m2 = jnp.swapaxes(m1, 1, 2)
    inv = 1.0 / (ROPE_BASE ** (jnp.arange(0, QK_ROPE, 2, dtype=F32) / QK_ROPE))
    ang = jnp.arange(L, dtype=F32)[:, None] * inv[None, :]
    cos, sin = jnp.cos(ang), jnp.sin(ang)
    rcos = jnp.concatenate([cos, cos], axis=1)
    rsin = jnp.concatenate([sin, sin], axis=1)
    kcs =jnp.concatenate([cos, cos, sin, sin, jnp.zeros((L, HEAD_PAD - 2 * QK_ROPE), F32)], axis=1)
    cc = jnp.arange(MLA_HEADS * HEAD_PAD)
    src = jnp.arange(HEAD_PAD)
    pk = ((cc[None, :] % HEAD_PAD) - QK_NOPE == src[:, None]) & (src[:, None] < QK_ROPE)
    grp = jnp.arange(HY_WIDTH) // (HY_WIDTH // HY_GROUPS)
    gsum = (grp[:, None] == grp[None, :]).astype(BF16)
    t = jnp.linspace(0.0, 1.0, L, dtype=F32)[:, None]
    omega = (two_pi / L) * jnp.arange(L, dtype=F32)
    bands = jnp.linspace(1e-4, HY_BANDS - 1, HY_BANDS, dtype=F32)
    phase = omega[:, None] * bands[None, :]
    z = jnp.concatenate([t, jnp.cos(phase), -jnp.sin(phase), jnp.zeros((L, HY_EMB_PAD - HY_EMB), F32)], axis=-1)
    ztab = jnp.concatenate([z, z[:1], z[:0:-1]], axis=0)
    return dict(f1d=f1d, f1f=f1f, gre=gre, gim=gim, m1=m1.astype(BF16), m2=m2.astype(BF16), rcos_t=rcos.T,
                rsin_t=rsin.T, kcs=kcs, pk=pk.astype(BF16), gsum=gsum, ztab=ztab)


def _rot_half_cols(w):
    half = QK_ROPE // 2
    return jnp.concatenate([-w[..., half:], w[..., :half]], axis=-1)


def _layer_weights(i, p, tabs):
    D = p["w_in"].shape[1]
    H = MLA_HEADS
    hw3 = 3 * HY_WIDTH
    w_in = p["w_in"][i]
    kpe = w_in[:, hw3 + Q_RANK + KV_RANK:]
    win = jnp.concatenate([w_in, _rot_half_cols(kpe),
                           jnp.zeros((D, HEAD_PAD - 2 * QK_ROPE), F32)], axis=1).astype(BF16)
    dq = QK_NOPE + QK_ROPE
    wq = p["mla_w_uq"][i].reshape(Q_RANK, H, dq)
    wqt = jnp.concatenate([wq.reshape(Q_RANK, H * dq),
                           _rot_half_cols(wq[..., QK_NOPE:]).reshape(Q_RANK, H * QK_ROPE)], axis=1).T
    wkv =p["mla_w_ukv"][i].reshape(KV_RANK, H, QK_NOPE + V_HEAD)
    wk = jnp.concatenate([wkv[..., :QK_NOPE], jnp.zeros((KV_RANK, H, HEAD_PAD - QK_NOPE), F32)], axis=2)
    wv = jnp.concatenate([wkv[..., QK_NOPE:], jnp.zeros((KV_RANK, H, V_PAD - V_HEAD), F32)], axis=2)
    row = lambda a: a.reshape(1, -1)
    return dict(
        win=win, g_pre=row(p["norm_mix_pre"][i]), conv_w=p["hy_conv_w"][i], conv_b=row(p["hy_conv_b"][i]),
        q_g=row(p["mla_q_norm"][i]), wqt=wqt.astype(BF16), kv_g=row(p["mla_kv_norm"][i]),
        wkp=jnp.concatenate([wk.reshape(KV_RANK, H * HEAD_PAD).astype(BF16), tabs["pk"]], axis=0),
        wvt=wv.reshape(KV_RANK, H * V_PAD).T.astype(BF16),
        g_hy=row(p["grp_norm_hy"][i]), g_attn=row(p["grp_norm_attn"][i]), wo=p["w_out"][i].astype(BF16),
        g_post=row(p["norm_mix_post"][i]), g_mpre=row(p["norm_mlp_pre"][i]), g_mpost=row(p["norm_mlp_post"][i]),
        wup=p["w_mlp_up"][i].astype(BF16), wdn=p["w_mlp_down"][i].astype(BF16),
    )


def _filter_weights(p):
    depth = p["hy_ffn_w1"].shape[0]
    oc = HY_ORDER * HY_WIDTH
    w1 = jnp.pad(p["hy_ffn_w1"], ((0, 0), (0, HY_EMB_PAD - HY_EMB), (0, 0)))
    w3 = p["hy_ffn_w3"].reshape(depth, HY_FFN, HY_ORDER, 2, HY_WIDTH).transpose(0, 3, 1, 2, 4)
    dec = p["hy_decay"].transpose(0, 2, 1, 3).reshape(depth, 2, 1, oc)
    tr = lambda a: jnp.swapaxes(a, 1, 2)
    w3 = w3.reshape(depth, 2, HY_FFN, oc)
    w3_hi = w3.astype(BF16)
    w3_lo = (w3 - w3_hi.astype(F32)).astype(BF16)
    return dict(w1=tr(w1), b1=p["hy_ffn_b1"][:, :, None], sf=tr(p["hy_sin_freq"]), w2=tr(p["hy_ffn_w2"]),
                b2=p["hy_ffn_b2"][:, :, None], w3=jnp.stack([w3_hi, w3_lo], axis=2), dec=dec)


def kernel(x_prompt, x_sample, w_in, hy_conv_w, hy_conv_b, hy_ffn_w1, hy_ffn_b1, hy_ffn_w2, hy_ffn_b2,
           hy_ffn_w3, hy_sin_freq, hy_decay, hy_bias, mla_q_norm, mla_w_uq, mla_kv_norm, mla_w_ukv,
           grp_norm_hy, grp_norm_attn, w_out, norm_mix_pre, norm_mix_post, norm_mlp_pre, norm_mlp_post,
           w_mlp_up, w_mlp_down):
    p = dict(w_in=w_in, hy_conv_w=hy_conv_w, hy_conv_b=hy_conv_b, hy_ffn_w1=hy_ffn_w1, hy_ffn_b1=hy_ffn_b1,
             hy_ffn_w2=hy_ffn_w2, hy_ffn_b2=hy_ffn_b2, hy_ffn_w3=hy_ffn_w3, hy_sin_freq=hy_sin_freq,
             hy_decay=hy_decay, hy_bias=hy_bias, mla_q_norm=mla_q_norm, mla_w_uq=mla_w_uq,
             mla_kv_norm=mla_kv_norm, mla_w_ukv=mla_w_ukv, grp_norm_hy=grp_norm_hy,
             grp_norm_attn=grp_norm_attn, w_out=w_out, norm_mix_pre=norm_mix_pre, norm_mix_post=norm_mix_post,
             norm_mlp_pre=norm_mlp_pre, norm_mlp_post=norm_mlp_post, w_mlp_up=w_mlp_up, w_mlp_down=w_mlp_down)
    bp, L, D = x_prompt.shape
    bs, Ls, _ = x_sample.shape
    assert L == Ls and L % (DFT_N2 * SUBLANES) == 0
    nb = bp + bs
    depth = w_in.shape[0]

    tabs = _tables(L)
    kc = _filters(tabs["ztab"], _filter_weights(p), L)
    kf = _filter_spectrum(_dft1_pairs(kc, tabs["f1f"]), tabs["m1"])

    x = jnp.concatenate([x_prompt.reshape(bp * L, D), x_sample.reshape(bs * L, D)], axis=0)
    for i in range(depth):
        lw = _layer_weights(i, p, tabs)
        vx1, x2, qt, k, vt, stats = _inproj(x, lw, tabs, nb, L)
        z = _hyena_convs(vx1, x2, kf, hy_bias[i], tabs, i)
        an = _attention(_fast_flags(stats), qt, k, vt, lw["g_attn"], nb, L).reshape(nb * L, ATTN_WIDTH)
        x = _mix_mlp(x, z, an, lw, tabs, L)
    return (x[:bp * L].reshape(bp, L, D), x[bp * L:].reshape(bs, L, D))
```

```python
import functools
import math

import jax
import jax.numpy as jnp
from jax import lax
from jax.experimental import pallas as pl
from jax.experimental.pallas import tpu as pltpu

F32 = jnp.float32
BF16 = jnp.bfloat16

EPS = 1e-6
HY_WIDTH = 512
HY_GROUPS = 8
HY_ORDER = 2
HY_BANDS = 16
HY_EMB = 2 * HY_BANDS + 1
HY_EMB_PAD = 40
HY_FFN = 64
MLA_HEADS = 8
QK_NOPE = 64
QK_ROPE = 32
V_HEAD = 64
Q_RANK = 256
KV_RANK = 128
ROPE_BASE = 10000.0
HEAD_PAD = 128
V_PAD = 128
ATTN_WIDTH = MLA_HEADS * V_HEAD

LANES = 128
SUBLANES = 8
HY_CB = HY_WIDTH // LANES
DFT_N2 = 128
DFT_G = DFT_N2 // SUBLANES
MID_CB = 2
OUTER_GB = 8
HALO = 16
VMEM_LIMIT = 56 * 1024 * 1024
ATTN_UNROLL = 32
STAT_ROWS = 24

FAST_S_MAX = 64.0
FAST_V_MAX = 2.0 ** 30


def _cparams(sem):
    return pltpu.CompilerParams(dimension_semantics=sem, vmem_limit_bytes=VMEM_LIMIT)


def _dot(a, b):
    return jnp.dot(a, b, preferred_element_type=F32)


def _dot_nt(a, b):
    return lax.dot_general(a, b, (((1,), (1,)), ((), ())), preferred_element_type=F32)


def _rms(x, g):
    return x * lax.rsqrt(jnp.mean(x * x, axis=-1, keepdims=True) + EPS) * g


def _store_grouped(out_ref, val, lane0, ncb):
    for j in range(ncb):
        for nl in range(val.shape[0] // DFT_N2):
            for g in range(DFT_G):
                r0 = nl * DFT_N2 + g * SUBLANES
                out_ref[j, g, nl * SUBLANES:(nl + 1) * SUBLANES, :] = (
                    val[r0:r0 + SUBLANES, lane0 + j * LANES:lane0 + (j + 1) * LANES])


def _fold_lanes(row):
    parts = [row[:, j * LANES:(j + 1) * LANES] for j in range(row.shape[1] // LANES)]
    return functools.reduce(jnp.maximum, parts)


def _inproj_kernel(xp_ref, x_ref, xn_ref, gpre_ref, win_ref, cw_ref, cb_ref, qg_ref, wqt_ref,
                   kvg_ref, wkp_ref, wvt_ref, qcos_ref, qsin_ref, kcs_ref,
                   v_out, x1_out, x2_out, qt_out, k_out, vt_out, st_out, h_scr, pe_scr,
                   *, tm, tiles_per_seq, scale):
    i = pl.program_id(0)
    t_idx = i % tiles_per_seq
    g = gpre_ref[...]
    h_scr[0:HALO, :] = _rms(xp_ref[...], g).astype(BF16)
    h_scr[HALO:HALO + tm, :] = _rms(x_ref[...], g).astype(BF16)
    h_scr[HALO + tm:2 * HALO + tm, :] = _rms(xn_ref[...], g).astype(BF16)
    hw3 = 3 * HY_WIDTH
    core = _dot(h_scr[HALO:HALO + tm, :], win_ref[:, hw3:])
    pe_scr[...] = _dot(h_scr[...], win_ref[:, 0:hw3])
    row = lax.broadcasted_iota(jnp.int32, (tm, 1), 0)
    prev = pe_scr[HALO - 1:HALO - 1 + tm, :]
    cur = pe_scr[HALO:HALO + tm, :]
    nxt = pe_scr[HALO + 1:HALO + 1 + tm, :]
    prev = jnp.where(jnp.logical_and(row == 0, t_idx == 0), 0.0, prev)
    nxt = jnp.where(jnp.logical_and(row == tm - 1, t_idx == tiles_per_seq - 1), 0.0, nxt)
    u = prev * cw_ref[0:1, :] + cur * cw_ref[1:2, :] + nxt * cw_ref[2:3, :] + cb_ref[...]
    for o, out in enumerate((v_out, x1_out, x2_out)):
        _store_grouped(out, u, o * HY_WIDTH, HY_CB)

    cq = core[:, 0:Q_RANK]
    cqn =_rms(cq, qg_ref[...]).astype(BF16)
    dq = QK_NOPE + QK_ROPE
    qall = _dot_nt(wqt_ref[...], cqn)
    rcos = qcos_ref[...]
    rsin = qsin_ref[...]
    zpad = jnp.zeros((HEAD_PAD - dq, tm), F32)
    for h in range(MLA_HEADS):
        rot = qall[MLA_HEADS * dq + h * QK_ROPE:MLA_HEADS * dq + (h + 1) * QK_ROPE]
        rope = qall[h * dq + QK_NOPE:(h + 1) * dq] * rcos + rot * rsin
        qh = (jnp.concatenate([qall[h * dq:h * dq + QK_NOPE], rope, zpad], axis=0) * scale).astype(BF16)
        qt_out[h] = qh
        qf = qh.astype(F32)
        st_out[h:h + 1, :] = _fold_lanes(jnp.sum(qf * qf, axis=0, keepdims=True))

    ckv = core[:, Q_RANK:Q_RANK + KV_RANK]
    ckvn = _rms(ckv, kvg_ref[...]).astype(BF16)
    kp = core[:, Q_RANK + KV_RANK:] * kcs_ref[...]
    kp = kp + pltpu.roll(kp, HEAD_PAD - QK_ROPE, axis=1)
    kk = _dot(jnp.concatenate([ckvn, kp.astype(BF16)], axis=1), wkp_ref[...])
    for h in range(MLA_HEADS):
        kh = kk[:, h * HEAD_PAD:(h + 1) * HEAD_PAD].astype(BF16)
        k_out[h] = kh
        kf = kh.astype(F32)
        kn = jnp.max(jnp.sum(kf * kf, axis=1, keepdims=True), axis=0, keepdims=True)
        st_out[MLA_HEADS + h:MLA_HEADS + h + 1, :] = jnp.broadcast_to(kn, (1, LANES))
    vt = _dot_nt(wvt_ref[...], ckvn)
    st_out[2 * MLA_HEADS:2 * MLA_HEADS + 1, :] = _fold_lanes(jnp.max(jnp.abs(vt), axis=0, keepdims=True))
    st_out[2 * MLA_HEADS + 1:, :] = jnp.zeros((STAT_ROWS - 2 * MLA_HEADS - 1, LANES), F32)
    frow = lax.broadcasted_iota(jnp.int32, (MLA_HEADS * V_PAD, 1), 0)
    vt = vt + jnp.where(frow % V_PAD == V_HEAD, 1.0, 0.0)
    for h in range(MLA_HEADS):
        vt_out[h] = vt[h * V_PAD:(h + 1) * V_PAD, :].astype(BF16)


def _inproj(x, lw, tabs, nb, L):
    M, D = x.shape
    tm = min(512, L)
    tps = L // tm
    nt = M // tm
    hb = tm // HALO
    nhb = M // HALO
    H = MLA_HEADS
    wcols = lw["win"].shape[1]
    const = lambda *shape: pl.BlockSpec(shape, lambda i: (0,) * len(shape))
    tab_t = pl.BlockSpec((QK_ROPE, tm), lambda i: (0, i % tps))
    in_specs = [
        pl.BlockSpec((HALO, D), lambda i: (jnp.maximum(i * hb - 1, 0), 0)),
        pl.BlockSpec((tm, D), lambda i: (i, 0)),
        pl.BlockSpec((HALO, D), lambda i: (jnp.minimum((i + 1) * hb, nhb - 1), 0)),
        const(1, D), const(D, wcols), const(3, 3 * HY_WIDTH), const(1, 3 * HY_WIDTH),
        const(1, Q_RANK), const(H * HEAD_PAD, Q_RANK),
        const(1, KV_RANK), const(KV_RANK + HEAD_PAD, H * HEAD_PAD),
        const(H * V_PAD, KV_RANK),
        tab_t, tab_t, pl.BlockSpec((tm, HEAD_PAD), lambda i: (i % tps, 0)),
    ]
    hy_spec = pl.BlockSpec((None, HY_CB, DFT_G, tm // DFT_G, LANES), lambda i: (i // tps, 0, 0, i % tps, 0))
    out_specs = [
        hy_spec, hy_spec, hy_spec,
        pl.BlockSpec((None, H, HEAD_PAD, tm), lambda i: (i // tps, 0, 0, i % tps)),
        pl.BlockSpec((None, H, tm, HEAD_PAD), lambda i: (i // tps, 0, i % tps, 0)),
        pl.BlockSpec((None, H, None, V_PAD, tm), lambda i: (i // tps, 0, i % tps, 0, 0)),
        pl.BlockSpec((None, None, STAT_ROWS, LANES), lambda i: (i // tps, i % tps, 0, 0)),
    ]
    hy_shape = jax.ShapeDtypeStruct((nb, HY_CB, DFT_G, L // DFT_G, LANES), F32)
    out_shape = [
        hy_shape, hy_shape, hy_shape,
        jax.ShapeDtypeStruct((nb, H, HEAD_PAD, L), BF16),
        jax.ShapeDtypeStruct((nb, H, L, HEAD_PAD), BF16),
        jax.ShapeDtypeStruct((nb, H, tps, V_PAD, tm), BF16),
        jax.ShapeDtypeStruct((nb, tps, STAT_ROWS, LANES), F32),
    ]
    scale = float((QK_NOPE + QK_ROPE) ** -0.5 * math.log2(math.e))
    return pl.pallas_call(
        functools.partial(_inproj_kernel, tm=tm, tiles_per_seq=tps, scale=scale),
        grid=(nt,), in_specs=in_specs, out_specs=out_specs, out_shape=out_shape,
        scratch_shapes=[pltpu.VMEM((tm + 2 * HALO, D), BF16),
                        pltpu.VMEM((tm + 2 * HALO, 3 * HY_WIDTH), F32)],
        compiler_params=_cparams(("parallel",)), name="inproj",
    )(x, x, x, lw["g_pre"], lw["win"], lw["conv_w"], lw["conv_b"], lw["q_g"], lw["wqt"],
      lw["kv_g"], lw["wkp"], lw["wvt"], tabs["rcos_t"], tabs["rsin_t"], tabs["kcs"])


def _filter_kernel(t_ref, zt_ref, w1_ref, b1_ref, sf_ref, w2_ref, b2_ref, w3_ref, dec_ref, out_ref, *, rb, L):
    hi = lax.Precision.HIGHEST
    r = pl.program_id(1)
    h = jnp.sin(sf_ref[:, 0:1] * (jnp.dot(w1_ref[...], zt_ref[...], precision=hi, preferred_element_type=F32)
                                  + b1_ref[...]))
    h = jnp.sin(sf_ref[:, 1:2] * (jnp.dot(w2_ref[...], h, precision=hi, preferred_element_type=F32) + b2_ref[...]))
    tn = lambda a, b: lax.dot_general(a, b, (((0,), (0,)), ((), ())), preferred_element_type=F32)
    h_hi = h.astype(BF16)
    h_lo = (h - h_hi.astype(F32)).astype(BF16)
    k = tn(h_hi, w3_ref[0]) + (tn(h_hi, w3_ref[1]) + tn(h_lo, w3_ref[0]))
    k = k * jnp.exp(-t_ref[...] * jnp.abs(dec_ref[...]))
    row = r * rb + lax.broadcasted_iota(jnp.int32, (rb, 1), 0)
    k = jnp.where(row == L, 0.0, k)
    npair = out_ref.shape[0]
    w = jnp.concatenate([_pack_pair(k[:, 2 * j * LANES:(2 * j + 1) * LANES], k[:, (2 * j + 1) * LANES:(2 * j + 2) * LANES])
                         for j in range(npair)], axis=1)
    _store_grouped(out_ref, w, 0, npair)


def _filters(ztab, fw, L):
    depth = fw["w1"].shape[0]
    rows = 2 * L
    rb = min(512, L)
    nblk = rows // rb
    half = nblk // 2
    oc = HY_ORDER * HY_WIDTH
    lay = lambda *shape: pl.BlockSpec((None,) + shape, lambda l, r: (l,) + (0,) * len(shape))
    in_specs = [
        pl.BlockSpec((rb, 1), lambda l, r: (r, 0)), pl.BlockSpec((HY_EMB_PAD, rb), lambda l, r: (0, r)),
        lay(HY_FFN, HY_EMB_PAD), lay(HY_FFN, 1), lay(HY_FFN, 2), lay(HY_FFN, HY_FFN), lay(HY_FFN, 1),
        pl.BlockSpec((None, None, 2, HY_FFN, oc), lambda l, r: (l, r // half, 0, 0, 0)),
        pl.BlockSpec((None, None, 1, oc), lambda l, r: (l, r // half, 0, 0)),
    ]
    return pl.pallas_call(
        functools.partial(_filter_kernel, rb=rb, L=L),
        grid=(depth, nblk), in_specs=in_specs,
        out_specs=pl.BlockSpec((None, oc // LANES // 2, DFT_G, rb // DFT_G, LANES), lambda l, r: (l, 0, 0, r, 0)),
        out_shape=jax.ShapeDtypeStruct((depth, oc // LANES // 2, DFT_G, rows // DFT_G, LANES), jnp.uint32),
        compiler_params=_cparams(("parallel", "parallel")), name="hyena_filter",
    )(ztab[:, 0:1], ztab.T, fw["w1"], fw["b1"], fw["sf"], fw["w2"], fw["b2"], fw["w3"], fw["dec"])


def _rows_of(ref, p, n, lead=()):
    return ref[lead + (pl.ds(p, n, stride=SUBLANES), slice(None))]


def _pack_pair(re, im):
    hi = lax.bitcast_convert_type(re.astype(BF16).astype(F32), jnp.uint32)
    lo = lax.bitcast_convert_type(im.astype(BF16).astype(F32), jnp.uint32)
    return hi | (lo >> 16)


def _unpack_pair(w, dtype=BF16):
    re = lax.bitcast_convert_type(w & jnp.uint32(0xFFFF0000), F32)
    im = lax.bitcast_convert_type(w << 16, F32)
    return re.astype(dtype), im.astype(dtype)


def _dft1_block(f_ref, xs, out_ref, gg):
    kp = f_ref.shape[0] // 2
    res = _dot(f_ref[...], jnp.concatenate([x.astype(BF16) for x in xs], axis=1))
    w = _pack_pair(res[:kp], res[kp:])
    for kg in range(kp // SUBLANES):
        for p in range(SUBLANES):
            r0 = (gg * SUBLANES + p) * SUBLANES
            out_ref[kg, r0:r0 + SUBLANES, :] = w[kg * SUBLANES:(kg + 1) * SUBLANES, p * LANES:(p + 1) * LANES]


def _dft1_kernel(f_ref, x_ref, out_ref):
    R = f_ref.shape[1]
    for gg in range(OUTER_GB):
        _dft1_block(f_ref, [_rows_of(x_ref, p, R, (gg,)) for p in range(SUBLANES)], out_ref, gg)


def _dft1_spec(kg):
    return pl.BlockSpec((None, None, kg, OUTER_GB * SUBLANES * SUBLANES, LANES), lambda b, c, g: (b, c, 0, g, 0))


def _dft1(x, f1):
    nb, ncb, _, rows, _ = x.shape
    kp = f1.shape[0] // 2
    kg = kp // SUBLANES
    return pl.pallas_call(
        _dft1_kernel, grid=(nb, ncb, DFT_G // OUTER_GB),
        in_specs=[pl.BlockSpec(f1.shape, lambda b, c, g: (0, 0)),
                  pl.BlockSpec((None, None, OUTER_GB, rows, LANES), lambda b, c, g: (b, c, g, 0, 0))],
        out_specs=_dft1_spec(kg),
        out_shape=jax.ShapeDtypeStruct((nb, ncb, kg, DFT_N2 * SUBLANES, LANES), jnp.uint32),
        compiler_params=_cparams(("parallel", "parallel", "parallel")), name="dft_outer",
    )(f1, x)


def _dft1_pairs_kernel(f_ref, x_ref, out_ref):
    R = f_ref.shape[1]
    for gg in range(OUTER_GB):
        pairs = [_unpack_pair(_rows_of(x_ref, p, R, (gg,))) for p in range(SUBLANES)]
        for c in range(2):
            _dft1_block(f_ref, [pr[c] for pr in pairs], out_ref.at[c], gg)


def _dft1_pairs(x, f1):
    nb, npair, _, rows, _ = x.shape
    kg = f1.shape[0] // 2 // SUBLANES
    return pl.pallas_call(
        _dft1_pairs_kernel, grid=(nb, npair, DFT_G // OUTER_GB),
        in_specs=[pl.BlockSpec(f1.shape, lambda b, c, g: (0, 0)),
                  pl.BlockSpec((None, None, OUTER_GB, rows, LANES), lambda b, c, g: (b, c, g, 0, 0))],
        out_specs=pl.BlockSpec((None, 2, kg, OUTER_GB * SUBLANES * SUBLANES, LANES), lambda b, c, g: (b, c, 0, g, 0)),
        out_shape=jax.ShapeDtypeStruct((nb, 2 * npair, kg, DFT_N2 * SUBLANES, LANES), jnp.uint32),
        compiler_params=_cparams(("parallel", "parallel", "parallel")), name="dft_outer_filter",
    )(f1, x)


def _inner_fwd(m1_ref, a_ref, kk):
    w = jnp.concatenate([_rows_of(a_ref, kk, DFT_N2, (c,)) for c in range(a_ref.shape[0])], axis=1)
    ar, ai = _unpack_pair(w)
    return _dot(m1_ref[kk], jnp.concatenate([ar, ai], axis=0))


def _spec_kernel(m1_ref, a_ref, out_ref):
    for kk in range(SUBLANES):
        x = _inner_fwd(m1_ref, a_ref, kk)
        for c in range(a_ref.shape[0]):
            out_ref[c, kk] = _pack_pair(x[:DFT_N2, c * LANES:(c + 1) * LANES], x[DFT_N2:, c * LANES:(c + 1) * LANES])


def _filter_spectrum(a, m1):
    depth, ncb, kg, rows, _ = a.shape
    n2 = DFT_N2
    return pl.pallas_call(
        _spec_kernel, grid=(kg, depth, ncb // MID_CB),
        in_specs=[pl.BlockSpec((SUBLANES, 2 * n2, 2 * n2), lambda k, l, c: (k, 0, 0)),
                  pl.BlockSpec((None, MID_CB, None, rows, LANES), lambda k, l, c: (l, c, k, 0, 0))],
        out_specs=pl.BlockSpec((None, MID_CB, SUBLANES, n2, LANES), lambda k, l, c: (l, c, k, 0, 0)),
        out_shape=jax.ShapeDtypeStruct((depth, ncb, kg * SUBLANES, n2, LANES), jnp.uint32),
        compiler_params=_cparams(("parallel", "parallel", "parallel")), name="filter_spectrum",
    )(m1, a)


def _mid_kernel(m1_ref, m2_ref, kf_ref, a_ref, out_ref):
    ncb = a_ref.shape[0]
    lanes = lambda parts: jnp.concatenate(parts, axis=1)
    for kk in range(SUBLANES):
        x = _inner_fwd(m1_ref, a_ref, kk)
        xr, xi = x[:DFT_N2], x[DFT_N2:]
        kr, ki = _unpack_pair(lanes([kf_ref[c, kk] for c in range(ncb)]), F32)
        yr = (xr * kr - xi * ki).astype(BF16)
        yi = (xr * ki + xi * kr).astype(BF16)
        y = _dot(m2_ref[kk], jnp.concatenate([yr, yi], axis=0))
        w = _pack_pair(y[:DFT_N2], y[DFT_N2:])
        for c in range(ncb):
            for g in range(DFT_G):
                out_ref[c, g, kk * SUBLANES:(kk + 1) * SUBLANES, :] = (
                    w[g * SUBLANES:(g + 1) * SUBLANES, c * LANES:(c + 1) * LANES])


def _conv_mid(a, kf, m1, m2, layer, order):
    nb, ncb, kg, rows, _ = a.shape
    n2 = DFT_N2
    mspec = pl.BlockSpec((SUBLANES, 2 * n2, 2 * n2), lambda k, c, b: (k, 0, 0))
    return pl.pallas_call(
        _mid_kernel, grid=(kg, ncb // MID_CB, nb),
        in_specs=[mspec, mspec,
                  pl.BlockSpec((None, MID_CB, SUBLANES, n2, LANES),
                               lambda k, c, b: (layer, order * (ncb // MID_CB) + c, k, 0, 0)),
                  pl.BlockSpec((None, MID_CB, None, rows, LANES), lambda k, c, b: (b, c, k, 0, 0))],
        out_specs=pl.BlockSpec((None, MID_CB, DFT_G, SUBLANES * SUBLANES, LANES), lambda k, c, b: (b, c, 0, k, 0)),
        out_shape=jax.ShapeDtypeStruct((nb, ncb, DFT_G, kg * SUBLANES * SUBLANES, LANES), jnp.uint32),
        compiler_params=_cparams(("parallel", "parallel", "arbitrary")), name="conv_mid",
    )(m1, m2, kf, a)


def _gate_kernel(gre_ref, gim_ref, f_ref, b_ref, z_ref, gate_ref, bias_ref, out_ref, *next_ref):
    R, kp = gre_ref.shape
    bias = bias_ref[...]
    for gg in range(OUTER_GB):
        bre, bim = _unpack_pair(jnp.concatenate([_rows_of(b_ref, p, kp, (gg,)) for p in range(SUBLANES)], axis=1))
        y = _dot(gre_ref[...], bre) + _dot(gim_ref[...], bim)
        vals = []
        for p in range(SUBLANES):
            yp = y[:, p * LANES:(p + 1) * LANES]
            vals.append(_rows_of(gate_ref, p, R, (gg,)) * (yp + _rows_of(z_ref, p, R, (gg,)) * bias))
            out_ref[gg, pl.ds(p, R, stride=SUBLANES), :] = vals[p]
        if next_ref:
            _dft1_block(f_ref, vals, next_ref[0], gg)


def _conv_out(bsp, tabs, z, gate, bias, with_next):
    nb, ncb, _, krows, _ = bsp.shape
    rows = z.shape[3]
    gre, gim, f1 = tabs["gre"], tabs["gim"], tabs["f1d"]
    kg = f1.shape[0] // 2 // SUBLANES
    full = lambda t: pl.BlockSpec(t.shape, lambda b, c, g: (0, 0))
    tile = pl.BlockSpec((None, None, OUTER_GB, rows, LANES), lambda b, c, g: (b, c, g, 0, 0))
    out_specs, out_shape = [tile], [jax.ShapeDtypeStruct(z.shape, F32)]
    if with_next:
        out_specs.append(_dft1_spec(kg))
        out_shape.append(jax.ShapeDtypeStruct((nb, ncb, kg, DFT_N2 * SUBLANES, LANES), jnp.uint32))
    return pl.pallas_call(
        _gate_kernel, grid=(nb, ncb, DFT_G // OUTER_GB),
        in_specs=[full(gre), full(gim), full(f1),
                  pl.BlockSpec((None, None, OUTER_GB, krows, LANES), lambda b, c, g: (b, c, g, 0, 0)),
                  tile, tile, pl.BlockSpec((None, 1, LANES), lambda b, c, g: (c, 0, 0))],
        out_specs=out_specs, out_shape=out_shape,
        compiler_params=_cparams(("parallel", "parallel", "parallel")), name="conv_gate",
    )(gre, gim, f1, bsp, z, gate, bias)


def _hyena_convs(v, x1, x2, kf, bias, tabs, layer):
    b0, b1 = bias[0].reshape(HY_CB, 1, LANES), bias[1].reshape(HY_CB, 1, LANES)
    bsp = _conv_mid(_dft1(v, tabs["f1d"]), kf, tabs["m1"], tabs["m2"], layer, 0)
    z1, a2 = _conv_out(bsp, tabs, v, x1, b0, True)
    bsp = _conv_mid(a2, kf, tabs["m1"], tabs["m2"], layer, 1)
    return _conv_out(bsp, tabs, z1, x2, b1, False)[0]


def _attn_kernel(flag_ref, qt_ref, qtn_ref, k_ref, vt_ref, g_ref, o_ref, acc_scr, m_scr, s_scr, q_scr,
                 *, nk, tk, unroll):
    pair = pl.program_id(0) * pl.num_programs(1) + pl.program_id(1)
    fast = jnp.logical_and(flag_ref[2 * pair] == 1, flag_ref[2 * pair + 1] == 1)
    acc_scr[...] = jnp.zeros(acc_scr.shape, F32)

    @pl.when(fast)
    def _():
        q_scr[0] = qt_ref[0]
        q_scr[1] = qt_ref[1]
        q_scr[2] = qtn_ref[...]

        def scores(cc):
            hq = cc // nk
            c = cc - hq * nk
            hk = jnp.where(hq == 2, 0, hq)
            return _dot(k_ref[hk, pl.ds(pl.multiple_of(c * tk, tk), tk), :], q_scr[hq])

        @pl.when(pl.program_id(2) == 0)
        def _():
            s_scr[0] = scores(jnp.int32(0))

        def body(j, carry):
            hd = (j * unroll) // nk
            c0 = j * unroll - hd * nk
            pv = None
            for u in range(unroll):
                s_scr[(u + 1) % 2] = scores(j * unroll + u + 1)
                pt = jnp.exp2(s_scr[u % 2]).astype(BF16)
                d = _dot(vt_ref[hd, c0 + u], pt)
                pv = d if pv is None else pv + d
            acc_scr[hd] += pv
            return carry
        lax.fori_loop(0, 2 * nk // unroll, body, 0)

    @pl.when(jnp.logical_not(fast))
    def _():
        for hh in range(2):
            qt = qt_ref[hh]
            m_scr[...] = jnp.full(m_scr.shape, -jnp.inf, F32)

            def body(j, carry, hh=hh, qt=qt):
                kk = k_ref[hh, pl.ds(pl.multiple_of(j * tk, tk), tk), :]
                s = _dot(kk, qt)
                m_prev = m_scr[...]
                m_new = jnp.maximum(m_prev, jnp.max(s, axis=0, keepdims=True))
                pt = jnp.exp2(s - m_new).astype(BF16)
                acc_scr[hh] = jnp.exp2(m_prev - m_new) * acc_scr[hh] + _dot(vt_ref[hh, j], pt)
                m_scr[...] = m_new
                return carry
            lax.fori_loop(0, nk, body, 0)

    outs = []
    for hh in range(2):
        acc = acc_scr[hh]
        o = acc[:V_HEAD] / acc[V_HEAD:V_HEAD + 1]
        ms = jnp.mean(o * o, axis=0, keepdims=True)
        outs.append(o * lax.rsqrt(ms + EPS))
    ot = jnp.concatenate(outs, axis=0)
    o_ref[...] = (ot.T * g_ref[...]).astype(o_ref.dtype)


def _attention(flags, qt, k, vt, g_attn, nb, L):
    H = MLA_HEADS
    tq = min(512, L)
    nq = L // tq
    nk, tk = vt.shape[2], vt.shape[4]
    grid_spec = pltpu.PrefetchScalarGridSpec(
        num_scalar_prefetch=1, grid=(nb, H // 2, nq),
        in_specs=[pl.BlockSpec((None, 2, HEAD_PAD, tq), lambda b, h, i, f: (b, h, 0, i)),
                  pl.BlockSpec((None, None, HEAD_PAD, tq), lambda b, h, i, f: (b, 2 * h, 0, jnp.minimum(i + 1, nq - 1))),
                  pl.BlockSpec((None, 2, L, HEAD_PAD), lambda b, h, i, f: (b, h, 0, 0)),
                  pl.BlockSpec((None, 2, nk, V_PAD, tk), lambda b, h, i, f: (b, h, 0, 0, 0)),
                  pl.BlockSpec((1, 2 * V_HEAD), lambda b, h, i, f: (0, h))],
        out_specs=pl.BlockSpec((None, tq, 2 * V_HEAD), lambda b, h, i, f: (b, i, h)),
        scratch_shapes=[pltpu.VMEM((2, V_PAD, tq), F32), pltpu.VMEM((1, tq), F32), pltpu.VMEM((2, tk, tq), F32),
                        pltpu.VMEM((3, HEAD_PAD, tq), BF16)])
    unroll = math.gcd(nk, ATTN_UNROLL)
    assert unroll % 2 == 0
    return pl.pallas_call(
        functools.partial(_attn_kernel, nk=nk, tk=tk, unroll=unroll), grid_spec=grid_spec,
        out_shape=jax.ShapeDtypeStruct((nb, L, ATTN_WIDTH), BF16),
        compiler_params=_cparams(("parallel", "parallel", "arbitrary")), name="attention",
    )(flags, qt, qt, k, vt, g_attn)


def _fast_flags(stats):
    H = MLA_HEADS
    qn = jnp.max(stats[:, :, 0:H, :], axis=(1, 3))
    kn = jnp.max(stats[:, :, H:2 * H, :], axis=(1, 3))
    vm = jnp.max(stats[:, :, 2 * H, :], axis=(1, 2))
    ok = jnp.logical_and(qn * kn <= FAST_S_MAX * FAST_S_MAX, (vm <= FAST_V_MAX)[:, None])
    return ok.astype(jnp.int32).reshape(-1)


def _mix_mlp_kernel(x_ref, zh_ref, an_ref, ghy_ref, gsum_ref, wo_ref, gpost_ref, gmpre_ref, wup_ref,
                    wdn_ref, gmpost_ref, out_ref):
    zh = jnp.concatenate(
        [jnp.concatenate([zh_ref[j, g, nl * SUBLANES:(nl + 1) * SUBLANES, :] for j in range(HY_CB)], axis=1)
         for nl in range(zh_ref.shape[2] // SUBLANES) for g in range(DFT_G)], axis=0)
    ms = _dot((zh * zh).astype(BF16), gsum_ref[...]) * (HY_GROUPS / HY_WIDTH)
    hn = (zh * lax.rsqrt(ms + EPS) * ghy_ref[...]).astype(BF16)
    mix = _dot(hn, wo_ref[0:HY_WIDTH, :]) + _dot(an_ref[...], wo_ref[HY_WIDTH:, :])
    x = x_ref[...] + _rms(mix, gpost_ref[...])
    h = _rms(x, gmpre_ref[...]).astype(BF16)
    up = jnp.maximum(_dot(h, wup_ref[...]), 0.0)
    m = _dot((up * up).astype(BF16), wdn_ref[...])
    out_ref[...] = x + _rms(m, gmpost_ref[...])


def _mix_mlp(x, zh, an, lw, tabs, L):
    M, D = x.shape
    tm = min(512, L)
    tps = L // tm
    dff = lw["wup"].shape[1]
    mw = HY_WIDTH + ATTN_WIDTH
    const = lambda *shape: pl.BlockSpec(shape, lambda i: (0,) * len(shape), pipeline_mode=pl.Buffered(1))
    rows = lambda w: pl.BlockSpec((tm, w), lambda i: (i, 0))
    return pl.pallas_call(
        _mix_mlp_kernel, grid=(M // tm,),
        in_specs=[rows(D), pl.BlockSpec((None, HY_CB, DFT_G, tm // DFT_G, LANES),
                                        lambda i: (i // tps, 0, 0, i % tps, 0)),
                  rows(ATTN_WIDTH), const(1, HY_WIDTH), const(HY_WIDTH, HY_WIDTH),
                  const(mw, D), const(1, D), const(1, D), const(D, dff), const(dff, D), const(1, D)],
        out_specs=rows(D), out_shape=jax.ShapeDtypeStruct((M, D), F32),
        compiler_params=_cparams(("parallel",)), name="mix_mlp",
    )(x, zh, an, lw["g_hy"], tabs["gsum"], lw["wo"], lw["g_post"], lw["g_mpre"], lw["wup"], lw["wdn"],
      lw["g_mpost"])


def _tables(L):
    n = 2 * L
    n2 = DFT_N2
    n1 = n // n2
    nh = n1 // 2
    kp = -(-(nh + 1) // SUBLANES) * SUBLANES
    two_pi = 2.0 * math.pi

    k1 = jnp.arange(kp, dtype=jnp.int32)
    valid = (k1 <= nh)
    def outer(ncols):
        nn = jnp.arange(ncols, dtype=jnp.int32)
        ang = ((k1[:, None] * nn[None, :]) % n1).astype(F32) * (two_pi / n1)
        c = jnp.where(valid[:, None], jnp.cos(ang), 0.0)
        s = jnp.where(valid[:, None], -jnp.sin(ang), 0.0)
        return jnp.concatenate([c, s], axis=0).astype(BF16)
    f1d = outer(nh)
    f1f = outer(n1)
    nn = jnp.arange(nh, dtype=jnp.int32)
    ang = ((nn[:, None] * k1[None, :]) % n1).astype(F32) * (two_pi / n1)
    wgt = jnp.where(valid, jnp.where((k1 == 0) | (k1 == nh), 1.0, 2.0), 0.0) / n
    gre = (jnp.cos(ang) * wgt[None, :]).astype(BF16)
    gim = (-jnp.sin(ang) * wgt[None, :]).astype(BF16)
    a2 = jnp.arange(n2, dtype=jnp.int32)
    idx = (a2[None, :, None] * a2[None, None, :] * n1 + a2[None, None, :] * k1[:, None, None]) % n
    ph = idx.astype(F32) * (two_pi / n)
    gr, gi = jnp.cos(ph), -jnp.sin(ph)
    m1 = jnp.concatenate([jnp.concatenate([gr, -gi], axis=2), jnp.concatenate([gi, gr], axis=2)], axis=1)
    m2 = jnp.swapaxes(m1, 1, 2)
    inv = 1.0 / (ROPE_BASE ** (jnp.arange(0, QK_ROPE, 2, dtype=F32) / QK_ROPE))
    ang = jnp.arange(L, dtype=F32)[:, None] * inv[None, :]
    cos, sin = jnp.cos(ang), jnp.sin(ang)
    rcos = jnp.concatenate([cos, cos], axis=1)
    rsin = jnp.concatenate([sin, sin], axis=1)
    kcs =jnp.concatenate([cos, cos, sin, sin, jnp.zeros((L, HEAD_PAD - 2 * QK_ROPE), F32)], axis=1)
    cc = jnp.arange(MLA_HEADS * HEAD_PAD)
    src = jnp.arange(HEAD_PAD)
    pk = ((cc[None, :] % HEAD_PAD) - QK_NOPE == src[:, None]) & (src[:, None] < QK_ROPE)
    grp = jnp.arange(HY_WIDTH) // (HY_WIDTH // HY_GROUPS)
    gsum = (grp[:, None] == grp[None, :]).astype(BF16)
    t = jnp.linspace(0.0, 1.0, L, dtype=F32)[:, None]
    omega = (two_pi / L) * jnp.arange(L, dtype=F32)
    bands = jnp.linspace(1e-4, HY_BANDS - 1, HY_BANDS, dtype=F32)
    phase = omega[:, None] * bands[None, :]
    z = jnp.concatenate([t, jnp.cos(phase), -jnp.sin(phase), jnp.zeros((L, HY_EMB_PAD - HY_EMB), F32)], axis=-1)
    ztab = jnp.concatenate([z, z[:1], z[:0:-1]], axis=0)
    return dict(f1d=f1d, f1f=f1f, gre=gre, gim=gim, m1=m1.astype(BF16), m2=m2.astype(BF16), rcos_t=rcos.T,
                rsin_t=rsin.T, kcs=kcs, pk=pk.astype(BF16), gsum=gsum, ztab=ztab)


def _rot_half_cols(w):
    half = QK_ROPE // 2
    return jnp.concatenate([-w[..., half:], w[..., :half]], axis=-1)


def _layer_weights(i, p, tabs):
    D = p["w_in"].shape[1]
    H = MLA_HEADS
    hw3 = 3 * HY_WIDTH
    w_in = p["w_in"][i]
    kpe = w_in[:, hw3 + Q_RANK + KV_RANK:]
    win = jnp.concatenate([w_in, _rot_half_cols(kpe),
                           jnp.zeros((D, HEAD_PAD - 2 * QK_ROPE), F32)], axis=1).astype(BF16)
    dq = QK_NOPE + QK_ROPE
    wq = p["mla_w_uq"][i].reshape(Q_RANK, H, dq)
    wqt = jnp.concatenate([wq.reshape(Q_RANK, H * dq),
                           _rot_half_cols(wq[..., QK_NOPE:]).reshape(Q_RANK, H * QK_ROPE)], axis=1).T
    wkv =p["mla_w_ukv"][i].reshape(KV_RANK, H, QK_NOPE + V_HEAD)
    wk = jnp.concatenate([wkv[..., :QK_NOPE], jnp.zeros((KV_RANK, H, HEAD_PAD - QK_NOPE), F32)], axis=2)
    wv = jnp.concatenate([wkv[..., QK_NOPE:], jnp.zeros((KV_RANK, H, V_PAD - V_HEAD), F32)], axis=2)
    row = lambda a: a.reshape(1, -1)
    return dict(
        win=win, g_pre=row(p["norm_mix_pre"][i]), conv_w=p["hy_conv_w"][i], conv_b=row(p["hy_conv_b"][i]),
        q_g=row(p["mla_q_norm"][i]), wqt=wqt.astype(BF16), kv_g=row(p["mla_kv_norm"][i]),
        wkp=jnp.concatenate([wk.reshape(KV_RANK, H * HEAD_PAD).astype(BF16), tabs["pk"]], axis=0),
        wvt=wv.reshape(KV_RANK, H * V_PAD).T.astype(BF16),
        g_hy=row(p["grp_norm_hy"][i]), g_attn=row(p["grp_norm_attn"][i]), wo=p["w_out"][i].astype(BF16),
        g_post=row(p["norm_mix_post"][i]), g_mpre=row(p["norm_mlp_pre"][i]), g_mpost=row(p["norm_mlp_post"][i]),
        wup=p["w_mlp_up"][i].astype(BF16), wdn=p["w_mlp_down"][i].astype(BF16),
    )


def _filter_weights(p):
    depth = p["hy_ffn_w1"].shape[0]
    oc = HY_ORDER * HY_WIDTH
    w1 = jnp.pad(p["hy_ffn_w1"], ((0, 0), (0, HY_EMB_PAD - HY_EMB), (0, 0)))
    w3 = p["hy_ffn_w3"].reshape(depth, HY_FFN, HY_ORDER, 2, HY_WIDTH).transpose(0, 3, 1, 2, 4)
    dec = p["hy_decay"].transpose(0, 2, 1, 3).reshape(depth, 2, 1, oc)
    tr = lambda a: jnp.swapaxes(a, 1, 2)
    w3 = w3.reshape(depth, 2, HY_FFN, oc)
    w3_hi = w3.astype(BF16)
    w3_lo = (w3 - w3_hi.astype(F32)).astype(BF16)
    return dict(w1=tr(w1), b1=p["hy_ffn_b1"][:, :, None], sf=tr(p["hy_sin_freq"]), w2=tr(p["hy_ffn_w2"]),
                b2=p["hy_ffn_b2"][:, :, None], w3=jnp.stack([w3_hi, w3_lo], axis=2), dec=dec)


def kernel(x_prompt, x_sample, w_in, hy_conv_w, hy_conv_b, hy_ffn_w1, hy_ffn_b1, hy_ffn_w2, hy_ffn_b2,
           hy_ffn_w3, hy_sin_freq, hy_decay, hy_bias, mla_q_norm, mla_w_uq, mla_kv_norm, mla_w_ukv,
           grp_norm_hy, grp_norm_attn, w_out, norm_mix_pre, norm_mix_post, norm_mlp_pre, norm_mlp_post,
           w_mlp_up, w_mlp_down):
    p = dict(w_in=w_in, hy_conv_w=hy_conv_w, hy_conv_b=hy_conv_b, hy_ffn_w1=hy_ffn_w1, hy_ffn_b1=hy_ffn_b1,
             hy_ffn_w2=hy_ffn_w2, hy_ffn_b2=hy_ffn_b2, hy_ffn_w3=hy_ffn_w3, hy_sin_freq=hy_sin_freq,
             hy_decay=hy_decay, hy_bias=hy_bias, mla_q_norm=mla_q_norm, mla_w_uq=mla_w_uq,
             mla_kv_norm=mla_kv_norm, mla_w_ukv=mla_w_ukv, grp_norm_hy=grp_norm_hy,
             grp_norm_attn=grp_norm_attn, w_out=w_out, norm_mix_pre=norm_mix_pre, norm_mix_post=norm_mix_post,
             norm_mlp_pre=norm_mlp_pre, norm_mlp_post=norm_mlp_post, w_mlp_up=w_mlp_up, w_mlp_down=w_mlp_down)
    bp, L, D = x_prompt.shape
    bs, Ls, _ = x_sample.shape
    assert L == Ls and L % (DFT_N2 * SUBLANES) == 0
    nb = bp + bs
    depth = w_in.shape[0]

    tabs = _tables(L)
    kc = _filters(tabs["ztab"], _filter_weights(p), L)
    kf = _filter_spectrum(_dft1_pairs(kc, tabs["f1f"]), tabs["m1"])

    x = jnp.concatenate([x_prompt.reshape(bp * L, D), x_sample.reshape(bs * L, D)], axis=0)
    for i in range(depth):
        lw = _layer_weights(i, p, tabs)
        v, x1, x2, qt, k, vt, stats = _inproj(x, lw, tabs, nb, L)
        z = _hyena_convs(v, x1, x2, kf, hy_bias[i], tabs, i)
        an = _attention(_fast_flags(stats), qt, k, vt, lw["g_attn"], nb, L).reshape(nb * L, ATTN_WIDTH)
        x = _mix_mlp(x, z, an, lw, tabs, L)
    return (x[:bp * L].reshape(bp, L, D), x[bp * L:].reshape(bs, L, D))
```

```python
import functools
import math

import jax
import jax.numpy as jnp
from jax import lax
from jax.experimental import pallas as pl
from jax.experimental.pallas import tpu as pltpu

F32 = jnp.float32
BF16 = jnp.bfloat16

EPS = 1e-6
HY_WIDTH = 512
HY_GROUPS = 8
HY_ORDER = 2
HY_BANDS = 16
HY_EMB = 2 * HY_BANDS + 1
HY_EMB_PAD = 40
HY_FFN = 64
MLA_HEADS = 8
QK_NOPE = 64
QK_ROPE = 32
V_HEAD = 64
Q_RANK = 256
KV_RANK = 128
ROPE_BASE = 10000.0
HEAD_PAD = 128
V_PAD = 128
ATTN_WIDTH = MLA_HEADS * V_HEAD

LANES = 128
SUBLANES = 8
HY_CB = HY_WIDTH // LANES
DFT_N2 = 128
DFT_G = DFT_N2 // SUBLANES
MID_CB = 2
MID_RING = 3
OUTER_GB = 8
HALO = 16
VMEM_LIMIT = 56 * 1024 * 1024
ATTN_UNROLL = 32
STAT_ROWS = 24

FAST_S_MAX = 64.0
FAST_V_MAX = 2.0 ** 30


def _cparams(sem):
    return pltpu.CompilerParams(dimension_semantics=sem, vmem_limit_bytes=VMEM_LIMIT)


def _dot(a, b):
    return jnp.dot(a, b, preferred_element_type=F32)


def _dot_nt(a, b):
    return lax.dot_general(a, b, (((1,), (1,)), ((), ())), preferred_element_type=F32)


def _rms(x, g):
    return x * lax.rsqrt(jnp.mean(x * x, axis=-1, keepdims=True) + EPS) * g


def _store_grouped(out_ref, val, lane0, ncb):
    for j in range(ncb):
        for nl in range(val.shape[0] // DFT_N2):
            for g in range(DFT_G):
                r0 = nl * DFT_N2 + g * SUBLANES
                out_ref[j, g, nl * SUBLANES:(nl + 1) * SUBLANES, :] = (
                    val[r0:r0 + SUBLANES, lane0 + j * LANES:lane0 + (j + 1) * LANES])


def _fold_lanes(row):
    parts = [row[:, j * LANES:(j + 1) * LANES] for j in range(row.shape[1] // LANES)]
    return functools.reduce(jnp.maximum, parts)


def _inproj_kernel(xp_ref, x_ref, xn_ref, gpre_ref, win_ref, cw_ref, cb_ref, qg_ref, wqt_ref,
                   kvg_ref, wkp_ref, wvt_ref, qcos_ref, qsin_ref, kcs_ref,
                   v_out, x1_out, x2_out, qt_out, k_out, vt_out, st_out, h_scr, pe_scr,
                   *, tm, tiles_per_seq, scale):
    i = pl.program_id(0)
    t_idx = i % tiles_per_seq
    g = gpre_ref[...]
    h_scr[0:HALO, :] = _rms(xp_ref[...], g).astype(BF16)
    h_scr[HALO:HALO + tm, :] = _rms(x_ref[...], g).astype(BF16)
    h_scr[HALO + tm:2 * HALO + tm, :] = _rms(xn_ref[...], g).astype(BF16)
    hw3 = 3 * HY_WIDTH
    core = _dot(h_scr[HALO:HALO + tm, :], win_ref[:, hw3:])
    pe_scr[...] = _dot(h_scr[...], win_ref[:, 0:hw3])
    row = lax.broadcasted_iota(jnp.int32, (tm, 1), 0)
    prev = pe_scr[HALO - 1:HALO - 1 + tm, :]
    cur = pe_scr[HALO:HALO + tm, :]
    nxt = pe_scr[HALO + 1:HALO + 1 + tm, :]
    prev = jnp.where(jnp.logical_and(row == 0, t_idx == 0), 0.0, prev)
    nxt = jnp.where(jnp.logical_and(row == tm - 1, t_idx == tiles_per_seq - 1), 0.0, nxt)
    u = prev * cw_ref[0:1, :] + cur * cw_ref[1:2, :] + nxt * cw_ref[2:3, :] + cb_ref[...]
    for o, out in enumerate((v_out, x1_out, x2_out)):
        _store_grouped(out, u, o * HY_WIDTH, HY_CB)

    cq = core[:, 0:Q_RANK]
    cqn =_rms(cq, qg_ref[...]).astype(BF16)
    dq = QK_NOPE + QK_ROPE
    qall = _dot_nt(wqt_ref[...], cqn)
    rcos = qcos_ref[...]
    rsin = qsin_ref[...]
    zpad = jnp.zeros((HEAD_PAD - dq, tm), F32)
    for h in range(MLA_HEADS):
        rot = qall[MLA_HEADS * dq + h * QK_ROPE:MLA_HEADS * dq + (h + 1) * QK_ROPE]
        rope = qall[h * dq + QK_NOPE:(h + 1) * dq] * rcos + rot * rsin
        qh = (jnp.concatenate([qall[h * dq:h * dq + QK_NOPE], rope, zpad], axis=0) * scale).astype(BF16)
        qt_out[h] = qh
        qf = qh.astype(F32)
        st_out[h:h + 1, :] = _fold_lanes(jnp.sum(qf * qf, axis=0, keepdims=True))

    ckv = core[:, Q_RANK:Q_RANK + KV_RANK]
    ckvn = _rms(ckv, kvg_ref[...]).astype(BF16)
    kp = core[:, Q_RANK + KV_RANK:] * kcs_ref[...]
    kp = kp + pltpu.roll(kp, HEAD_PAD - QK_ROPE, axis=1)
    kk = _dot(jnp.concatenate([ckvn, kp.astype(BF16)], axis=1), wkp_ref[...])
    for h in range(MLA_HEADS):
        kh = kk[:, h * HEAD_PAD:(h + 1) * HEAD_PAD].astype(BF16)
        k_out[h] = kh
        kf = kh.astype(F32)
        kn = jnp.max(jnp.sum(kf * kf, axis=1, keepdims=True), axis=0, keepdims=True)
        st_out[MLA_HEADS + h:MLA_HEADS + h + 1, :] = jnp.broadcast_to(kn, (1, LANES))
    vt = _dot_nt(wvt_ref[...], ckvn)
    st_out[2 * MLA_HEADS:2 * MLA_HEADS + 1, :] = _fold_lanes(jnp.max(jnp.abs(vt), axis=0, keepdims=True))
    st_out[2 * MLA_HEADS + 1:, :] = jnp.zeros((STAT_ROWS - 2 * MLA_HEADS - 1, LANES), F32)
    frow = lax.broadcasted_iota(jnp.int32, (MLA_HEADS * V_PAD, 1), 0)
    vt = vt + jnp.where(frow % V_PAD == V_HEAD, 1.0, 0.0)
    for h in range(MLA_HEADS):
        vt_out[h] = vt[h * V_PAD:(h + 1) * V_PAD, :].astype(BF16)


def _inproj(x, lw, tabs, nb, L):
    M, D = x.shape
    tm = min(512, L)
    tps = L // tm
    nt = M // tm
    hb = tm // HALO
    nhb = M // HALO
    H = MLA_HEADS
    wcols = lw["win"].shape[1]
    const = lambda *shape: pl.BlockSpec(shape, lambda i: (0,) * len(shape))
    tab_t = pl.BlockSpec((QK_ROPE, tm), lambda i: (0, i % tps))
    in_specs = [
        pl.BlockSpec((HALO, D), lambda i: (jnp.maximum(i * hb - 1, 0), 0)),
        pl.BlockSpec((tm, D), lambda i: (i, 0)),
        pl.BlockSpec((HALO, D), lambda i: (jnp.minimum((i + 1) * hb, nhb - 1), 0)),
        const(1, D), const(D, wcols), const(3, 3 * HY_WIDTH), const(1, 3 * HY_WIDTH),
        const(1, Q_RANK), const(H * HEAD_PAD, Q_RANK),
        const(1, KV_RANK), const(KV_RANK + HEAD_PAD, H * HEAD_PAD),
        const(H * V_PAD, KV_RANK),
        tab_t, tab_t, pl.BlockSpec((tm, HEAD_PAD), lambda i: (i % tps, 0)),
    ]
    hy_spec = pl.BlockSpec((None, HY_CB, DFT_G, tm // DFT_G, LANES), lambda i: (i // tps, 0, 0, i % tps, 0))
    out_specs = [
        hy_spec, hy_spec, hy_spec,
        pl.BlockSpec((None, H, HEAD_PAD, tm), lambda i: (i // tps, 0, 0, i % tps)),
        pl.BlockSpec((None, H, tm, HEAD_PAD), lambda i: (i // tps, 0, i % tps, 0)),
        pl.BlockSpec((None, H, None, V_PAD, tm), lambda i: (i // tps, 0, i % tps, 0, 0)),
        pl.BlockSpec((None, None, STAT_ROWS, LANES), lambda i: (i // tps, i % tps, 0, 0)),
    ]
    hy_shape = jax.ShapeDtypeStruct((nb, HY_CB, DFT_G, L // DFT_G, LANES), F32)
    out_shape = [
        hy_shape, hy_shape, hy_shape,
        jax.ShapeDtypeStruct((nb, H, HEAD_PAD, L), BF16),
        jax.ShapeDtypeStruct((nb, H, L, HEAD_PAD), BF16),
        jax.ShapeDtypeStruct((nb, H, tps, V_PAD, tm), BF16),
        jax.ShapeDtypeStruct((nb, tps, STAT_ROWS, LANES), F32),
    ]
    scale = float((QK_NOPE + QK_ROPE) ** -0.5 * math.log2(math.e))
    return pl.pallas_call(
        functools.partial(_inproj_kernel, tm=tm, tiles_per_seq=tps, scale=scale),
        grid=(nt,), in_specs=in_specs, out_specs=out_specs, out_shape=out_shape,
        scratch_shapes=[pltpu.VMEM((tm + 2 * HALO, D), BF16),
                        pltpu.VMEM((tm + 2 * HALO, 3 * HY_WIDTH), F32)],
        compiler_params=_cparams(("parallel",)), name="inproj",
    )(x, x, x, lw["g_pre"], lw["win"], lw["conv_w"], lw["conv_b"], lw["q_g"], lw["wqt"],
      lw["kv_g"], lw["wkp"], lw["wvt"], tabs["rcos_t"], tabs["rsin_t"], tabs["kcs"])


def _filter_kernel(t_ref, zt_ref, w1_ref, b1_ref, sf_ref, w2_ref, b2_ref, w3_ref, dec_ref, out_ref, *, rb, L):
    hi = lax.Precision.HIGHEST
    r = pl.program_id(1)
    h = jnp.sin(sf_ref[:, 0:1] * (jnp.dot(w1_ref[...], zt_ref[...], precision=hi, preferred_element_type=F32)
                                  + b1_ref[...]))
    h = jnp.sin(sf_ref[:, 1:2] * (jnp.dot(w2_ref[...], h, precision=hi, preferred_element_type=F32) + b2_ref[...]))
    tn = lambda a, b: lax.dot_general(a, b, (((0,), (0,)), ((), ())), preferred_element_type=F32)
    h_hi = h.astype(BF16)
    h_lo = (h - h_hi.astype(F32)).astype(BF16)
    k = tn(h_hi, w3_ref[0]) + (tn(h_hi, w3_ref[1]) + tn(h_lo, w3_ref[0]))
    k = k * jnp.exp(-t_ref[...] * jnp.abs(dec_ref[...]))
    row = r * rb + lax.broadcasted_iota(jnp.int32, (rb, 1), 0)
    k = jnp.where(row == L, 0.0, k)
    npair = out_ref.shape[0]
    w = jnp.concatenate([_pack_pair(k[:, 2 * j * LANES:(2 * j + 1) * LANES], k[:, (2 * j + 1) * LANES:(2 * j + 2) * LANES])
                         for j in range(npair)], axis=1)
    _store_grouped(out_ref, w, 0, npair)


def _filters(ztab, fw, L):
    depth = fw["w1"].shape[0]
    rows = 2 * L
    rb = min(512, L)
    nblk = rows // rb
    half = nblk // 2
    oc = HY_ORDER * HY_WIDTH
    lay = lambda *shape: pl.BlockSpec((None,) + shape, lambda l, r: (l,) + (0,) * len(shape))
    in_specs = [
        pl.BlockSpec((rb, 1), lambda l, r: (r, 0)), pl.BlockSpec((HY_EMB_PAD, rb), lambda l, r: (0, r)),
        lay(HY_FFN, HY_EMB_PAD), lay(HY_FFN, 1), lay(HY_FFN, 2), lay(HY_FFN, HY_FFN), lay(HY_FFN, 1),
        pl.BlockSpec((None, None, 2, HY_FFN, oc), lambda l, r: (l, r // half, 0, 0, 0)),
        pl.BlockSpec((None, None, 1, oc), lambda l, r: (l, r // half, 0, 0)),
    ]
    return pl.pallas_call(
        functools.partial(_filter_kernel, rb=rb, L=L),
        grid=(depth, nblk), in_specs=in_specs,
        out_specs=pl.BlockSpec((None, oc // LANES // 2, DFT_G, rb // DFT_G, LANES), lambda l, r: (l, 0, 0, r, 0)),
        out_shape=jax.ShapeDtypeStruct((depth, oc // LANES // 2, DFT_G, rows // DFT_G, LANES), jnp.uint32),
        compiler_params=_cparams(("parallel", "parallel")), name="hyena_filter",
    )(ztab[:, 0:1], ztab.T, fw["w1"], fw["b1"], fw["sf"], fw["w2"], fw["b2"], fw["w3"], fw["dec"])


def _rows_of(ref, p, n, lead=()):
    return ref[lead + (pl.ds(p, n, stride=SUBLANES), slice(None))]


def _pack_pair(re, im):
    hi = lax.bitcast_convert_type(re.astype(BF16).astype(F32), jnp.uint32)
    lo = lax.bitcast_convert_type(im.astype(BF16).astype(F32), jnp.uint32)
    return hi | (lo >> 16)


def _unpack_pair(w, dtype=BF16):
    re = lax.bitcast_convert_type(w & jnp.uint32(0xFFFF0000), F32)
    im = lax.bitcast_convert_type(w << 16, F32)
    return re.astype(dtype), im.astype(dtype)


def _dft1_block(f_ref, xs, out_ref, gg):
    kp = f_ref.shape[0] // 2
    res = _dot(f_ref[...], jnp.concatenate([x.astype(BF16) for x in xs], axis=1))
    w = _pack_pair(res[:kp], res[kp:])
    for kg in range(kp // SUBLANES):
        for p in range(SUBLANES):
            r0 = (gg * SUBLANES + p) * SUBLANES
            out_ref[kg, r0:r0 + SUBLANES, :] = w[kg * SUBLANES:(kg + 1) * SUBLANES, p * LANES:(p + 1) * LANES]


def _dft1_kernel(f_ref, x_ref, out_ref):
    R = f_ref.shape[1]
    for gg in range(OUTER_GB):
        _dft1_block(f_ref, [_rows_of(x_ref, p, R, (gg,)) for p in range(SUBLANES)], out_ref, gg)


def _dft1_spec(kg):
    return pl.BlockSpec((None, None, kg, OUTER_GB * SUBLANES * SUBLANES, LANES), lambda b, c, g: (b, c, 0, g, 0))


def _dft1(x, f1):
    nb, ncb, _, rows, _ = x.shape
    kp = f1.shape[0] // 2
    kg = kp // SUBLANES
    return pl.pallas_call(
        _dft1_kernel, grid=(nb, ncb, DFT_G // OUTER_GB),
        in_specs=[pl.BlockSpec(f1.shape, lambda b, c, g: (0, 0)),
                  pl.BlockSpec((None, None, OUTER_GB, rows, LANES), lambda b, c, g: (b, c, g, 0, 0))],
        out_specs=_dft1_spec(kg),
        out_shape=jax.ShapeDtypeStruct((nb, ncb, kg, DFT_N2 * SUBLANES, LANES), jnp.uint32),
        compiler_params=_cparams(("parallel", "parallel", "parallel")), name="dft_outer",
    )(f1, x)


def _dft1_pairs_kernel(f_ref, x_ref, out_ref):
    R = f_ref.shape[1]
    for gg in range(OUTER_GB):
        pairs = [_unpack_pair(_rows_of(x_ref, p, R, (gg,))) for p in range(SUBLANES)]
        for c in range(2):
            _dft1_block(f_ref, [pr[c] for pr in pairs], out_ref.at[c], gg)


def _dft1_pairs(x, f1):
    nb, npair, _, rows, _ = x.shape
    kg = f1.shape[0] // 2 // SUBLANES
    return pl.pallas_call(
        _dft1_pairs_kernel, grid=(nb, npair, DFT_G // OUTER_GB),
        in_specs=[pl.BlockSpec(f1.shape, lambda b, c, g: (0, 0)),
                  pl.BlockSpec((None, None, OUTER_GB, rows, LANES), lambda b, c, g: (b, c, g, 0, 0))],
        out_specs=pl.BlockSpec((None, 2, kg, OUTER_GB * SUBLANES * SUBLANES, LANES), lambda b, c, g: (b, c, 0, g, 0)),
        out_shape=jax.ShapeDtypeStruct((nb, 2 * npair, kg, DFT_N2 * SUBLANES, LANES), jnp.uint32),
        compiler_params=_cparams(("parallel", "parallel", "parallel")), name="dft_outer_filter",
    )(f1, x)


def _inner_fwd(m1_ref, a_ref, kk):
    w = jnp.concatenate([_rows_of(a_ref, kk, DFT_N2, (c,)) for c in range(a_ref.shape[0])], axis=1)
    ar, ai = _unpack_pair(w)
    return _dot(m1_ref[kk], jnp.concatenate([ar, ai], axis=0))


def _spec_kernel(m1_ref, a_ref, out_ref):
    for kk in range(SUBLANES):
        x = _inner_fwd(m1_ref, a_ref, kk)
        for c in range(a_ref.shape[0]):
            out_ref[c, kk] = _pack_pair(x[:DFT_N2, c * LANES:(c + 1) * LANES], x[DFT_N2:, c * LANES:(c + 1) * LANES])


def _filter_spectrum(a, m1):
    depth, ncb, kg, rows, _ = a.shape
    n2 = DFT_N2
    return pl.pallas_call(
        _spec_kernel, grid=(kg, depth, ncb // MID_CB),
        in_specs=[pl.BlockSpec((SUBLANES, 2 * n2, 2 * n2), lambda k, l, c: (k, 0, 0)),
                  pl.BlockSpec((None, MID_CB, None, rows, LANES), lambda k, l, c: (l, c, k, 0, 0))],
        out_specs=pl.BlockSpec((None, MID_CB, SUBLANES, n2, LANES), lambda k, l, c: (l, c, k, 0, 0)),
        out_shape=jax.ShapeDtypeStruct((depth, ncb, kg * SUBLANES, n2, LANES), jnp.uint32),
        compiler_params=_cparams(("parallel", "parallel", "parallel")), name="filter_spectrum",
    )(m1, a)


def _mid_kernel(m1_ref, m2_ref, kf_ref, a_hbm, out_ref, ring, sems):
    nc, nb = pl.num_programs(1), pl.num_programs(2)
    steps = pl.num_programs(0) * nc * nb
    s = (pl.program_id(0) * nc + pl.program_id(1)) * nb + pl.program_id(2)

    def copy(t):
        k, c, b = t // (nc * nb), (t // nb) % nc, t % nb
        slot = t % MID_RING
        return pltpu.make_async_copy(a_hbm.at[b, pl.ds(c * MID_CB, MID_CB), k], ring.at[slot], sems.at[slot])

    @pl.when(s == 0)
    def _():
        for t in range(MID_RING - 1):
            @pl.when(t < steps)
            def _(t=t):
                copy(jnp.int32(t)).start()

    @pl.when(s + MID_RING - 1 < steps)
    def _():
        copy(s + MID_RING - 1).start()

    copy(s).wait()
    a_ref = ring.at[s % MID_RING]
    ncb = a_ref.shape[0]
    lanes = lambda parts: jnp.concatenate(parts, axis=1)
    for kk in range(SUBLANES):
        x = _inner_fwd(m1_ref, a_ref, kk)
        xr, xi = x[:DFT_N2], x[DFT_N2:]
        kr, ki = _unpack_pair(lanes([kf_ref[c, kk] for c in range(ncb)]), F32)
        yr = (xr * kr - xi * ki).astype(BF16)
        yi = (xr * ki + xi * kr).astype(BF16)
        y = _dot(m2_ref[kk], jnp.concatenate([yr, yi], axis=0))
        w = _pack_pair(y[:DFT_N2], y[DFT_N2:])
        for c in range(ncb):
            for g in range(DFT_G):
                out_ref[c, g, kk * SUBLANES:(kk + 1) * SUBLANES, :] = (
                    w[g * SUBLANES:(g + 1) * SUBLANES, c * LANES:(c + 1) * LANES])


def _conv_mid(a, kf, m1, m2, layer, order):
    nb, ncb, kg, rows, _ = a.shape
    n2 = DFT_N2
    mspec = pl.BlockSpec((SUBLANES, 2 * n2, 2 * n2), lambda k, c, b: (k, 0, 0))
    return pl.pallas_call(
        _mid_kernel, grid=(kg, ncb // MID_CB, nb),
        in_specs=[mspec, mspec,
                  pl.BlockSpec((None, MID_CB, SUBLANES, n2, LANES),
                               lambda k, c, b: (layer, order * (ncb // MID_CB) + c, k, 0, 0)),
                  pl.BlockSpec(memory_space=pl.ANY)],
        out_specs=pl.BlockSpec((None, MID_CB, DFT_G, SUBLANES * SUBLANES, LANES), lambda k, c, b: (b, c, 0, k, 0)),
        out_shape=jax.ShapeDtypeStruct((nb, ncb, DFT_G, kg * SUBLANES * SUBLANES, LANES), jnp.uint32),
        scratch_shapes=[pltpu.VMEM((MID_RING, MID_CB, rows, LANES), jnp.uint32), pltpu.SemaphoreType.DMA((MID_RING,))],
        compiler_params=_cparams(("arbitrary", "arbitrary", "arbitrary")), name="conv_mid",
    )(m1, m2, kf, a)


def _gate_kernel(gre_ref, gim_ref, f_ref, b_ref, z_ref, gate_ref, bias_ref, out_ref, *next_ref):
    R, kp = gre_ref.shape
    bias = bias_ref[...]
    for gg in range(OUTER_GB):
        bre, bim = _unpack_pair(jnp.concatenate([_rows_of(b_ref, p, kp, (gg,)) for p in range(SUBLANES)], axis=1))
        y = _dot(gre_ref[...], bre) + _dot(gim_ref[...], bim)
        vals = []
        for p in range(SUBLANES):
            yp = y[:, p * LANES:(p + 1) * LANES]
            vals.append(_rows_of(gate_ref, p, R, (gg,)) * (yp + _rows_of(z_ref, p, R, (gg,)) * bias))
            out_ref[gg, pl.ds(p, R, stride=SUBLANES), :] = vals[p]
        if next_ref:
            _dft1_block(f_ref, vals, next_ref[0], gg)


def _conv_out(bsp, tabs, z, gate, bias, with_next):
    nb, ncb, _, krows, _ = bsp.shape
    rows = z.shape[3]
    gre, gim, f1 = tabs["gre"], tabs["gim"], tabs["f1d"]
    kg = f1.shape[0] // 2 // SUBLANES
    full = lambda t: pl.BlockSpec(t.shape, lambda b, c, g: (0, 0))
    tile = pl.BlockSpec((None, None, OUTER_GB, rows, LANES), lambda b, c, g: (b, c, g, 0, 0))
    out_specs, out_shape = [tile], [jax.ShapeDtypeStruct(z.shape, F32)]
    if with_next:
        out_specs.append(_dft1_spec(kg))
        out_shape.append(jax.ShapeDtypeStruct((nb, ncb, kg, DFT_N2 * SUBLANES, LANES), jnp.uint32))
    return pl.pallas_call(
        _gate_kernel, grid=(nb, ncb, DFT_G // OUTER_GB),
        in_specs=[full(gre), full(gim), full(f1),
                  pl.BlockSpec((None, None, OUTER_GB, krows, LANES), lambda b, c, g: (b, c, g, 0, 0)),
                  tile, tile, pl.BlockSpec((None, 1, LANES), lambda b, c, g: (c, 0, 0))],
        out_specs=out_specs, out_shape=out_shape,
        compiler_params=_cparams(("parallel", "parallel", "parallel")), name="conv_gate",
    )(gre, gim, f1, bsp, z, gate, bias)


def _hyena_convs(v, x1, x2, kf, bias, tabs, layer):
    b0, b1 = bias[0].reshape(HY_CB, 1, LANES), bias[1].reshape(HY_CB, 1, LANES)
    bsp = _conv_mid(_dft1(v, tabs["f1d"]), kf, tabs["m1"], tabs["m2"], layer, 0)
    z1, a2 = _conv_out(bsp, tabs, v, x1, b0, True)
    bsp = _conv_mid(a2, kf, tabs["m1"], tabs["m2"], layer, 1)
    return _conv_out(bsp, tabs, z1, x2, b1, False)[0]


def _attn_kernel(flag_ref, qt_ref, qtn_ref, k_ref, vt_ref, g_ref, o_ref, acc_scr, m_scr, s_scr, q_scr,
                 *, nk, tk, unroll):
    pair = pl.program_id(0) * pl.num_programs(1) + pl.program_id(1)
    fast = jnp.logical_and(flag_ref[2 * pair] == 1, flag_ref[2 * pair + 1] == 1)
    acc_scr[...] = jnp.zeros(acc_scr.shape, F32)

    @pl.when(fast)
    def _():
        q_scr[0] = qt_ref[0]
        q_scr[1] = qt_ref[1]
        q_scr[2] = qtn_ref[...]

        def scores(cc):
            hq = cc // nk
            c = cc - hq * nk
            hk = jnp.where(hq == 2, 0, hq)
            return _dot(k_ref[hk, pl.ds(pl.multiple_of(c * tk, tk), tk), :], q_scr[hq])

        @pl.when(pl.program_id(2) == 0)
        def _():
            s_scr[0] = scores(jnp.int32(0))

        def body(j, carry):
            hd = (j * unroll) // nk
            c0 = j * unroll - hd * nk
            pv = None
            for u in range(unroll):
                s_scr[(u + 1) % 2] = scores(j * unroll + u + 1)
                pt = jnp.exp2(s_scr[u % 2]).astype(BF16)
                d = _dot(vt_ref[hd, c0 + u], pt)
                pv = d if pv is None else pv + d
            acc_scr[hd] += pv
            return carry
        lax.fori_loop(0, 2 * nk // unroll, body, 0)

    @pl.when(jnp.logical_not(fast))
    def _():
        for hh in range(2):
            qt = qt_ref[hh]
            m_scr[...] = jnp.full(m_scr.shape, -jnp.inf, F32)

            def body(j, carry, hh=hh, qt=qt):
                kk = k_ref[hh, pl.ds(pl.multiple_of(j * tk, tk), tk), :]
                s = _dot(kk, qt)
                m_prev = m_scr[...]
                m_new = jnp.maximum(m_prev, jnp.max(s, axis=0, keepdims=True))
                pt = jnp.exp2(s - m_new).astype(BF16)
                acc_scr[hh] = jnp.exp2(m_prev - m_new) * acc_scr[hh] + _dot(vt_ref[hh, j], pt)
                m_scr[...] = m_new
                return carry
            lax.fori_loop(0, nk, body, 0)

    outs = []
    for hh in range(2):
        acc = acc_scr[hh]
        o = acc[:V_HEAD] / acc[V_HEAD:V_HEAD + 1]
        ms = jnp.mean(o * o, axis=0, keepdims=True)
        outs.append(o * lax.rsqrt(ms + EPS))
    ot = jnp.concatenate(outs, axis=0)
    o_ref[...] = (ot.T * g_ref[...]).astype(o_ref.dtype)


def _attention(flags, qt, k, vt, g_attn, nb, L):
    H = MLA_HEADS
    tq = min(512, L)
    nq = L // tq
    nk, tk = vt.shape[2], vt.shape[4]
    grid_spec = pltpu.PrefetchScalarGridSpec(
        num_scalar_prefetch=1, grid=(nb, H // 2, nq),
        in_specs=[pl.BlockSpec((None, 2, HEAD_PAD, tq), lambda b, h, i, f: (b, h, 0, i)),
                  pl.BlockSpec((None, None, HEAD_PAD, tq), lambda b, h, i, f: (b, 2 * h, 0, jnp.minimum(i + 1, nq - 1))),
                  pl.BlockSpec((None, 2, L, HEAD_PAD), lambda b, h, i, f: (b, h, 0, 0)),
                  pl.BlockSpec((None, 2, nk, V_PAD, tk), lambda b, h, i, f: (b, h, 0, 0, 0)),
                  pl.BlockSpec((1, 2 * V_HEAD), lambda b, h, i, f: (0, h))],
        out_specs=pl.BlockSpec((None, tq, 2 * V_HEAD), lambda b, h, i, f: (b, i, h)),
        scratch_shapes=[pltpu.VMEM((2, V_PAD, tq), F32), pltpu.VMEM((1, tq), F32), pltpu.VMEM((2, tk, tq), F32),
                        pltpu.VMEM((3, HEAD_PAD, tq), BF16)])
    unroll = math.gcd(nk, ATTN_UNROLL)
    assert unroll % 2 == 0
    return pl.pallas_call(
        functools.partial(_attn_kernel, nk=nk, tk=tk, unroll=unroll), grid_spec=grid_spec,
        out_shape=jax.ShapeDtypeStruct((nb, L, ATTN_WIDTH), BF16),
        compiler_params=_cparams(("parallel", "parallel", "arbitrary")), name="attention",
    )(flags, qt, qt, k, vt, g_attn)


def _fast_flags(stats):
    H = MLA_HEADS
    qn = jnp.max(stats[:, :, 0:H, :], axis=(1, 3))
    kn = jnp.max(stats[:, :, H:2 * H, :], axis=(1, 3))
    vm = jnp.max(stats[:, :, 2 * H, :], axis=(1, 2))
    ok = jnp.logical_and(qn * kn <= FAST_S_MAX * FAST_S_MAX, (vm <= FAST_V_MAX)[:, None])
    return ok.astype(jnp.int32).reshape(-1)


def _mix_mlp_kernel(x_ref, zh_ref, an_ref, ghy_ref, gsum_ref, wo_ref, gpost_ref, gmpre_ref, wup_ref,
                    wdn_ref, gmpost_ref, out_ref):
    zh = jnp.concatenate(
        [jnp.concatenate([zh_ref[j, g, nl * SUBLANES:(nl + 1) * SUBLANES, :] for j in range(HY_CB)], axis=1)
         for nl in range(zh_ref.shape[2] // SUBLANES) for g in range(DFT_G)], axis=0)
    ms = _dot((zh * zh).astype(BF16), gsum_ref[...]) * (HY_GROUPS / HY_WIDTH)
    hn = (zh * lax.rsqrt(ms + EPS) * ghy_ref[...]).astype(BF16)
    mix = _dot(hn, wo_ref[0:HY_WIDTH, :]) + _dot(an_ref[...], wo_ref[HY_WIDTH:, :])
    x = x_ref[...] + _rms(mix, gpost_ref[...])
    h = _rms(x, gmpre_ref[...]).astype(BF16)
    up = jnp.maximum(_dot(h, wup_ref[...]), 0.0)
    m = _dot((up * up).astype(BF16), wdn_ref[...])
    out_ref[...] = x + _rms(m, gmpost_ref[...])


def _mix_mlp(x, zh, an, lw, tabs, L):
    M, D = x.shape
    tm = min(512, L)
    tps = L // tm
    dff = lw["wup"].shape[1]
    mw = HY_WIDTH + ATTN_WIDTH
    const = lambda *shape: pl.BlockSpec(shape, lambda i: (0,) * len(shape), pipeline_mode=pl.Buffered(1))
    rows = lambda w: pl.BlockSpec((tm, w), lambda i: (i, 0))
    return pl.pallas_call(
        _mix_mlp_kernel, grid=(M // tm,),
        in_specs=[rows(D), pl.BlockSpec((None, HY_CB, DFT_G, tm // DFT_G, LANES),
                                        lambda i: (i // tps, 0, 0, i % tps, 0)),
                  rows(ATTN_WIDTH), const(1, HY_WIDTH), const(HY_WIDTH, HY_WIDTH),
                  const(mw, D), const(1, D), const(1, D), const(D, dff), const(dff, D), const(1, D)],
        out_specs=rows(D), out_shape=jax.ShapeDtypeStruct((M, D), F32),
        compiler_params=_cparams(("parallel",)), name="mix_mlp",
    )(x, zh, an, lw["g_hy"], tabs["gsum"], lw["wo"], lw["g_post"], lw["g_mpre"], lw["wup"], lw["wdn"],
      lw["g_mpost"])


def _tables(L):
    n = 2 * L
    n2 = DFT_N2
    n1 = n // n2
    nh = n1 // 2
    kp = -(-(nh + 1) // SUBLANES) * SUBLANES
    two_pi = 2.0 * math.pi

    k1 = jnp.arange(kp, dtype=jnp.int32)
    valid = (k1 <= nh)
    def outer(ncols):
        nn = jnp.arange(ncols, dtype=jnp.int32)
        ang = ((k1[:, None] * nn[None, :]) % n1).astype(F32) * (two_pi / n1)
        c = jnp.where(valid[:, None], jnp.cos(ang), 0.0)
        s = jnp.where(valid[:, None], -jnp.sin(ang), 0.0)
        return jnp.concatenate([c, s], axis=0).astype(BF16)
    f1d = outer(nh)
    f1f = outer(n1)
    nn = jnp.arange(nh, dtype=jnp.int32)
    ang = ((nn[:, None] * k1[None, :]) % n1).astype(F32) * (two_pi / n1)
    wgt = jnp.where(valid, jnp.where((k1 == 0) | (k1 == nh), 1.0, 2.0), 0.0) / n
    gre = (jnp.cos(ang) * wgt[None, :]).astype(BF16)
    gim = (-jnp.sin(ang) * wgt[None, :]).astype(BF16)
    a2 = jnp.arange(n2, dtype=jnp.int32)
    idx = (a2[None, :, None] * a2[None, None, :] * n1 + a2[None, None, :] * k1[:, None, None]) % n
    ph = idx.astype(F32) * (two_pi / n)
    gr, gi = jnp.cos(ph), -jnp.sin(ph)
    m1 = jnp.concatenate([jnp.concatenate([gr, -gi], axis=2), jnp.concatenate([gi, gr], axis=2)], axis=1)
    m2 = jnp.swapaxes(m1, 1, 2)
    inv = 1.0 / (ROPE_BASE ** (jnp.arange(0, QK_ROPE, 2, dtype=F32) / QK_ROPE))
    ang = jnp.arange(L, dtype=F32)[:, None] * inv[None, :]
    cos, sin = jnp.cos(ang), jnp.sin(ang)
    rcos = jnp.concatenate([cos, cos], axis=1)
    rsin = jnp.concatenate([sin, sin], axis=1)
    kcs =jnp.concatenate([cos, cos, sin, sin, jnp.zeros((L, HEAD_PAD - 2 * QK_ROPE), F32)], axis=1)
    cc = jnp.arange(MLA_HEADS * HEAD_PAD)
    src = jnp.arange(HEAD_PAD)
    pk = ((cc[None, :] % HEAD_PAD) - QK_NOPE == src[:, None]) & (src[:, None] < QK_ROPE)
    grp = jnp.arange(HY_WIDTH) // (HY_WIDTH // HY_GROUPS)
    gsum = (grp[:, None] == grp[None, :]).astype(BF16)
    t = jnp.linspace(0.0, 1.0, L, dtype=F32)[:, None]
    omega = (two_pi / L) * jnp.arange(L, dtype=F32)
    bands = jnp.linspace(1e-4, HY_BANDS - 1, HY_BANDS, dtype=F32)
    phase = omega[:, None] * bands[None, :]
    z = jnp.concatenate([t, jnp.cos(phase), -jnp.sin(phase), jnp.zeros((L, HY_EMB_PAD - HY_EMB), F32)], axis=-1)
    ztab = jnp.concatenate([z, z[:1], z[:0:-1]], axis=0)
    return dict(f1d=f1d, f1f=f1f, gre=gre, gim=gim, m1=m1.astype(BF16), m2=m2.astype(BF16), rcos_t=rcos.T,
                rsin_t=rsin.T, kcs=kcs, pk=pk.astype(BF16), gsum=gsum, ztab=ztab)


def _rot_half_cols(w):
    half = QK_ROPE // 2
    return jnp.concatenate([-w[..., half:], w[..., :half]], axis=-1)


def _layer_weights(i, p, tabs):
    D = p["w_in"].shape[1]
    H = MLA_HEADS
    hw3 = 3 * HY_WIDTH
    w_in = p["w_in"][i]
    kpe = w_in[:, hw3 + Q_RANK + KV_RANK:]
    win = jnp.concatenate([w_in, _rot_half_cols(kpe),
                           jnp.zeros((D, HEAD_PAD - 2 * QK_ROPE), F32)], axis=1).astype(BF16)
    dq = QK_NOPE + QK_ROPE
    wq = p["mla_w_uq"][i].reshape(Q_RANK, H, dq)
    wqt = jnp.concatenate([wq.reshape(Q_RANK, H * dq),
                           _rot_half_cols(wq[..., QK_NOPE:]).reshape(Q_RANK, H * QK_ROPE)], axis=1).T
    wkv =p["mla_w_ukv"][i].reshape(KV_RANK, H, QK_NOPE + V_HEAD)
    wk = jnp.concatenate([wkv[..., :QK_NOPE], jnp.zeros((KV_RANK, H, HEAD_PAD - QK_NOPE), F32)], axis=2)
    wv = jnp.concatenate([wkv[..., QK_NOPE:], jnp.zeros((KV_RANK, H, V_PAD - V_HEAD), F32)], axis=2)
    row = lambda a: a.reshape(1, -1)
    return dict(
        win=win, g_pre=row(p["norm_mix_pre"][i]), conv_w=p["hy_conv_w"][i], conv_b=row(p["hy_conv_b"][i]),
        q_g=row(p["mla_q_norm"][i]), wqt=wqt.astype(BF16), kv_g=row(p["mla_kv_norm"][i]),
        wkp=jnp.concatenate([wk.reshape(KV_RANK, H * HEAD_PAD).astype(BF16), tabs["pk"]], axis=0),
        wvt=wv.reshape(KV_RANK, H * V_PAD).T.astype(BF16),
        g_hy=row(p["grp_norm_hy"][i]), g_attn=row(p["grp_norm_attn"][i]), wo=p["w_out"][i].astype(BF16),
        g_post=row(p["norm_mix_post"][i]), g_mpre=row(p["norm_mlp_pre"][i]), g_mpost=row(p["norm_mlp_post"][i]),
        wup=p["w_mlp_up"][i].astype(BF16), wdn=p["w_mlp_down"][i].astype(BF16),
    )


def _filter_weights(p):
    depth = p["hy_ffn_w1"].shape[0]
    oc = HY_ORDER * HY_WIDTH
    w1 = jnp.pad(p["hy_ffn_w1"], ((0, 0), (0, HY_EMB_PAD - HY_EMB), (0, 0)))
    w3 = p["hy_ffn_w3"].reshape(depth, HY_FFN, HY_ORDER, 2, HY_WIDTH).transpose(0, 3, 1, 2, 4)
    dec = p["hy_decay"].transpose(0, 2, 1, 3).reshape(depth, 2, 1, oc)
    tr = lambda a: jnp.swapaxes(a, 1, 2)
    w3 = w3.reshape(depth, 2, HY_FFN, oc)
    w3_hi = w3.astype(BF16)
    w3_lo = (w3 - w3_hi.astype(F32)).astype(BF16)
    return dict(w1=tr(w1), b1=p["hy_ffn_b1"][:, :, None], sf=tr(p["hy_sin_freq"]), w2=tr(p["hy_ffn_w2"]),
                b2=p["hy_ffn_b2"][:, :, None], w3=jnp.stack([w3_hi, w3_lo], axis=2), dec=dec)


def kernel(x_prompt, x_sample, w_in, hy_conv_w, hy_conv_b, hy_ffn_w1, hy_ffn_b1, hy_ffn_w2, hy_ffn_b2,
           hy_ffn_w3, hy_sin_freq, hy_decay, hy_bias, mla_q_norm, mla_w_uq, mla_kv_norm, mla_w_ukv,
           grp_norm_hy, grp_norm_attn, w_out, norm_mix_pre, norm_mix_post, norm_mlp_pre, norm_mlp_post,
           w_mlp_up, w_mlp_down):
    p = dict(w_in=w_in, hy_conv_w=hy_conv_w, hy_conv_b=hy_conv_b, hy_ffn_w1=hy_ffn_w1, hy_ffn_b1=hy_ffn_b1,
             hy_ffn_w2=hy_ffn_w2, hy_ffn_b2=hy_ffn_b2, hy_ffn_w3=hy_ffn_w3, hy_sin_freq=hy_sin_freq,
             hy_decay=hy_decay, hy_bias=hy_bias, mla_q_norm=mla_q_norm, mla_w_uq=mla_w_uq,
             mla_kv_norm=mla_kv_norm, mla_w_ukv=mla_w_ukv, grp_norm_hy=grp_norm_hy,
             grp_norm_attn=grp_norm_attn, w_out=w_out, norm_mix_pre=norm_mix_pre, norm_mix_post=norm_mix_post,
             norm_mlp_pre=norm_mlp_pre, norm_mlp_post=norm_mlp_post, w_mlp_up=w_mlp_up, w_mlp_down=w_mlp_down)
    bp, L, D = x_prompt.shape
    bs, Ls, _ = x_sample.shape
    assert L == Ls and L % (DFT_N2 * SUBLANES) == 0
    nb = bp + bs
    depth = w_in.shape[0]

    tabs = _tables(L)
    kc = _filters(tabs["ztab"], _filter_weights(p), L)
    kf = _filter_spectrum(_dft1_pairs(kc, tabs["f1f"]), tabs["m1"])

    x = jnp.concatenate([x_prompt.reshape(bp * L, D), x_sample.reshape(bs * L, D)], axis=0)
    for i in range(depth):
        lw = _layer_weights(i, p, tabs)
        v, x1, x2, qt, k, vt, stats = _inproj(x, lw, tabs, nb, L)
        z = _hyena_convs(v, x1, x2, kf, hy_bias[i], tabs, i)
        an = _attention(_fast_flags(stats), qt, k, vt, lw["g_attn"], nb, L).reshape(nb * L, ATTN_WIDTH)
        x = _mix_mlp(x, z, an, lw, tabs, L)
    return (x[:bp * L].reshape(bp, L, D), x[bp * L:].reshape(bs, L, D))
```
